```python
import math
import jax
import jax.numpy as jnp
from jax import lax
import numpy as np

D_MODEL = 1024
BATCH = 8
SEQ = 2048
DEPTH = 4

CHUNK = 64
N_MIXERS = 3
N_LAYERS_A = (DEPTH + 2) // 3
N_LAYERS_B = (DEPTH + 1) // 3
N_LAYERS_C = DEPTH // 3
RMS_EPS = 1e-6
N_MOD = 6
D_FF = -(-8 * D_MODEL // (3 * 256)) * 256

M_D_INNER = 2 * D_MODEL
M_HEAD_DIM = 64
M_HEADS = M_D_INNER // M_HEAD_DIM
M_GROUPS = 8
M_D_STATE = 128
M_D_CONV = 4
M_CONV_DIM = M_D_INNER + 2 * M_GROUPS * M_D_STATE
M_D_IN_PROJ = M_D_INNER + M_CONV_DIM + M_HEADS

R_HEAD_DIM = 64
R_HEADS = D_MODEL // R_HEAD_DIM
R_DECAY_LORA = max(32, int(round(1.8 * D_MODEL ** 0.5 / 32)) * 32)
R_AAA_LORA = max(32, int(round(1.8 * D_MODEL ** 0.5 / 32)) * 32)
R_GATE_LORA = max(32, int(round(0.6 * D_MODEL ** 0.8 / 32)) * 32)
R_GN_EPS = 64e-5

F_HEAD_DIM = 64
F_HEADS = D_MODEL // F_HEAD_DIM
Q_BLOCK = 128

kernel_name = 'hybrid_ssd_rwkv7_fox_adaln_trunk'


def rms_norm(x, w, eps=RMS_EPS):
    xf = x.astype(jnp.float32)
    y = xf * lax.rsqrt(jnp.mean(xf * xf, axis=-1, keepdims=True) + eps)
    return (y * w.astype(jnp.float32)).astype(x.dtype)


def causal_depthwise_conv(u, w, b):
    k = w.shape[0]
    out = lax.conv_general_dilated(
        u, w[:, None, :], window_strides=(1,), padding=[(k - 1, 0)],
        dimension_numbers=('NWC', 'WIO', 'NWC'), feature_group_count=u.shape[-1])
    return out + b


def segsum(a):
    t = a.shape[-1]
    a_rep = jnp.broadcast_to(a[..., None], a.shape + (t,))
    a_rep = jnp.where(jnp.tril(jnp.ones((t, t), bool), -1), a_rep, 0.0)
    ss = jnp.cumsum(a_rep, axis=-2)
    return jnp.where(jnp.tril(jnp.ones((t, t), bool)), ss, -jnp.inf)


def ssd_chunked(x, dt, A, Bm, Cm):
    b, s, h, p = x.shape
    g, n = Bm.shape[-2:]
    j = h // g
    nc = s // CHUNK
    xd = (x.astype(jnp.float32) * dt[..., None]).reshape(b, nc, CHUNK, g, j, p)
    a = (dt * A).reshape(b, nc, CHUNK, g, j).transpose(0, 3, 4, 1, 2)
    Bc = Bm.astype(jnp.float32).reshape(b, nc, CHUNK, g, n)
    Cc = Cm.astype(jnp.float32).reshape(b, nc, CHUNK, g, n)
    a_cs = jnp.cumsum(a, axis=-1)
    Lmat = jnp.exp(segsum(a))
    scores = jnp.einsum('bclgn,bcmgn->bgclm', Cc, Bc)
    y_diag = jnp.einsum('bgclm,bgjclm,bcmgjp->bclgjp', scores, Lmat, xd)
    decay_states = jnp.exp(a_cs[..., -1:] - a_cs)
    states = jnp.einsum('bclgn,bgjcl,bclgjp->bcgjpn', Bc, decay_states, xd)
    chunk_decay = jnp.exp(a_cs[..., -1])

    def step(carry, inp):
        st, dec = inp
        return carry * dec[..., None, None] + st, carry

    init = jnp.zeros((b, g, j, p, n), jnp.float32)
    _, prev = lax.scan(step, init, (jnp.moveaxis(states, 1, 0), jnp.moveaxis(chunk_decay, -1, 0)))
    prev = jnp.moveaxis(prev, 0, 1)
    y_off = jnp.einsum('bclgn,bcgjpn,bgjcl->bclgjp', Cc, prev, jnp.exp(a_cs))
    return (y_diag + y_off).reshape(b, s, h, p).astype(x.dtype)


def mamba2_mixer(h, in_w, conv_w, conv_b, dt_bias, A_log, d_skip, norm_w, out_w):
    b, s, _ = h.shape
    zxbcdt = h @ in_w
    z, xbc, dt = jnp.split(zxbcdt, [M_D_INNER, M_D_INNER + M_CONV_DIM], axis=-1)
    xbc = jax.nn.silu(causal_depthwise_conv(xbc, conv_w, conv_b))
    xs, Bm, Cm = jnp.split(xbc, [M_D_INNER, M_D_INNER + M_GROUPS * M_D_STATE], axis=-1)
    xs = xs.reshape(b, s, M_HEADS, M_HEAD_DIM)
    Bm = Bm.reshape(b, s, M_GROUPS, M_D_STATE)
    Cm = Cm.reshape(b, s, M_GROUPS, M_D_STATE)
    dt = jax.nn.softplus((dt + dt_bias).astype(jnp.float32))
    A = -jnp.exp(A_log.astype(jnp.float32))
    y = ssd_chunked(xs, dt, A, Bm, Cm) + xs * d_skip[:, None]
    yg = (y.reshape(b, s, M_D_INNER) * jax.nn.silu(z)).reshape(b, s, M_GROUPS, -1)
    yg = rms_norm(yg, norm_w.reshape(M_GROUPS, -1)).reshape(b, s, M_D_INNER)
    return yg @ out_w


def rwkv7_mixer(h, mix, w_rkv, w0, w1, w2, a0, a1, a2, g1, g2, k_k, k_a, r_k, lnx_w, lnx_b, out_w):
    b, s, d = h.shape
    f32 = jnp.float32
    dx = jnp.pad(h, ((0, 0), (1, 0), (0, 0)))[:, :-1] - h
    xs = h[None] + dx[None] * mix[:, None, None, :]
    r, k, v = jnp.einsum('nbsd,nde->nbse', xs[:3], w_rkv)
    w_log = -jax.nn.softplus(-(w0 + jnp.tanh(xs[3] @ w1) @ w2)) - 0.5
    decay = jnp.exp(-jnp.exp(w_log.astype(f32)))
    a = jax.nn.sigmoid(a0 + (xs[4] @ a1) @ a2)
    g = jax.nn.sigmoid(xs[5] @ g1) @ g2

    def heads(t):
        return t.reshape(b, s, R_HEADS, R_HEAD_DIM).astype(f32)

    kk = heads(k * k_k)
    kk = kk / jnp.maximum(jnp.linalg.norm(kk, axis=-1, keepdims=True), 1e-12)
    k = k * (1 + (a - 1) * k_a)
    rh, kh, vh, ah, wh = heads(r), heads(k), heads(v), heads(a), heads(decay)

    def step(state, inp):
        r_t, w_t, k_t, v_t, kk_t, bb_t = inp
        sa = jnp.einsum('bhij,bhj->bhi', state, -kk_t)
        state = (state * w_t[:, :, None, :] + sa[..., None] * bb_t[:, :, None, :]
                 + v_t[..., None] * k_t[:, :, None, :])
        return state, jnp.einsum('bhij,bhj->bhi', state, r_t)

    tm = lambda t: jnp.moveaxis(t, 1, 0)
    s0 = jnp.zeros((b, R_HEADS, R_HEAD_DIM, R_HEAD_DIM), f32)
    _, y = lax.scan(step, s0, (tm(rh), tm(wh), tm(kh), tm(vh), tm(kk), tm(kk * ah)))
    y = jnp.moveaxis(y, 0, 1)
    mu = jnp.mean(y, axis=-1, keepdims=True)
    var = jnp.mean(jnp.square(y - mu), axis=-1, keepdims=True)
    y = ((y - mu) * lax.rsqrt(var + R_GN_EPS)).reshape(b, s, d) * lnx_w + lnx_b
    bonus = jnp.sum(rh * kh * r_k, axis=-1, keepdims=True) * vh
    y = (y + bonus.reshape(b, s, d)).astype(h.dtype)
    return (y * g) @ out_w


def fox_mixer(h, qkvf_w, fgate_b, q_norm_w, k_norm_w, out_w):
    b, s, d = h.shape
    proj = h @ qkvf_w
    q, k, v, f = jnp.split(proj, [d, 2 * d, 3 * d], axis=-1)
    q = rms_norm(q.reshape(b, s, F_HEADS, F_HEAD_DIM), q_norm_w).transpose(0, 2, 1, 3)
    k = rms_norm(k.reshape(b, s, F_HEADS, F_HEAD_DIM), k_norm_w).transpose(0, 2, 1, 3)
    v = v.reshape(b, s, F_HEADS, F_HEAD_DIM).transpose(0, 2, 1, 3)
    log_f = jax.nn.log_sigmoid((f + fgate_b).astype(jnp.float32))
    cum = jnp.cumsum(log_f, axis=1).transpose(0, 2, 1)
    scale = F_HEAD_DIM ** -0.5
    outs = []
    for qb in range(s // Q_BLOCK):
        q0 = qb * Q_BLOCK
        kend = q0 + Q_BLOCK
        logits = jnp.einsum('bhqd,bhkd->bhqk', q[:, :, q0:kend], k[:, :, :kend]).astype(jnp.float32) * scale
        logits = logits + cum[:, :, q0:kend, None] - cum[:, :, None, :kend]
        mask = (q0 + jnp.arange(Q_BLOCK))[:, None] >= jnp.arange(kend)[None, :]
        p = jax.nn.softmax(jnp.where(mask, logits, -jnp.inf), axis=-1)
        outs.append(jnp.einsum('bhqk,bhkd->bhqd', p.astype(v.dtype), v[:, :, :kend]))
    o = jnp.concatenate(outs, axis=2).transpose(0, 2, 1, 3).reshape(b, s, d)
    return o @ out_w


def swiglu_ffn(h, w1, w3, w2):
    return (jax.nn.silu(h @ w1) * (h @ w3)) @ w2


def setup_inputs(seed: int = 0) -> dict:
    key = jax.random.key(seed)
    keys = jax.random.split(key, 48)
    counter = [0]
    f32 = jnp.float32

    def nk():
        counter[0] += 1
        return keys[counter[0] - 1]

    def nrm(shape, scale):
        return jax.random.normal(nk(), shape, f32) * scale

    def unif(shape, lo, hi):
        return jax.random.uniform(nk(), shape, f32, lo, hi)

    D = D_MODEL
    na, nb, nc = N_LAYERS_A, N_LAYERS_B, N_LAYERS_C
    inv = lambda n: n ** -0.5
    x = nrm((BATCH, SEQ, D), 1.0)
    c = nrm((BATCH, D), 1.0)
    ada_w = nrm((DEPTH, D, N_MOD * D), 0.1 * inv(D))
    ada_b = nrm((DEPTH, N_MOD * D), 0.02)
    norm1_w = 1.0 + nrm((DEPTH, D), 0.02)
    norm2_w = 1.0 + nrm((DEPTH, D), 0.02)
    ffn_w1 = nrm((DEPTH, D, D_FF), inv(D))
    ffn_w3 = nrm((DEPTH, D, D_FF), inv(D))
    ffn_w2 = nrm((DEPTH, D_FF, D), inv(D_FF))
    m_in_w = nrm((na, D, M_D_IN_PROJ), inv(D))
    m_conv_w = nrm((na, M_D_CONV, M_CONV_DIM), inv(M_D_CONV))
    m_conv_b = nrm((na, M_CONV_DIM), 0.02)
    dt0 = jnp.exp(unif((na, M_HEADS), math.log(1e-3), math.log(1e-1)))
    m_dt_bias = dt0 + jnp.log(-jnp.expm1(-dt0))
    m_A_log = jnp.log(unif((na, M_HEADS), 1.0, 16.0))
    m_D = 1.0 + nrm((na, M_HEADS), 0.1)
    m_norm_w = 1.0 + nrm((na, M_D_INNER), 0.02)
    m_out_w = nrm((na, M_D_INNER, D), inv(M_D_INNER))
    r_mix = unif((nb, 6, D), 0.0, 1.0)
    r_w_rkv = nrm((nb, 3, D, D), inv(D))
    r_w0 = unif((nb, D), -6.0, 1.0)
    r_w1 = nrm((nb, D, R_DECAY_LORA), inv(D))
    r_w2 = nrm((nb, R_DECAY_LORA, D), 0.5 * inv(R_DECAY_LORA))
    r_a0 = nrm((nb, D), 0.1)
    r_a1 = nrm((nb, D, R_AAA_LORA), inv(D))
    r_a2 = nrm((nb, R_AAA_LORA, D), 0.5 * inv(R_AAA_LORA))
    r_g1 = nrm((nb, D, R_GATE_LORA), inv(D))
    r_g2 = nrm((nb, R_GATE_LORA, D), inv(R_GATE_LORA))
    r_k_k = 0.85 + nrm((nb, D), 0.02)
    r_k_a = 1.0 + nrm((nb, D), 0.02)
    r_r_k = nrm((nb, R_HEADS, R_HEAD_DIM), 0.1)
    r_lnx_w = 1.0 + nrm((nb, D), 0.02)
    r_lnx_b = nrm((nb, D), 0.02)
    r_out_w = nrm((nb, D, D), inv(D))
    f_qkvf_w = nrm((nc, D, 3 * D + F_HEADS), inv(D))
    f_fgate_b = unif((nc, F_HEADS), 1.0, 5.0)
    f_q_norm_w = 1.0 + nrm((nc, F_HEAD_DIM), 0.02)
    f_k_norm_w = 1.0 + nrm((nc, F_HEAD_DIM), 0.02)
    f_out_w = nrm((nc, D, D), inv(D))
    return {
        'x': x, 'c': c, 'ada_w': ada_w, 'ada_b': ada_b,
        'norm1_w': norm1_w, 'norm2_w': norm2_w,
        'ffn_w1': ffn_w1, 'ffn_w3': ffn_w3, 'ffn_w2': ffn_w2,
        'm_in_w': m_in_w, 'm_conv_w': m_conv_w, 'm_conv_b': m_conv_b,
        'm_dt_bias': m_dt_bias, 'm_A_log': m_A_log, 'm_D': m_D,
        'm_norm_w': m_norm_w, 'm_out_w': m_out_w,
        'r_mix': r_mix, 'r_w_rkv': r_w_rkv, 'r_w0': r_w0, 'r_w1': r_w1, 'r_w2': r_w2,
        'r_a0': r_a0, 'r_a1': r_a1, 'r_a2': r_a2, 'r_g1': r_g1, 'r_g2': r_g2,
        'r_k_k': r_k_k, 'r_k_a': r_k_a, 'r_r_k': r_r_k,
        'r_lnx_w': r_lnx_w, 'r_lnx_b': r_lnx_b, 'r_out_w': r_out_w,
        'f_qkvf_w': f_qkvf_w, 'f_fgate_b': f_fgate_b,
        'f_q_norm_w': f_q_norm_w, 'f_k_norm_w': f_k_norm_w, 'f_out_w': f_out_w,
    }


def reference(x, c, ada_w, ada_b, norm1_w, norm2_w, ffn_w1, ffn_w3, ffn_w2,
              m_in_w, m_conv_w, m_conv_b, m_dt_bias, m_A_log, m_D, m_norm_w, m_out_w,
              r_mix, r_w_rkv, r_w0, r_w1, r_w2, r_a0, r_a1, r_a2, r_g1, r_g2,
              r_k_k, r_k_a, r_r_k, r_lnx_w, r_lnx_b, r_out_w,
              f_qkvf_w, f_fgate_b, f_q_norm_w, f_k_norm_w, f_out_w):
    cond = jax.nn.silu(c)
    ia = ib = ic = 0
    for i in range(DEPTH):
        mod = (cond @ ada_w[i] + ada_b[i])[:, None, :]
        sh1, sc1, gt1, sh2, sc2, gt2 = jnp.split(mod, N_MOD, axis=-1)
        h = rms_norm(x, norm1_w[i]) * (1 + sc1) + sh1
        kind = i % N_MIXERS
        if kind == 0:
            y = mamba2_mixer(h, m_in_w[ia], m_conv_w[ia], m_conv_b[ia], m_dt_bias[ia],
                             m_A_log[ia], m_D[ia], m_norm_w[ia], m_out_w[ia])
            ia += 1
        elif kind == 1:
            y = rwkv7_mixer(h, r_mix[ib], r_w_rkv[ib], r_w0[ib], r_w1[ib], r_w2[ib],
                            r_a0[ib], r_a1[ib], r_a2[ib], r_g1[ib], r_g2[ib],
                            r_k_k[ib], r_k_a[ib], r_r_k[ib], r_lnx_w[ib], r_lnx_b[ib], r_out_w[ib])
            ib += 1
        else:
            y = fox_mixer(h, f_qkvf_w[ic], f_fgate_b[ic], f_q_norm_w[ic], f_k_norm_w[ic], f_out_w[ic])
            ic += 1
        x = x + (1 + gt1) * y
        h = rms_norm(x, norm2_w[i]) * (1 + sc2) + sh2
        x = x + (1 + gt2) * swiglu_ffn(h, ffn_w1[i], ffn_w3[i], ffn_w2[i])
    return x
```

```python
import functools

import jax
import jax.numpy as jnp
from jax import lax
from jax.experimental import pallas as pl
from jax.experimental.pallas import tpu as pltpu

F32 = jnp.float32
BF16 = jnp.bfloat16

RMS_EPS = 1e-6
GN_EPS = 64e-5
KK_EPS = 1e-12
NEG = -1e30

V7X_VMEM_LIMIT = 56 * 1024 * 1024

HEAD_DIM = 64
SSD_GROUPS = 8
SSD_STATE = 128
SSD_CHUNK = 128
WKV_CHUNK = 64
WKV_GROUP = 256
ROW_TILE = 256
LANE = 128


def _mm(a, b):
    return jnp.dot(a.astype(BF16), b.astype(BF16), preferred_element_type=F32)


def _mm_nt(a, b):
    return lax.dot_general(a.astype(BF16), b.astype(BF16), (((1,), (1,)), ((), ())),
                           preferred_element_type=F32)


def _mm_tn(a, b):
    return jnp.dot(a.T.astype(BF16), b.astype(BF16), preferred_element_type=F32)


def _split3(x):
    hi = x.astype(BF16)
    r1 = x - hi.astype(F32)
    mid = r1.astype(BF16)
    lo = (r1 - mid.astype(F32)).astype(BF16)
    return hi, mid, lo


def _mm_sel_l(sel, x):
    hi, mid, lo = _split3(x)
    d = lambda p: jnp.dot(sel, p, preferred_element_type=F32)
    return d(hi) + d(mid) + d(lo)


def _mm_sel_r(x, sel):
    hi, mid, lo = _split3(x)
    d = lambda p: jnp.dot(p, sel, preferred_element_type=F32)
    return d(hi) + d(mid) + d(lo)


def _mm_hi(a, b):
    ah = a.astype(BF16)
    al = (a - ah.astype(F32)).astype(BF16)
    bh = b.astype(BF16)
    bl = (b - bh.astype(F32)).astype(BF16)
    d = lambda p, q: jnp.dot(p, q, preferred_element_type=F32)
    return d(ah, bh) + d(ah, bl) + d(al, bh)


def _sigmoid(x):
    return jax.nn.sigmoid(x)


def _silu(x):
    return x * _sigmoid(x)


def _softplus(x):
    return jnp.maximum(x, 0.0) + jnp.log1p(jnp.exp(-jnp.abs(x)))


def _norm_mod(x, nw, scale, shift):
    y = x * lax.rsqrt(jnp.mean(x * x, axis=-1, keepdims=True) + RMS_EPS)
    return (y * nw) * (1.0 + scale) + shift


def _tril(n, strict=False):
    r = lax.broadcasted_iota(jnp.int32, (n, n), 0)
    c = lax.broadcasted_iota(jnp.int32, (n, n), 1)
    return (r > c) if strict else (r >= c)


def _resident(shape):
    nd = len(shape)
    return pl.BlockSpec(shape, lambda *_: (0,) * nd, pipeline_mode=pl.Buffered(1))


def _rows(t, width):
    return pl.BlockSpec((None, t, width), lambda b, s: (b, s, 0))


def _mod_spec(d):
    return pl.BlockSpec((None, 6, d), lambda b, s: (b, 0, 0))


def _params(seq_axis_carries):
    sem = ("parallel", "arbitrary") if seq_axis_carries else ("parallel", "parallel")
    return pltpu.CompilerParams(dimension_semantics=sem, vmem_limit_bytes=V7X_VMEM_LIMIT)


def _head_indicator(d, hd):
    ind = (jnp.arange(d)[:, None] // hd == jnp.arange(LANE)[None, :]).astype(BF16)
    return ind, ind.T


def _ada_kernel(c_ref, w_ref, b_ref, o_ref):
    c = c_ref[...]
    o_ref[...] = _mm_hi(_silu(c), w_ref[...]) + b_ref[...]


def _ada(c, ada_w, ada_b):
    depth, d, n6 = ada_w.shape
    b = c.shape[0]
    tn = 768
    out = pl.pallas_call(
        _ada_kernel,
        grid=(depth, n6 // tn),
        in_specs=[pl.BlockSpec((b, d), lambda l, j: (0, 0)),
                  pl.BlockSpec((None, d, tn), lambda l, j: (l, 0, j)),
                  pl.BlockSpec((None, 1, tn), lambda l, j: (l, 0, j))],
        out_specs=pl.BlockSpec((None, b, tn), lambda l, j: (l, 0, j)),
        out_shape=jax.ShapeDtypeStruct((depth, b, n6), F32),
        compiler_params=_params(False),
        name="ada",
    )(c, ada_w, ada_b.reshape(depth, 1, n6))
    return out.reshape(depth, b, 6, d)


def _ffn_kernel(x_ref, mod_ref, nw_ref, w1_ref, w3_ref, w2_ref, o_ref):
    x = x_ref[...]
    h = _norm_mod(x, nw_ref[...], mod_ref[4:5, :], mod_ref[3:4, :]).astype(BF16)
    a = jnp.dot(h, w1_ref[...], preferred_element_type=F32)
    b = jnp.dot(h, w3_ref[...], preferred_element_type=F32)
    g = (_silu(a) * b).astype(BF16)
    y = jnp.dot(g, w2_ref[...], preferred_element_type=F32)
    o_ref[...] = x + (1.0 + mod_ref[5:6, :]) * y


def _ffn(x, mod, nw, w1, w3, w2):
    b, s, d = x.shape
    dff = w1.shape[1]
    t = ROW_TILE
    return pl.pallas_call(
        _ffn_kernel,
        grid=(b, s // t),
        in_specs=[_rows(t, d), _mod_spec(d), _resident((1, d)),
                  _resident((d, dff)), _resident((d, dff)), _resident((dff, d))],
        out_specs=_rows(t, d),
        out_shape=jax.ShapeDtypeStruct(x.shape, F32),
        compiler_params=_params(False),
        name="ffn",
    )(x, mod, nw.reshape(1, d), w1.astype(BF16), w3.astype(BF16), w2.astype(BF16))


def _m_in_kernel(x_ref, mod_ref, nw_ref, wz_ref, wx_ref, wd_ref, z_ref, xbc_ref, dt_ref):
    h = _norm_mod(x_ref[...], nw_ref[...], mod_ref[1:2, :], mod_ref[0:1, :]).astype(BF16)
    z_ref[...] = jnp.dot(h, wz_ref[...], preferred_element_type=F32)
    xbc_ref[...] = jnp.dot(h, wx_ref[...], preferred_element_type=F32)
    dt_ref[...] = jnp.dot(h, wd_ref[...], preferred_element_type=F32)


def _m_ssd_kernel(xbc_ref, z_ref, dt_ref, x_ref, mod_ref, cw_ref, cb_ref, dtb_ref, alog_ref, e_ref,
                  dskip_ref, gnw_ref, ow_ref, o_ref, ubuf, st, ybuf, *, nh):
    t = x_ref.shape[0]
    di = z_ref.shape[1]
    ng, ns, gw = st.shape
    hpg = nh // ng
    hd = gw // hpg

    @pl.when(pl.program_id(1) == 0)
    def _():
        ubuf[0:8, :] = jnp.zeros((8, ubuf.shape[1]), F32)
        st[...] = jnp.zeros(st.shape, F32)

    u = xbc_ref[...]
    ubuf[8:8 + t, :] = u
    acc = cb_ref[...] + cw_ref[3:4, :] * u
    for k in range(3):
        acc = acc + cw_ref[k:k + 1, :] * ubuf[5 + k:5 + k + t, :]
    ubuf[0:8, :] = u[t - 8:t, :]
    xc = _silu(acc)
    xs = xc[:, :di]

    lane = lax.broadcasted_iota(jnp.int32, (1, LANE), 1)
    dt = _softplus(dt_ref[...] + dtb_ref[...])
    a_neg = jnp.where(lane < nh, -jnp.exp(alog_ref[...]), 0.0)
    a = dt * a_neg
    tril = _tril(t)
    cs = _mm_sel_l(jnp.where(tril, 1.0, 0.0).astype(BF16), a)
    cs_t = cs.T
    e = e_ref[...]
    dt_e = _mm_sel_r(dt, e)
    cs_e = _mm_sel_r(cs, e)
    cs_last = cs_e[t - 1:t, :]
    xd = xs * dt_e
    ecs = jnp.exp(cs_e)
    xdd = xd * jnp.exp(cs_last - cs_e)
    cdec = jnp.exp(cs_last)
    yskip = xs * dskip_ref[...]
    zz = z_ref[...]
    gate = _silu(zz)

    rr = lax.broadcasted_iota(jnp.int32, (hpg * t, gw), 0) // t
    ll = lax.broadcasted_iota(jnp.int32, (hpg * t, gw), 1) // hd
    bd = rr == ll
    for g in range(ng):
        lo = g * gw
        bg = xc[:, di + g * ns:di + (g + 1) * ns].astype(BF16)
        cg = xc[:, di + ng * ns + g * ns:di + ng * ns + (g + 1) * ns].astype(BF16)
        scores = _mm_nt(cg, bg)
        parts = []
        for j in range(hpg):
            h = g * hpg + j
            diff = cs[:, h:h + 1] - cs_t[h:h + 1, :]
            parts.append((scores * jnp.exp(jnp.where(tril, diff, NEG))).astype(BF16))
        lhs = jnp.concatenate(parts, axis=1)
        xdg = xd[:, lo:lo + gw]
        rhs = jnp.where(bd, jnp.concatenate([xdg] * hpg, axis=0), 0.0).astype(BF16)
        stg = st[g]
        y = jnp.dot(lhs, rhs, preferred_element_type=F32)
        y = y + _mm(cg, stg) * ecs[:, lo:lo + gw]
        st[g] = stg * cdec[:, lo:lo + gw] + _mm_tn(bg, xdd[:, lo:lo + gw])
        y = (y + yskip[:, lo:lo + gw]) * gate[:, lo:lo + gw]
        y = y * lax.rsqrt(jnp.mean(y * y, axis=-1, keepdims=True) + RMS_EPS) * gnw_ref[:, lo:lo + gw]
        ybuf[:, lo:lo + gw] = y.astype(BF16)

    out = jnp.dot(ybuf[...], ow_ref[...], preferred_element_type=F32)
    o_ref[...] = x_ref[...] + (1.0 + mod_ref[2:3, :]) * out


def _mamba(x, mod, nw, in_w, conv_w, conv_b, dt_bias, a_log, d_skip, norm_w, out_w):
    b, s, d = x.shape
    nh = dt_bias.shape[0]
    di = nh * HEAD_DIM
    cdim = conv_w.shape[1]
    ng, ns = SSD_GROUPS, SSD_STATE
    gw = di // ng
    assert cdim == di + 2 * ng * ns and in_w.shape[1] == di + cdim + nh and nh <= LANE
    wz = in_w[:, :di].astype(BF16)
    wx = in_w[:, di:di + cdim].astype(BF16)
    wd = jnp.pad(in_w[:, di + cdim:], ((0, 0), (0, LANE - nh))).astype(BF16)
    t = ROW_TILE
    z, xbc, dtr = pl.pallas_call(
        _m_in_kernel,
        grid=(b, s // t),
        in_specs=[_rows(t, d), _mod_spec(d), _resident((1, d)),
                  _resident((d, di)), _resident((d, cdim)), _resident((d, LANE))],
        out_specs=[_rows(t, di), _rows(t, cdim), _rows(t, LANE)],
        out_shape=[jax.ShapeDtypeStruct((b, s, di), F32), jax.ShapeDtypeStruct((b, s, cdim), F32),
                   jax.ShapeDtypeStruct((b, s, LANE), F32)],
        compiler_params=_params(False),
        name="m_in",
    )(x, mod, nw.reshape(1, d), wz, wx, wd)

    pad1 = lambda v: jnp.pad(v, (0, LANE - nh)).reshape(1, LANE)
    expand = (jnp.arange(LANE)[:, None] == jnp.arange(di)[None, :] // HEAD_DIM).astype(BF16)
    tc = SSD_CHUNK
    return pl.pallas_call(
        functools.partial(_m_ssd_kernel, nh=nh),
        grid=(b, s // tc),
        in_specs=[_rows(tc, cdim), _rows(tc, di), _rows(tc, LANE), _rows(tc, d), _mod_spec(d),
                  _resident((4, cdim)), _resident((1, cdim)), _resident((1, LANE)), _resident((1, LANE)),
                  _resident((LANE, di)), _resident((1, di)), _resident((1, di)), _resident((di, d))],
        out_specs=_rows(tc, d),
        out_shape=jax.ShapeDtypeStruct(x.shape, F32),
        scratch_shapes=[pltpu.VMEM((tc + 8, cdim), F32), pltpu.VMEM((ng, ns, gw), F32),
                        pltpu.VMEM((tc, di), BF16)],
        compiler_params=_params(True),
        name="m_ssd",
    )(xbc, z, dtr, x, mod, conv_w, conv_b.reshape(1, cdim), pad1(dt_bias), pad1(a_log), expand,
      jnp.repeat(d_skip, HEAD_DIM).reshape(1, di), norm_w.reshape(1, di), out_w.astype(BF16))


def _r_in_kernel(x_ref, mod_ref, nw_ref, mix_ref, wr_ref, wk_ref, wv_ref, w1_ref, w2_ref, a1_ref, a2_ref,
                 g1_ref, g2_ref, vec_ref, ind_ref, indt_ref,
                 r_ref, lw_ref, k_ref, v_ref, kk_ref, bb_ref, bonus_ref, g_ref, hlast):
    t = x_ref.shape[0]

    @pl.when(pl.program_id(1) == 0)
    def _():
        hlast[...] = jnp.zeros(hlast.shape, F32)

    h = _norm_mod(x_ref[...], nw_ref[...], mod_ref[1:2, :], mod_ref[0:1, :])
    row = lax.broadcasted_iota(jnp.int32, h.shape, 0)
    hprev = jnp.where(row == 0, hlast[7:8, :], pltpu.roll(h, 1, 0))
    hlast[...] = h[t - 8:t, :]
    dx = hprev - h
    mixed = lambda n: (h + dx * mix_ref[n:n + 1, :]).astype(BF16)
    w0, a0, k_k, k_a, r_k = (vec_ref[i:i + 1, :] for i in range(5))

    r = jnp.dot(mixed(0), wr_ref[...], preferred_element_type=F32)
    k = jnp.dot(mixed(1), wk_ref[...], preferred_element_type=F32)
    v = jnp.dot(mixed(2), wv_ref[...], preferred_element_type=F32)
    wl = _mm(jnp.tanh(jnp.dot(mixed(3), w1_ref[...], preferred_element_type=F32)), w2_ref[...])
    al = _mm(jnp.dot(mixed(4), a1_ref[...], preferred_element_type=F32), a2_ref[...])
    g = _mm(_sigmoid(jnp.dot(mixed(5), g1_ref[...], preferred_element_type=F32)), g2_ref[...])

    w_log = -_softplus(-(w0 + wl)) - 0.5
    a = _sigmoid(a0 + al)
    ind, indt = ind_ref[...], indt_ref[...]
    headsum = lambda q: _mm_sel_r(_mm_sel_r(q, ind), indt)
    kk = k * k_k
    kk = kk / jnp.maximum(jnp.sqrt(headsum(kk * kk)), KK_EPS)
    k2 = k * (1.0 + (a - 1.0) * k_a)

    r_ref[...] = r
    lw_ref[...] = -jnp.exp(w_log)
    k_ref[...] = k2
    v_ref[...] = v
    kk_ref[...] = kk
    bb_ref[...] = kk * a
    bonus_ref[...] = headsum(r * k2 * r_k) * v
    g_ref[...] = g


def _r_wkv_kernel(r_ref, lw_ref, k_ref, v_ref, kk_ref, bb_ref, bonus_ref, g_ref, x_ref, mod_ref,
                  lnw_ref, lnb_ref, ind_ref, indt_ref, ow_ref, o_ref, state, ybuf, *, hd):
    t = x_ref.shape[0]
    d = x_ref.shape[1]
    ngr, gw, _ = state.shape
    hpg = gw // hd
    n = hpg * t

    @pl.when(pl.program_id(1) == 0)
    def _():
        state[...] = jnp.zeros(state.shape, F32)

    lw = lw_ref[...]
    cl = _mm_sel_l(jnp.where(_tril(t), 1.0, 0.0).astype(BF16), lw)
    ecl = jnp.exp(cl)
    encl = jnp.exp(-cl)
    rt = r_ref[...] * ecl
    at = -kk_ref[...] * jnp.exp(cl - lw)
    kt = k_ref[...] * encl
    bt = bb_ref[...] * encl
    v = v_ref[...]
    ptot = ecl[t - 1:t, :]

    ri = lax.broadcasted_iota(jnp.int32, (n, gw), 0)
    ci = lax.broadcasted_iota(jnp.int32, (n, gw), 1)
    headm = (ri // t) == (ci // hd)
    ri2 = lax.broadcasted_iota(jnp.int32, (n, n), 0)
    ci2 = lax.broadcasted_iota(jnp.int32, (n, n), 1)
    same = (ri2 // t) == (ci2 // t)
    bd_strict = same & ((ri2 % t) > (ci2 % t))
    bd_incl = same & ((ri2 % t) >= (ci2 % t))
    eye = jnp.where(ri2 == ci2, 1.0, 0.0)

    for g in range(ngr):
        sl = slice(g * gw, (g + 1) * gw)
        stack = lambda m: jnp.concatenate([m[:, sl]] * hpg, axis=0)
        yb = stack(bt)
        yk = stack(kt)
        xa = jnp.where(headm, stack(at), 0.0)
        xr = jnp.where(headm, stack(rt), 0.0)
        xb = jnp.where(headm, yb, 0.0)
        xk = jnp.where(headm, yk, 0.0)
        vst = jnp.where(headm, stack(v), 0.0)
        a_ab = jnp.where(bd_strict, _mm_nt(xa, yb), 0.0)
        a_ak = jnp.where(bd_strict, _mm_nt(xa, yk), 0.0)
        r_b = jnp.where(bd_incl, _mm_nt(xr, yb), 0.0)
        r_k = jnp.where(bd_incl, _mm_nt(xr, yk), 0.0)
        hg = state[g]
        rhs = _mm_nt(xa, hg) + _mm(a_ak, vst)
        p = a_ab
        tinv = eye + p
        steps = max(t - 1, 1).bit_length() - 1
        for _ in range(steps):
            p = _mm(p, p)
            tinv = tinv + _mm(tinv, p)
        u = _mm(tinv, rhs)
        yst = _mm_nt(xr, hg) + _mm(r_b, u) + _mm(r_k, vst)
        y = yst[0:t]
        for j in range(1, hpg):
            y = y + yst[j * t:(j + 1) * t]
        state[g] = (hg + _mm_tn(u, xb) + _mm_tn(vst, xk)) * ptot[:, sl]
        ybuf[:, sl] = y

    y = ybuf[...]
    ind, indt = ind_ref[...], indt_ref[...]
    headmean = lambda q: _mm_sel_r(_mm_sel_r(q, ind), indt) * (1.0 / hd)
    mu = headmean(y)
    yc = y - mu
    var = headmean(yc * yc)
    yn = yc * lax.rsqrt(var + GN_EPS) * lnw_ref[...] + lnb_ref[...]
    yo = (yn + bonus_ref[...]) * g_ref[...]
    out = _mm(yo, ow_ref[...])
    o_ref[...] = x_ref[...] + (1.0 + mod_ref[2:3, :]) * out


def _rwkv(x, mod, nw, mix, w_rkv, w0, w1, w2, a0, a1, a2, g1, g2, k_k, k_a, r_k, lnx_w, lnx_b, out_w):
    b, s, d = x.shape
    hd = HEAD_DIM
    assert d % WKV_GROUP == 0 and d // hd <= LANE

    def pad_pair(p1, p2):
        r = p1.shape[1]
        rp = -(-r // LANE) * LANE
        return (jnp.pad(p1, ((0, 0), (0, rp - r))).astype(BF16), jnp.pad(p2, ((0, rp - r), (0, 0))).astype(BF16))

    w1p, w2p = pad_pair(w1, w2)
    a1p, a2p = pad_pair(a1, a2)
    g1p, g2p = pad_pair(g1, g2)
    vecs = jnp.stack([w0, a0, k_k, k_a, r_k.reshape(d), jnp.zeros_like(w0), jnp.zeros_like(w0), jnp.zeros_like(w0)])
    ind, indt = _head_indicator(d, hd)
    wb = w_rkv.astype(BF16)
    t = ROW_TILE
    res = lambda arr: _resident(arr.shape)
    outs = pl.pallas_call(
        _r_in_kernel,
        grid=(b, s // t),
        in_specs=[_rows(t, d), _mod_spec(d), _resident((1, d)), _resident((6, d)),
                  _resident((d, d)), _resident((d, d)), _resident((d, d)),
                  res(w1p), res(w2p), res(a1p), res(a2p), res(g1p), res(g2p),
                  _resident((8, d)), res(ind), res(indt)],
        out_specs=[_rows(t, d)] * 8,
        out_shape=[jax.ShapeDtypeStruct((b, s, d), F32)] * 8,
        scratch_shapes=[pltpu.VMEM((8, d), F32)],
        compiler_params=_params(True),
        name="r_in",
    )(x, mod, nw.reshape(1, d), mix, wb[0], wb[1], wb[2], w1p, w2p, a1p, a2p, g1p, g2p, vecs, ind, indt)

    tc = WKV_CHUNK
    return pl.pallas_call(
        functools.partial(_r_wkv_kernel, hd=hd),
        grid=(b, s // tc),
        in_specs=[_rows(tc, d)] * 9 + [_mod_spec(d), _resident((1, d)), _resident((1, d)),
                                       res(ind), res(indt), _resident((d, d))],
        out_specs=_rows(tc, d),
        out_shape=jax.ShapeDtypeStruct(x.shape, F32),
        scratch_shapes=[pltpu.VMEM((d // WKV_GROUP, WKV_GROUP, WKV_GROUP), F32), pltpu.VMEM((tc, d), F32)],
        compiler_params=_params(True),
        name="r_wkv",
    )(*outs, x, mod, lnx_w.reshape(1, d), lnx_b.reshape(1, d), ind, indt, out_w.astype(BF16))


def _f_in_kernel(x_ref, mod_ref, nw_ref, wq_ref, wk_ref, wv_ref, wf_ref, vec_ref, fb_ref, ind_ref, indt_ref,
                 q_ref, k_ref, v_ref, cum_ref, cumt_ref, carry, *, hd):
    t = x_ref.shape[0]

    @pl.when(pl.program_id(1) == 0)
    def _():
        carry[...] = jnp.zeros(carry.shape, F32)

    h = _norm_mod(x_ref[...], nw_ref[...], mod_ref[1:2, :], mod_ref[0:1, :]).astype(BF16)
    ind, indt = ind_ref[...], indt_ref[...]

    def head_rms(q, w):
        ms = _mm_sel_r(_mm_sel_r(q * q, ind), indt) * (1.0 / hd)
        return q * lax.rsqrt(ms + RMS_EPS) * w

    q = head_rms(jnp.dot(h, wq_ref[...], preferred_element_type=F32), vec_ref[0:1, :])
    k = head_rms(jnp.dot(h, wk_ref[...], preferred_element_type=F32), vec_ref[1:2, :])
    q_ref[...] = (q * (hd ** -0.5)).astype(BF16)
    k_ref[...] = k.astype(BF16)
    v_ref[...] = jnp.dot(h, wv_ref[...], preferred_element_type=F32).astype(BF16)
    f = jnp.dot(h, wf_ref[...], preferred_element_type=F32) + fb_ref[...]
    logf = -_softplus(-f)
    cum = _mm_sel_l(jnp.where(_tril(t), 1.0, 0.0).astype(BF16), logf) + carry[0:1, :]
    carry[...] = jnp.broadcast_to(cum[t - 1:t, :], carry.shape)
    cum_ref[...] = cum
    cumt_ref[...] = cum.T


def _f_att_kernel(q_ref, k_ref, v_ref, cum_ref, cumt_ref, x_ref, mod_ref, ow_ref, o_ref, obuf, *, hd):
    tq = x_ref.shape[0]
    d = x_ref.shape[1]
    tk = cumt_ref.shape[2]
    qi = pl.program_id(1)
    lane = lax.broadcasted_iota(jnp.int32, (1, LANE), 1)
    first = lane < hd
    rowpos = qi * tq + lax.broadcasted_iota(jnp.int32, (tq, tk), 0)
    colidx = lax.broadcasted_iota(jnp.int32, (tq, tk), 1)
    zero_b = jnp.zeros((), BF16)

    for p in range(d // LANE):
        ls = slice(p * LANE, (p + 1) * LANE)
        qp = q_ref[:, ls]
        qs = (jnp.where(first, qp, zero_b), jnp.where(first, zero_b, qp))
        cq = (cum_ref[:, 2 * p:2 * p + 1], cum_ref[:, 2 * p + 1:2 * p + 2])

        def body(j, carry, ls=ls, qs=qs, cq=cq, p=p):
            start = pl.multiple_of(j * tk, tk)
            kj = k_ref[pl.ds(start, tk), ls]
            vj = v_ref[pl.ds(start, tk), ls]
            causal = rowpos >= j * tk + colidx
            new = []
            for i in range(2):
                m, l, acc = carry[i]
                ck = cumt_ref[2 * p + i, pl.ds(j, 1), :]
                sc = lax.dot_general(qs[i], kj, (((1,), (1,)), ((), ())), preferred_element_type=F32)
                sc = jnp.where(causal, sc + cq[i] - ck, NEG)
                m_new = jnp.maximum(m, jnp.max(sc, axis=-1, keepdims=True))
                alpha = jnp.exp(m - m_new)
                pr = jnp.exp(sc - m_new)
                l = alpha * l + jnp.sum(pr, axis=-1, keepdims=True)
                acc = alpha * acc + jnp.dot(pr.astype(BF16), vj, preferred_element_type=F32)
                new.append((m_new, l, acc))
            return tuple(new)

        init = tuple((jnp.full((tq, 1), NEG, F32), jnp.zeros((tq, 1), F32), jnp.zeros((tq, LANE), F32))
                     for _ in range(2))
        (m0, l0, acc0), (m1, l1, acc1) = lax.fori_loop(0, qi + 1, body, init)
        obuf[:, ls] = jnp.where(first, acc0 / l0, acc1 / l1).astype(BF16)

    out = jnp.dot(obuf[...], ow_ref[...], preferred_element_type=F32)
    o_ref[...] = x_ref[...] + (1.0 + mod_ref[2:3, :]) * out


def _fox(x, mod, nw, qkvf_w, fgate_b, q_norm_w, k_norm_w, out_w):
    b, s, d = x.shape
    hd = HEAD_DIM
    nh = d // hd
    assert qkvf_w.shape[1] == 3 * d + nh and nh <= LANE and 2 * hd == LANE
    wq, wk, wv = (qkvf_w[:, i * d:(i + 1) * d].astype(BF16) for i in range(3))
    wf = jnp.pad(qkvf_w[:, 3 * d:], ((0, 0), (0, LANE - nh))).astype(BF16)
    fb = jnp.pad(fgate_b, (0, LANE - nh)).reshape(1, LANE)
    z = jnp.zeros((d,), F32)
    vecs = jnp.stack([jnp.tile(q_norm_w, nh), jnp.tile(k_norm_w, nh), z, z, z, z, z, z])
    ind, indt = _head_indicator(d, hd)
    t = ROW_TILE
    res = lambda arr: _resident(arr.shape)
    q, k, v, cum, cumt = pl.pallas_call(
        functools.partial(_f_in_kernel, hd=hd),
        grid=(b, s // t),
        in_specs=[_rows(t, d), _mod_spec(d), _resident((1, d)), _resident((d, d)), _resident((d, d)),
                  _resident((d, d)), _resident((d, LANE)), _resident((8, d)), _resident((1, LANE)),
                  res(ind), res(indt)],
        out_specs=[_rows(t, d)] * 3 + [_rows(t, LANE), pl.BlockSpec((None, LANE, t), lambda bi, si: (bi, 0, si))],
        out_shape=[jax.ShapeDtypeStruct((b, s, d), BF16)] * 3
        + [jax.ShapeDtypeStruct((b, s, LANE), F32), jax.ShapeDtypeStruct((b, LANE, s), F32)],
        scratch_shapes=[pltpu.VMEM((8, LANE), F32)],
        compiler_params=_params(True),
        name="f_in",
    )(x, mod, nw.reshape(1, d), wq, wk, wv, wf, vecs, fb, ind, indt)

    cumt4 = cumt[:, :nh, :].reshape(b, nh, s // t, t)
    whole = lambda width: pl.BlockSpec((None, s, width), lambda bi, si: (bi, 0, 0))
    return pl.pallas_call(
        functools.partial(_f_att_kernel, hd=hd),
        grid=(b, s // t),
        in_specs=[_rows(t, d), whole(d), whole(d), _rows(t, LANE),
                  pl.BlockSpec((None, nh, s // t, t), lambda bi, si: (bi, 0, 0, 0)),
                  _rows(t, d), _mod_spec(d), _resident((d, d))],
        out_specs=_rows(t, d),
        out_shape=jax.ShapeDtypeStruct(x.shape, F32),
        scratch_shapes=[pltpu.VMEM((t, d), BF16)],
        compiler_params=_params(False),
        name="f_att",
    )(q, k, v, cum, cumt4, x, mod, out_w.astype(BF16))


def kernel(x, c, ada_w, ada_b, norm1_w, norm2_w, ffn_w1, ffn_w3, ffn_w2, m_in_w, m_conv_w, m_conv_b, m_dt_bias, m_A_log, m_D, m_norm_w, m_out_w, r_mix, r_w_rkv, r_w0, r_w1, r_w2, r_a0, r_a1, r_a2, r_g1, r_g2, r_k_k, r_k_a, r_r_k, r_lnx_w, r_lnx_b, r_out_w, f_qkvf_w, f_fgate_b, f_q_norm_w, f_k_norm_w, f_out_w):
    depth = ada_w.shape[0]
    mod = _ada(c, ada_w, ada_b)
    ia = ib = ic = 0
    for i in range(depth):
        kind = i % 3
        if kind == 0:
            x = _mamba(x, mod[i], norm1_w[i], m_in_w[ia], m_conv_w[ia], m_conv_b[ia], m_dt_bias[ia],
                       m_A_log[ia], m_D[ia], m_norm_w[ia], m_out_w[ia])
            ia += 1
        elif kind == 1:
            x = _rwkv(x, mod[i], norm1_w[i], r_mix[ib], r_w_rkv[ib], r_w0[ib], r_w1[ib], r_w2[ib], r_a0[ib],
                      r_a1[ib], r_a2[ib], r_g1[ib], r_g2[ib], r_k_k[ib], r_k_a[ib], r_r_k[ib],
                      r_lnx_w[ib], r_lnx_b[ib], r_out_w[ib])
            ib += 1
        else:
            x = _fox(x, mod[i], norm1_w[i], f_qkvf_w[ic], f_fgate_b[ic], f_q_norm_w[ic], f_k_norm_w[ic],
                     f_out_w[ic])
            ic += 1
        x = _ffn(x, mod[i], norm2_w[i], ffn_w1[i], ffn_w3[i], ffn_w2[i])
    return x
```

```python
import functools

import jax
import jax.numpy as jnp
from jax import lax
from jax.experimental import pallas as pl
from jax.experimental.pallas import tpu as pltpu

F32 = jnp.float32
BF16 = jnp.bfloat16

RMS_EPS = 1e-6
GN_EPS = 64e-5
KK_EPS = 1e-12
NEG = -1e30

V7X_VMEM_LIMIT = 56 * 1024 * 1024

HEAD_DIM = 64
SSD_GROUPS = 8
SSD_STATE = 128
SSD_CHUNK = 128
WKV_CHUNK = 64
WKV_GROUP = 256
ROW_TILE = 256
ATT_PAIRS_PER_LOOP = 4
LANE = 128


def _mm(a, b):
    return jnp.dot(a.astype(BF16), b.astype(BF16), preferred_element_type=F32)


def _mm_nt(a, b):
    return lax.dot_general(a.astype(BF16), b.astype(BF16), (((1,), (1,)), ((), ())),
                           preferred_element_type=F32)


def _mm_tn(a, b):
    return jnp.dot(a.T.astype(BF16), b.astype(BF16), preferred_element_type=F32)


def _split3(x):
    hi = x.astype(BF16)
    r1 = x - hi.astype(F32)
    mid = r1.astype(BF16)
    lo = (r1 - mid.astype(F32)).astype(BF16)
    return hi, mid, lo


def _mm_sel_l(sel, x):
    hi, mid, lo = _split3(x)
    d = lambda p: jnp.dot(sel, p, preferred_element_type=F32)
    return d(hi) + d(mid) + d(lo)


def _mm_sel_r(x, sel):
    hi, mid, lo = _split3(x)
    d = lambda p: jnp.dot(p, sel, preferred_element_type=F32)
    return d(hi) + d(mid) + d(lo)


def _mm_hi(a, b):
    ah = a.astype(BF16)
    al = (a - ah.astype(F32)).astype(BF16)
    bh = b.astype(BF16)
    bl = (b - bh.astype(F32)).astype(BF16)
    d = lambda p, q: jnp.dot(p, q, preferred_element_type=F32)
    return d(ah, bh) + d(ah, bl) + d(al, bh)


def _sigmoid(x):
    return jax.nn.sigmoid(x)


def _silu(x):
    return x * _sigmoid(x)


def _softplus(x):
    return jnp.maximum(x, 0.0) + jnp.log1p(jnp.exp(-jnp.abs(x)))


def _norm_mod(x, nw, scale, shift):
    y = x * lax.rsqrt(jnp.mean(x * x, axis=-1, keepdims=True) + RMS_EPS)
    return (y * nw) * (1.0 + scale) + shift


def _tril(n, strict=False):
    r = lax.broadcasted_iota(jnp.int32, (n, n), 0)
    c = lax.broadcasted_iota(jnp.int32, (n, n), 1)
    return (r > c) if strict else (r >= c)


def _resident(shape):
    nd = len(shape)
    return pl.BlockSpec(shape, lambda *_: (0,) * nd, pipeline_mode=pl.Buffered(1))


def _rows(t, width):
    return pl.BlockSpec((None, t, width), lambda b, s: (b, s, 0))


def _mod_spec(d):
    return pl.BlockSpec((None, 6, d), lambda b, s: (b, 0, 0))


def _params(seq_axis_carries):
    sem = ("parallel", "arbitrary") if seq_axis_carries else ("parallel", "parallel")
    return pltpu.CompilerParams(dimension_semantics=sem, vmem_limit_bytes=V7X_VMEM_LIMIT)


def _head_indicator(d, hd):
    ind = (jnp.arange(d)[:, None] // hd == jnp.arange(LANE)[None, :]).astype(BF16)
    return ind, ind.T


def _ada_kernel(c_ref, w_ref, b_ref, o_ref):
    c = c_ref[...]
    o_ref[...] = _mm_hi(_silu(c), w_ref[...]) + b_ref[...]


def _ada(c, ada_w, ada_b):
    depth, d, n6 = ada_w.shape
    b = c.shape[0]
    tn = 768
    out = pl.pallas_call(
        _ada_kernel,
        grid=(depth, n6 // tn),
        in_specs=[pl.BlockSpec((b, d), lambda l, j: (0, 0)),
                  pl.BlockSpec((None, d, tn), lambda l, j: (l, 0, j)),
                  pl.BlockSpec((None, 1, tn), lambda l, j: (l, 0, j))],
        out_specs=pl.BlockSpec((None, b, tn), lambda l, j: (l, 0, j)),
        out_shape=jax.ShapeDtypeStruct((depth, b, n6), F32),
        compiler_params=_params(False),
        name="ada",
    )(c, ada_w, ada_b.reshape(depth, 1, n6))
    return out.reshape(depth, b, 6, d)


def _ffn_kernel(x_ref, mod_ref, nw_ref, w1_ref, w3_ref, w2_ref, o_ref):
    x = x_ref[...]
    h = _norm_mod(x, nw_ref[...], mod_ref[4:5, :], mod_ref[3:4, :]).astype(BF16)
    a = jnp.dot(h, w1_ref[...], preferred_element_type=F32)
    b = jnp.dot(h, w3_ref[...], preferred_element_type=F32)
    g = (_silu(a) * b).astype(BF16)
    y = jnp.dot(g, w2_ref[...], preferred_element_type=F32)
    o_ref[...] = x + (1.0 + mod_ref[5:6, :]) * y


def _ffn(x, mod, nw, w1, w3, w2):
    b, s, d = x.shape
    dff = w1.shape[1]
    t = ROW_TILE
    return pl.pallas_call(
        _ffn_kernel,
        grid=(b, s // t),
        in_specs=[_rows(t, d), _mod_spec(d), _resident((1, d)),
                  _resident((d, dff)), _resident((d, dff)), _resident((dff, d))],
        out_specs=_rows(t, d),
        out_shape=jax.ShapeDtypeStruct(x.shape, F32),
        compiler_params=_params(False),
        name="ffn",
    )(x, mod, nw.reshape(1, d), w1.astype(BF16), w3.astype(BF16), w2.astype(BF16))


def _m_in_kernel(x_ref, mod_ref, nw_ref, wz_ref, wx_ref, wd_ref, z_ref, xbc_ref, dt_ref):
    h = _norm_mod(x_ref[...], nw_ref[...], mod_ref[1:2, :], mod_ref[0:1, :]).astype(BF16)
    z_ref[...] = jnp.dot(h, wz_ref[...], preferred_element_type=F32)
    xbc_ref[...] = jnp.dot(h, wx_ref[...], preferred_element_type=F32)
    dt_ref[...] = jnp.dot(h, wd_ref[...], preferred_element_type=F32)


def _m_ssd_kernel(xbc_ref, z_ref, dt_ref, x_ref, mod_ref, cw_ref, cb_ref, dtb_ref, alog_ref, e_ref,
                  dskip_ref, gnw_ref, ow_ref, o_ref, ubuf, st, ybuf, *, nh):
    t = x_ref.shape[0]
    di = z_ref.shape[1]
    ng, ns, gw = st.shape
    hpg = nh // ng
    hd = gw // hpg

    @pl.when(pl.program_id(1) == 0)
    def _():
        ubuf[0:8, :] = jnp.zeros((8, ubuf.shape[1]), F32)
        st[...] = jnp.zeros(st.shape, F32)

    u = xbc_ref[...]
    ubuf[8:8 + t, :] = u
    acc = cb_ref[...] + cw_ref[3:4, :] * u
    for k in range(3):
        acc = acc + cw_ref[k:k + 1, :] * ubuf[5 + k:5 + k + t, :]
    ubuf[0:8, :] = u[t - 8:t, :]
    xc = _silu(acc)
    xs = xc[:, :di]

    lane = lax.broadcasted_iota(jnp.int32, (1, LANE), 1)
    dt = _softplus(dt_ref[...] + dtb_ref[...])
    a_neg = jnp.where(lane < nh, -jnp.exp(alog_ref[...]), 0.0)
    a = dt * a_neg
    tril = _tril(t)
    cs = _mm_sel_l(jnp.where(tril, 1.0, 0.0).astype(BF16), a)
    cs_t = cs.T
    e = e_ref[...]
    dt_e = _mm_sel_r(dt, e)
    cs_e = _mm_sel_r(cs, e)
    cs_last = cs_e[t - 1:t, :]
    xd = xs * dt_e
    ecs = jnp.exp(cs_e)
    xdd = xd * jnp.exp(cs_last - cs_e)
    cdec = jnp.exp(cs_last)
    yskip = xs * dskip_ref[...]
    zz = z_ref[...]
    gate = _silu(zz)

    rr = lax.broadcasted_iota(jnp.int32, (hpg * t, gw), 0) // t
    ll = lax.broadcasted_iota(jnp.int32, (hpg * t, gw), 1) // hd
    bd = rr == ll
    for g in range(ng):
        lo = g * gw
        bg = xc[:, di + g * ns:di + (g + 1) * ns].astype(BF16)
        cg = xc[:, di + ng * ns + g * ns:di + ng * ns + (g + 1) * ns].astype(BF16)
        scores = _mm_nt(cg, bg)
        parts = []
        for j in range(hpg):
            h = g * hpg + j
            diff = cs[:, h:h + 1] - cs_t[h:h + 1, :]
            parts.append((scores * jnp.exp(jnp.where(tril, diff, NEG))).astype(BF16))
        lhs = jnp.concatenate(parts, axis=1)
        xdg = xd[:, lo:lo + gw]
        rhs = jnp.where(bd, jnp.concatenate([xdg] * hpg, axis=0), 0.0).astype(BF16)
        stg = st[g]
        y = jnp.dot(lhs, rhs, preferred_element_type=F32)
        y = y + _mm(cg, stg) * ecs[:, lo:lo + gw]
        st[g] = stg * cdec[:, lo:lo + gw] + _mm_tn(bg, xdd[:, lo:lo + gw])
        y = (y + yskip[:, lo:lo + gw]) * gate[:, lo:lo + gw]
        y = y * lax.rsqrt(jnp.mean(y * y, axis=-1, keepdims=True) + RMS_EPS) * gnw_ref[:, lo:lo + gw]
        ybuf[:, lo:lo + gw] = y.astype(BF16)

    out = jnp.dot(ybuf[...], ow_ref[...], preferred_element_type=F32)
    o_ref[...] = x_ref[...] + (1.0 + mod_ref[2:3, :]) * out


def _mamba(x, mod, nw, in_w, conv_w, conv_b, dt_bias, a_log, d_skip, norm_w, out_w):
    b, s, d = x.shape
    nh = dt_bias.shape[0]
    di = nh * HEAD_DIM
    cdim = conv_w.shape[1]
    ng, ns = SSD_GROUPS, SSD_STATE
    gw = di // ng
    assert cdim == di + 2 * ng * ns and in_w.shape[1] == di + cdim + nh and nh <= LANE
    wz = in_w[:, :di].astype(BF16)
    wx = in_w[:, di:di + cdim].astype(BF16)
    wd = jnp.pad(in_w[:, di + cdim:], ((0, 0), (0, LANE - nh))).astype(BF16)
    t = ROW_TILE
    z, xbc, dtr = pl.pallas_call(
        _m_in_kernel,
        grid=(b, s // t),
        in_specs=[_rows(t, d), _mod_spec(d), _resident((1, d)),
                  _resident((d, di)), _resident((d, cdim)), _resident((d, LANE))],
        out_specs=[_rows(t, di), _rows(t, cdim), _rows(t, LANE)],
        out_shape=[jax.ShapeDtypeStruct((b, s, di), F32), jax.ShapeDtypeStruct((b, s, cdim), F32),
                   jax.ShapeDtypeStruct((b, s, LANE), F32)],
        compiler_params=_params(False),
        name="m_in",
    )(x, mod, nw.reshape(1, d), wz, wx, wd)

    pad1 = lambda v: jnp.pad(v, (0, LANE - nh)).reshape(1, LANE)
    expand = (jnp.arange(LANE)[:, None] == jnp.arange(di)[None, :] // HEAD_DIM).astype(BF16)
    tc = SSD_CHUNK
    return pl.pallas_call(
        functools.partial(_m_ssd_kernel, nh=nh),
        grid=(b, s // tc),
        in_specs=[_rows(tc, cdim), _rows(tc, di), _rows(tc, LANE), _rows(tc, d), _mod_spec(d),
                  _resident((4, cdim)), _resident((1, cdim)), _resident((1, LANE)), _resident((1, LANE)),
                  _resident((LANE, di)), _resident((1, di)), _resident((1, di)), _resident((di, d))],
        out_specs=_rows(tc, d),
        out_shape=jax.ShapeDtypeStruct(x.shape, F32),
        scratch_shapes=[pltpu.VMEM((tc + 8, cdim), F32), pltpu.VMEM((ng, ns, gw), F32),
                        pltpu.VMEM((tc, di), BF16)],
        compiler_params=_params(True),
        name="m_ssd",
    )(xbc, z, dtr, x, mod, conv_w, conv_b.reshape(1, cdim), pad1(dt_bias), pad1(a_log), expand,
      jnp.repeat(d_skip, HEAD_DIM).reshape(1, di), norm_w.reshape(1, di), out_w.astype(BF16))


def _r_in_kernel(x_ref, mod_ref, nw_ref, mix_ref, wr_ref, wk_ref, wv_ref, w1_ref, w2_ref, a1_ref, a2_ref,
                 g1_ref, g2_ref, vec_ref, ind_ref, indt_ref,
                 r_ref, lw_ref, k_ref, v_ref, kk_ref, bb_ref, bonus_ref, g_ref, hlast):
    t = x_ref.shape[0]

    @pl.when(pl.program_id(1) == 0)
    def _():
        hlast[...] = jnp.zeros(hlast.shape, F32)

    h = _norm_mod(x_ref[...], nw_ref[...], mod_ref[1:2, :], mod_ref[0:1, :])
    row = lax.broadcasted_iota(jnp.int32, h.shape, 0)
    hprev = jnp.where(row == 0, hlast[7:8, :], pltpu.roll(h, 1, 0))
    hlast[...] = h[t - 8:t, :]
    dx = hprev - h
    mixed = lambda n: (h + dx * mix_ref[n:n + 1, :]).astype(BF16)
    w0, a0, k_k, k_a, r_k = (vec_ref[i:i + 1, :] for i in range(5))

    r = jnp.dot(mixed(0), wr_ref[...], preferred_element_type=F32)
    k = jnp.dot(mixed(1), wk_ref[...], preferred_element_type=F32)
    v = jnp.dot(mixed(2), wv_ref[...], preferred_element_type=F32)
    wl = _mm(jnp.tanh(jnp.dot(mixed(3), w1_ref[...], preferred_element_type=F32)), w2_ref[...])
    al = _mm(jnp.dot(mixed(4), a1_ref[...], preferred_element_type=F32), a2_ref[...])
    g = _mm(_sigmoid(jnp.dot(mixed(5), g1_ref[...], preferred_element_type=F32)), g2_ref[...])

    w_log = -_softplus(-(w0 + wl)) - 0.5
    a = _sigmoid(a0 + al)
    ind, indt = ind_ref[...], indt_ref[...]
    headsum = lambda q: _mm_sel_r(_mm_sel_r(q, ind), indt)
    kk = k * k_k
    kk = kk / jnp.maximum(jnp.sqrt(headsum(kk * kk)), KK_EPS)
    k2 = k * (1.0 + (a - 1.0) * k_a)

    r_ref[...] = r
    lw_ref[...] = -jnp.exp(w_log)
    k_ref[...] = k2
    v_ref[...] = v
    kk_ref[...] = kk
    bb_ref[...] = kk * a
    bonus_ref[...] = headsum(r * k2 * r_k) * v
    g_ref[...] = g


def _r_wkv_kernel(r_ref, lw_ref, k_ref, v_ref, kk_ref, bb_ref, bonus_ref, g_ref, x_ref, mod_ref,
                  lnw_ref, lnb_ref, ind_ref, indt_ref, ow_ref, o_ref, state, ybuf, *, hd):
    t = x_ref.shape[0]
    d = x_ref.shape[1]
    ngr, gw, _ = state.shape
    hpg = gw // hd
    n = hpg * t

    @pl.when(pl.program_id(1) == 0)
    def _():
        state[...] = jnp.zeros(state.shape, F32)

    lw = lw_ref[...]
    cl = _mm_sel_l(jnp.where(_tril(t), 1.0, 0.0).astype(BF16), lw)
    ecl = jnp.exp(cl)
    encl = jnp.exp(-cl)
    rt = r_ref[...] * ecl
    at = -kk_ref[...] * jnp.exp(cl - lw)
    kt = k_ref[...] * encl
    bt = bb_ref[...] * encl
    v = v_ref[...]
    ptot = ecl[t - 1:t, :]

    ri = lax.broadcasted_iota(jnp.int32, (n, gw), 0)
    ci = lax.broadcasted_iota(jnp.int32, (n, gw), 1)
    headm = (ri // t) == (ci // hd)
    ri2 = lax.broadcasted_iota(jnp.int32, (n, n), 0)
    ci2 = lax.broadcasted_iota(jnp.int32, (n, n), 1)
    same = (ri2 // t) == (ci2 // t)
    tt = lax.broadcasted_iota(jnp.int32, (t, n), 0)
    ss = lax.broadcasted_iota(jnp.int32, (t, n), 1) % t
    strict = tt > ss
    incl = tt >= ss
    eye = jnp.where(tt == ss, 1.0, 0.0)
    vals_same = (lax.broadcasted_iota(jnp.int32, (gw, gw), 0) // hd) == (
        lax.broadcasted_iota(jnp.int32, (gw, gw), 1) // hd)

    def blockdiag(m, mask):
        return jnp.where(mask, jnp.concatenate([m] * hpg, axis=0), 0.0).astype(BF16)

    groups = range(ngr)
    sls = [slice(g * gw, (g + 1) * gw) for g in groups]
    dot = lambda a, b: jnp.dot(a.astype(BF16), b, preferred_element_type=F32)
    ar = [jnp.concatenate([at[:, sl], rt[:, sl]], axis=0) for sl in sls]
    vst = [blockdiag(v[:, sl], headm) for sl in sls]
    xbk = [jnp.concatenate([blockdiag(bt[:, sl], headm), blockdiag(kt[:, sl], headm)], axis=0) for sl in sls]
    gram = [_mm_nt(ar[g], xbk[g]) for g in groups]
    hs = [state[g] for g in groups]
    hproj = [_mm_nt(ar[g], hs[g]) for g in groups]
    a_ab = [jnp.where(strict, gram[g][0:t, 0:n], 0.0) for g in groups]
    rhs = [hproj[g][0:t] + dot(jnp.where(strict, gram[g][0:t, n:2 * n], 0.0), vst[g]) for g in groups]
    tinv = [eye + a_ab[g] for g in groups]
    p = [dot(a_ab[g], blockdiag(a_ab[g], same)) for g in groups]
    steps = max(t - 1, 1).bit_length() - 1
    for i in range(steps):
        last = i + 1 == steps
        for g in groups:
            tb = blockdiag(tinv[g], same)
            prod = dot(p[g], tb if last else jnp.concatenate([tb, blockdiag(p[g], same)], axis=1))
            tinv[g] = tinv[g] + prod[:, 0:n]
            if not last:
                p[g] = prod[:, n:2 * n]
    u = [dot(tinv[g], blockdiag(rhs[g], headm)) for g in groups]
    for g in groups:
        r_bk = jnp.where(jnp.concatenate([incl, incl], axis=1), gram[g][t:2 * t, :], 0.0)
        ybuf[:, sls[g]] = hproj[g][t:2 * t] + dot(r_bk, jnp.concatenate([blockdiag(u[g], headm), vst[g]], axis=0))
    for g in groups:
        sl = sls[g]
        upd = _mm_tn(jnp.concatenate([u[g], v[:, sl]], axis=0), jnp.concatenate([bt[:, sl], kt[:, sl]], axis=0))
        state[g] = (hs[g] + jnp.where(vals_same, upd, 0.0)) * ptot[:, sl]

    y = ybuf[...]
    ind, indt = ind_ref[...], indt_ref[...]
    headmean = lambda q: _mm_sel_r(_mm_sel_r(q, ind), indt) * (1.0 / hd)
    mu = headmean(y)
    yc = y - mu
    var = headmean(yc * yc)
    yn = yc * lax.rsqrt(var + GN_EPS) * lnw_ref[...] + lnb_ref[...]
    yo = (yn + bonus_ref[...]) * g_ref[...]
    out = _mm(yo, ow_ref[...])
    o_ref[...] = x_ref[...] + (1.0 + mod_ref[2:3, :]) * out


def _rwkv(x, mod, nw, mix, w_rkv, w0, w1, w2, a0, a1, a2, g1, g2, k_k, k_a, r_k, lnx_w, lnx_b, out_w):
    b, s, d = x.shape
    hd = HEAD_DIM
    assert d % WKV_GROUP == 0 and d // hd <= LANE

    def pad_pair(p1, p2):
        r = p1.shape[1]
        rp = -(-r // LANE) * LANE
        return (jnp.pad(p1, ((0, 0), (0, rp - r))).astype(BF16), jnp.pad(p2, ((0, rp - r), (0, 0))).astype(BF16))

    w1p, w2p = pad_pair(w1, w2)
    a1p, a2p = pad_pair(a1, a2)
    g1p, g2p = pad_pair(g1, g2)
    vecs = jnp.stack([w0, a0, k_k, k_a, r_k.reshape(d), jnp.zeros_like(w0), jnp.zeros_like(w0), jnp.zeros_like(w0)])
    ind, indt = _head_indicator(d, hd)
    wb = w_rkv.astype(BF16)
    t = ROW_TILE
    res = lambda arr: _resident(arr.shape)
    outs = pl.pallas_call(
        _r_in_kernel,
        grid=(b, s // t),
        in_specs=[_rows(t, d), _mod_spec(d), _resident((1, d)), _resident((6, d)),
                  _resident((d, d)), _resident((d, d)), _resident((d, d)),
                  res(w1p), res(w2p), res(a1p), res(a2p), res(g1p), res(g2p),
                  _resident((8, d)), res(ind), res(indt)],
        out_specs=[_rows(t, d)] * 8,
        out_shape=[jax.ShapeDtypeStruct((b, s, d), F32)] * 8,
        scratch_shapes=[pltpu.VMEM((8, d), F32)],
        compiler_params=_params(True),
        name="r_in",
    )(x, mod, nw.reshape(1, d), mix, wb[0], wb[1], wb[2], w1p, w2p, a1p, a2p, g1p, g2p, vecs, ind, indt)

    tc = WKV_CHUNK
    return pl.pallas_call(
        functools.partial(_r_wkv_kernel, hd=hd),
        grid=(b, s // tc),
        in_specs=[_rows(tc, d)] * 9 + [_mod_spec(d), _resident((1, d)), _resident((1, d)),
                                       res(ind), res(indt), _resident((d, d))],
        out_specs=_rows(tc, d),
        out_shape=jax.ShapeDtypeStruct(x.shape, F32),
        scratch_shapes=[pltpu.VMEM((d // WKV_GROUP, WKV_GROUP, WKV_GROUP), F32), pltpu.VMEM((tc, d), F32)],
        compiler_params=_params(True),
        name="r_wkv",
    )(*outs, x, mod, lnx_w.reshape(1, d), lnx_b.reshape(1, d), ind, indt, out_w.astype(BF16))


def _f_in_kernel(x_ref, mod_ref, nw_ref, wq_ref, wk_ref, wvt_ref, wf_ref, vec_ref, fb_ref, ind_ref, indt_ref,
                 q_ref, k_ref, vt_ref, cum_ref, cumt_ref, carry, *, hd):
    t = x_ref.shape[0]

    @pl.when(pl.program_id(1) == 0)
    def _():
        carry[...] = jnp.zeros(carry.shape, F32)

    h = _norm_mod(x_ref[...], nw_ref[...], mod_ref[1:2, :], mod_ref[0:1, :]).astype(BF16)
    ind, indt = ind_ref[...], indt_ref[...]

    def head_rms(q, w):
        ms = _mm_sel_r(_mm_sel_r(q * q, ind), indt) * (1.0 / hd)
        return q * lax.rsqrt(ms + RMS_EPS) * w

    q = head_rms(jnp.dot(h, wq_ref[...], preferred_element_type=F32), vec_ref[0:1, :])
    k = head_rms(jnp.dot(h, wk_ref[...], preferred_element_type=F32), vec_ref[1:2, :])
    q_ref[...] = (q * (hd ** -0.5)).astype(BF16)
    k_ref[...] = k.astype(BF16)
    vt_ref[...] = _mm_nt(wvt_ref[...], h).astype(BF16)
    f = jnp.dot(h, wf_ref[...], preferred_element_type=F32) + fb_ref[...]
    logf = -_softplus(-f)
    cum = _mm_sel_l(jnp.where(_tril(t), 1.0, 0.0).astype(BF16), logf) + carry[0:1, :]
    carry[...] = jnp.broadcast_to(cum[t - 1:t, :], carry.shape)
    cum_ref[...] = cum
    cumt_ref[...] = cum.T


def _f_att_kernel(q_ref, k_ref, vt_ref, cum_ref, cumt_ref, x_ref, mod_ref, ow_ref, o_ref,
                  obuf, qm_scr, m_scr, acc_scr, *, hd, pairs_per_loop):
    tq = x_ref.shape[0]
    d = x_ref.shape[1]
    tk = vt_ref.shape[2]
    qi = pl.program_id(1)
    first_l = lax.broadcasted_iota(jnp.int32, (1, LANE), 1) < hd
    first_r = lax.broadcasted_iota(jnp.int32, (LANE, 1), 0) < hd
    on_or_below = (lax.broadcasted_iota(jnp.int32, (tk, tq), 0)
                   <= lax.broadcasted_iota(jnp.int32, (tk, tq), 1))
    zero_b = jnp.zeros((), BF16)
    one_b = jnp.ones((), BF16)

    lanes = lambda p: slice(p * LANE, (p + 1) * LANE)
    for p0 in range(0, d // LANE, pairs_per_loop):
        pairs = range(p0, p0 + pairs_per_loop)
        heads = [2 * p + i for p in pairs for i in range(2)]
        for p in pairs:
            qp = q_ref[:, lanes(p)]
            qm_scr[2 * p] = jnp.where(first_l, qp, zero_b)
            qm_scr[2 * p + 1] = jnp.where(first_l, zero_b, qp)
        for h in heads:
            m_scr[h] = jnp.full((1, tq), NEG, F32)
            acc_scr[h] = jnp.zeros((LANE, tq), F32)

        def step(j, diagonal, pairs=pairs, heads=heads):
            start = pl.multiple_of(j * tk, tk)
            sc = {}
            for p in pairs:
                kj = k_ref[pl.ds(start, tk), lanes(p)]
                for h in (2 * p, 2 * p + 1):
                    sc[h] = lax.dot_general(kj, qm_scr[h], (((1,), (1,)), ((), ())), preferred_element_type=F32)
            for h in heads:
                sc[h] = sc[h] + cumt_ref[qi, h:h + 1, :] - cum_ref[pl.ds(start, tk), h:h + 1]
                if diagonal:
                    sc[h] = jnp.where(on_or_below, sc[h], NEG)
            m_old = {h: m_scr[h] for h in heads}
            m_new = {h: jnp.maximum(m_old[h], jnp.max(sc[h], axis=0, keepdims=True)) for h in heads}
            pr = {h: jnp.exp(sc[h] - m_new[h]).astype(BF16) for h in heads}
            for p in pairs:
                vtj = vt_ref[j, lanes(p), :]
                vt2 = (jnp.where(first_r, vtj, one_b), jnp.where(first_r, one_b, vtj))
                for i in range(2):
                    h = 2 * p + i
                    acc_scr[h] = (jnp.exp(m_old[h] - m_new[h]) * acc_scr[h]
                                  + jnp.dot(vt2[i], pr[h], preferred_element_type=F32))
                    m_scr[h] = m_new[h]

        def body(j, c):
            step(j, False)
            return c

        lax.fori_loop(0, qi, body, 0)
        step(qi, True)
        for p in pairs:
            acc0, acc1 = acc_scr[2 * p], acc_scr[2 * p + 1]
            o_t = jnp.where(first_r, acc0 / acc0[hd:hd + 1, :], acc1 / acc1[0:1, :])
            obuf[:, lanes(p)] = o_t.T.astype(BF16)

    out = jnp.dot(obuf[...], ow_ref[...], preferred_element_type=F32)
    o_ref[...] = x_ref[...] + (1.0 + mod_ref[2:3, :]) * out


def _fox(x, mod, nw, qkvf_w, fgate_b, q_norm_w, k_norm_w, out_w):
    b, s, d = x.shape
    hd = HEAD_DIM
    nh = d // hd
    assert qkvf_w.shape[1] == 3 * d + nh and nh <= LANE and 2 * hd == LANE
    wq, wk, wv = (qkvf_w[:, i * d:(i + 1) * d].astype(BF16) for i in range(3))
    wf = jnp.pad(qkvf_w[:, 3 * d:], ((0, 0), (0, LANE - nh))).astype(BF16)
    fb = jnp.pad(fgate_b, (0, LANE - nh)).reshape(1, LANE)
    z = jnp.zeros((d,), F32)
    vecs = jnp.stack([jnp.tile(q_norm_w, nh), jnp.tile(k_norm_w, nh), z, z, z, z, z, z])
    ind, indt = _head_indicator(d, hd)
    t = ROW_TILE
    res = lambda arr: _resident(arr.shape)
    nt = s // t
    per_tile = lambda rows: pl.BlockSpec((None, None, rows, t), lambda bi, si: (bi, si, 0, 0))
    q, k, vt, cum, cumt = pl.pallas_call(
        functools.partial(_f_in_kernel, hd=hd),
        grid=(b, nt),
        in_specs=[_rows(t, d), _mod_spec(d), _resident((1, d)), _resident((d, d)), _resident((d, d)),
                  _resident((d, d)), _resident((d, LANE)), _resident((8, d)), _resident((1, LANE)),
                  res(ind), res(indt)],
        out_specs=[_rows(t, d), _rows(t, d), per_tile(d), _rows(t, LANE), per_tile(LANE)],
        out_shape=[jax.ShapeDtypeStruct((b, s, d), BF16), jax.ShapeDtypeStruct((b, s, d), BF16),
                   jax.ShapeDtypeStruct((b, nt, d, t), BF16), jax.ShapeDtypeStruct((b, s, LANE), F32),
                   jax.ShapeDtypeStruct((b, nt, LANE, t), F32)],
        scratch_shapes=[pltpu.VMEM((8, LANE), F32)],
        compiler_params=_params(True),
        name="f_in",
    )(x, mod, nw.reshape(1, d), wq, wk, wv.T, wf, vecs, fb, ind, indt)

    whole = lambda *shape: pl.BlockSpec((None,) + shape, lambda bi, si: (bi,) + (0,) * len(shape))
    return pl.pallas_call(
        functools.partial(_f_att_kernel, hd=hd, pairs_per_loop=ATT_PAIRS_PER_LOOP),
        grid=(b, nt),
        in_specs=[_rows(t, d), whole(s, d), whole(nt, d, t), whole(s, LANE), whole(nt, LANE, t),
                  _rows(t, d), _mod_spec(d), _resident((d, d))],
        out_specs=_rows(t, d),
        out_shape=jax.ShapeDtypeStruct(x.shape, F32),
        scratch_shapes=[pltpu.VMEM((t, d), BF16), pltpu.VMEM((nh, t, LANE), BF16),
                        pltpu.VMEM((nh, 1, t), F32), pltpu.VMEM((nh, LANE, t), F32)],
        compiler_params=_params(False),
        name="f_att",
    )(q, k, vt, cum, cumt, x, mod, out_w.astype(BF16))


def kernel(x, c, ada_w, ada_b, norm1_w, norm2_w, ffn_w1, ffn_w3, ffn_w2, m_in_w, m_conv_w, m_conv_b, m_dt_bias, m_A_log, m_D, m_norm_w, m_out_w, r_mix, r_w_rkv, r_w0, r_w1, r_w2, r_a0, r_a1, r_a2, r_g1, r_g2, r_k_k, r_k_a, r_r_k, r_lnx_w, r_lnx_b, r_out_w, f_qkvf_w, f_fgate_b, f_q_norm_w, f_k_norm_w, f_out_w):
    depth = ada_w.shape[0]
    mod = _ada(c, ada_w, ada_b)
    ia = ib = ic = 0
    for i in range(depth):
        kind = i % 3
        if kind == 0:
            x = _mamba(x, mod[i], norm1_w[i], m_in_w[ia], m_conv_w[ia], m_conv_b[ia], m_dt_bias[ia],
                       m_A_log[ia], m_D[ia], m_norm_w[ia], m_out_w[ia])
            ia += 1
        elif kind == 1:
            x = _rwkv(x, mod[i], norm1_w[i], r_mix[ib], r_w_rkv[ib], r_w0[ib], r_w1[ib], r_w2[ib], r_a0[ib],
                      r_a1[ib], r_a2[ib], r_g1[ib], r_g2[ib], r_k_k[ib], r_k_a[ib], r_r_k[ib],
                      r_lnx_w[ib], r_lnx_b[ib], r_out_w[ib])
            ib += 1
        else:
            x = _fox(x, mod[i], norm1_w[i], f_qkvf_w[ic], f_fgate_b[ic], f_q_norm_w[ic], f_k_norm_w[ic],
                     f_out_w[ic])
            ic += 1
        x = _ffn(x, mod[i], norm2_w[i], ffn_w1[i], ffn_w3[i], ffn_w2[i])
    return x
```

```python
import functools

import jax
import jax.numpy as jnp
from jax import lax
from jax.experimental import pallas as pl
from jax.experimental.pallas import tpu as pltpu

F32 = jnp.float32
BF16 = jnp.bfloat16

RMS_EPS = 1e-6
GN_EPS = 64e-5
KK_EPS = 1e-12
NEG = -1e30
LOG2E = 1.4426950408889634

V7X_VMEM_LIMIT = 56 * 1024 * 1024

HEAD_DIM = 64
SSD_GROUPS = 8
SSD_STATE = 128
SSD_CHUNK = 128
WKV_CHUNK = 64
WKV_GROUP = 256
WKV_BATCH_ROWS = 4
ROW_TILE = 256
ATT_PAIRS_PER_LOOP = 4
BIAS_COLS = 6
LANE = 128


def _mm(a, b):
    return jnp.dot(a.astype(BF16), b.astype(BF16), preferred_element_type=F32)


def _mm_nt(a, b):
    return lax.dot_general(a.astype(BF16), b.astype(BF16), (((1,), (1,)), ((), ())),
                           preferred_element_type=F32)


def _mm_tn(a, b):
    return jnp.dot(a.T.astype(BF16), b.astype(BF16), preferred_element_type=F32)


def _split(x, terms):
    parts = []
    for i in range(terms):
        p = x.astype(BF16)
        parts.append(p)
        if i + 1 < terms:
            x = x - p.astype(F32)
    return parts


def _mm_sel_l(sel, x, terms=3):
    return sum(jnp.dot(sel, p, preferred_element_type=F32) for p in _split(x, terms))


def _mm_sel_r(x, sel, terms=3):
    return sum(jnp.dot(p, sel, preferred_element_type=F32) for p in _split(x, terms))


def _mm_hi(a, b):
    ah = a.astype(BF16)
    al = (a - ah.astype(F32)).astype(BF16)
    bh = b.astype(BF16)
    bl = (b - bh.astype(F32)).astype(BF16)
    d = lambda p, q: jnp.dot(p, q, preferred_element_type=F32)
    return d(ah, bh) + d(ah, bl) + d(al, bh)


def _sigmoid(x):
    return jax.nn.sigmoid(x)


def _silu(x):
    return x * _sigmoid(x)


def _softplus(x):
    return jnp.maximum(x, 0.0) + jnp.log1p(jnp.exp(-jnp.abs(x)))


def _norm_mod(x, nw, scale, shift):
    y = x * lax.rsqrt(jnp.mean(x * x, axis=-1, keepdims=True) + RMS_EPS)
    return (y * nw) * (1.0 + scale) + shift


def _tril(n, strict=False):
    r = lax.broadcasted_iota(jnp.int32, (n, n), 0)
    c = lax.broadcasted_iota(jnp.int32, (n, n), 1)
    return (r > c) if strict else (r >= c)


def _resident(shape):
    nd = len(shape)
    return pl.BlockSpec(shape, lambda *_: (0,) * nd, pipeline_mode=pl.Buffered(1))


def _rows(t, width):
    return pl.BlockSpec((None, t, width), lambda b, s: (b, s, 0))


def _mod_spec(d):
    return pl.BlockSpec((None, 6, d), lambda b, s: (b, 0, 0))


def _params(seq_axis_carries):
    sem = ("parallel", "arbitrary") if seq_axis_carries else ("parallel", "parallel")
    return pltpu.CompilerParams(dimension_semantics=sem, vmem_limit_bytes=V7X_VMEM_LIMIT)


def _head_indicator(d, hd):
    ind = (jnp.arange(d)[:, None] // hd == jnp.arange(LANE)[None, :]).astype(BF16)
    return ind, ind.T


def _ada_kernel(c_ref, w_ref, b_ref, o_ref):
    c = c_ref[...]
    o_ref[...] = _mm_hi(_silu(c), w_ref[...]) + b_ref[...]


def _ada(c, ada_w, ada_b):
    depth, d, n6 = ada_w.shape
    b = c.shape[0]
    tn = 768
    out = pl.pallas_call(
        _ada_kernel,
        grid=(depth, n6 // tn),
        in_specs=[pl.BlockSpec((b, d), lambda l, j: (0, 0)),
                  pl.BlockSpec((None, d, tn), lambda l, j: (l, 0, j)),
                  pl.BlockSpec((None, 1, tn), lambda l, j: (l, 0, j))],
        out_specs=pl.BlockSpec((None, b, tn), lambda l, j: (l, 0, j)),
        out_shape=jax.ShapeDtypeStruct((depth, b, n6), F32),
        compiler_params=_params(False),
        name="ada",
    )(c, ada_w, ada_b.reshape(depth, 1, n6))
    return out.reshape(depth, b, 6, d)


def _ffn_kernel(x_ref, mod_ref, nw_ref, w1_ref, w3_ref, w2_ref, o_ref):
    x = x_ref[...]
    h = _norm_mod(x, nw_ref[...], mod_ref[4:5, :], mod_ref[3:4, :]).astype(BF16)
    a = jnp.dot(h, w1_ref[...], preferred_element_type=F32)
    b = jnp.dot(h, w3_ref[...], preferred_element_type=F32)
    g = (_silu(a) * b).astype(BF16)
    y = jnp.dot(g, w2_ref[...], preferred_element_type=F32)
    o_ref[...] = x + (1.0 + mod_ref[5:6, :]) * y


def _ffn(x, mod, nw, w1, w3, w2):
    b, s, d = x.shape
    dff = w1.shape[1]
    t = ROW_TILE
    return pl.pallas_call(
        _ffn_kernel,
        grid=(b, s // t),
        in_specs=[_rows(t, d), _mod_spec(d), _resident((1, d)),
                  _resident((d, dff)), _resident((d, dff)), _resident((dff, d))],
        out_specs=_rows(t, d),
        out_shape=jax.ShapeDtypeStruct(x.shape, F32),
        compiler_params=_params(False),
        name="ffn",
    )(x, mod, nw.reshape(1, d), w1.astype(BF16), w3.astype(BF16), w2.astype(BF16))


def _m_in_kernel(x_ref, mod_ref, nw_ref, wz_ref, wx_ref, wd_ref, cw_ref, cb_ref, gate_ref, xc_ref, dt_ref, ubuf):
    t = x_ref.shape[0]

    @pl.when(pl.program_id(1) == 0)
    def _():
        ubuf[0:8, :] = jnp.zeros((8, ubuf.shape[1]), F32)

    h = _norm_mod(x_ref[...], nw_ref[...], mod_ref[1:2, :], mod_ref[0:1, :]).astype(BF16)
    dt_ref[...] = jnp.dot(h, wd_ref[...], preferred_element_type=F32)
    gate_ref[...] = _silu(jnp.dot(h, wz_ref[...], preferred_element_type=F32))
    u = jnp.dot(h, wx_ref[...], preferred_element_type=F32)
    ubuf[8:8 + t, :] = u
    acc = cb_ref[...] + cw_ref[3:4, :] * u
    for k in range(3):
        acc = acc + cw_ref[k:k + 1, :] * ubuf[5 + k:5 + k + t, :]
    ubuf[0:8, :] = u[t - 8:t, :]
    xc_ref[...] = _silu(acc)


def _m_ssd_kernel(xc_ref, gate_ref, dt_ref, x_ref, mod_ref, dtb_ref, alog_ref, e_ref,
                  dskip_ref, gnw_ref, ow_ref, o_ref, st, ybuf, *, nh):
    t = x_ref.shape[0]
    di = gate_ref.shape[1]
    ng, ns, gw = st.shape
    hpg = nh // ng
    hd = gw // hpg

    @pl.when(pl.program_id(1) == 0)
    def _():
        st[...] = jnp.zeros(st.shape, F32)

    xc = xc_ref[...]
    xs = xc[:, :di]

    lane = lax.broadcasted_iota(jnp.int32, (1, LANE), 1)
    dt = _softplus(dt_ref[...] + dtb_ref[...])
    a_neg = jnp.where(lane < nh, -jnp.exp(alog_ref[...]), 0.0)
    a = dt * a_neg
    tril = _tril(t)
    cs = _mm_sel_l(jnp.where(tril, 1.0, 0.0).astype(BF16), a)
    cs_t = cs.T
    e = e_ref[...]
    dt_e = _mm_sel_r(dt, e, 2)
    cs_e = _mm_sel_r(cs, e, 2)
    cs_last = cs_e[t - 1:t, :]
    xd = xs * dt_e
    ecs = jnp.exp(cs_e)
    xdd = xd * jnp.exp(cs_last - cs_e)
    cdec = jnp.exp(cs_last)
    yskip = xs * dskip_ref[...]
    gate = gate_ref[...]

    rr = lax.broadcasted_iota(jnp.int32, (hpg * t, gw), 0) // t
    ll = lax.broadcasted_iota(jnp.int32, (hpg * t, gw), 1) // hd
    bd = rr == ll
    groups = range(ng)
    sls = [slice(g * gw, (g + 1) * gw) for g in groups]
    bgs = [xc[:, di + g * ns:di + (g + 1) * ns].astype(BF16) for g in groups]
    cgs = [xc[:, di + (ng + g) * ns:di + (ng + g + 1) * ns].astype(BF16) for g in groups]
    scores = [_mm_nt(cgs[g], bgs[g]) for g in groups]
    sts = [st[g] for g in groups]
    yoff = [_mm(cgs[g], sts[g]) for g in groups]
    supd = [_mm_tn(bgs[g], xdd[:, sls[g]]) for g in groups]
    ydiag = []
    for g in groups:
        parts = []
        for j in range(hpg):
            h = g * hpg + j
            diff = cs[:, h:h + 1] - cs_t[h:h + 1, :]
            parts.append((scores[g] * jnp.exp(jnp.where(tril, diff, NEG))).astype(BF16))
        lhs = jnp.concatenate(parts, axis=1)
        rhs = jnp.where(bd, jnp.concatenate([xd[:, sls[g]]] * hpg, axis=0), 0.0).astype(BF16)
        ydiag.append(jnp.dot(lhs, rhs, preferred_element_type=F32))
    for g in groups:
        sl = sls[g]
        st[g] = sts[g] * cdec[:, sl] + supd[g]
        y = (ydiag[g] + yoff[g] * ecs[:, sl] + yskip[:, sl]) * gate[:, sl]
        y = y * lax.rsqrt(jnp.mean(y * y, axis=-1, keepdims=True) + RMS_EPS) * gnw_ref[:, sl]
        ybuf[:, sl] = y.astype(BF16)

    out = jnp.dot(ybuf[...], ow_ref[...], preferred_element_type=F32)
    o_ref[...] = x_ref[...] + (1.0 + mod_ref[2:3, :]) * out


def _mamba(x, mod, nw, in_w, conv_w, conv_b, dt_bias, a_log, d_skip, norm_w, out_w):
    b, s, d = x.shape
    nh = dt_bias.shape[0]
    di = nh * HEAD_DIM
    cdim = conv_w.shape[1]
    ng, ns = SSD_GROUPS, SSD_STATE
    gw = di // ng
    assert cdim == di + 2 * ng * ns and in_w.shape[1] == di + cdim + nh and nh <= LANE
    wz = in_w[:, :di].astype(BF16)
    wx = in_w[:, di:di + cdim].astype(BF16)
    wd = jnp.pad(in_w[:, di + cdim:], ((0, 0), (0, LANE - nh))).astype(BF16)
    t = ROW_TILE
    gate, xc, dtr = pl.pallas_call(
        _m_in_kernel,
        grid=(b, s // t),
        in_specs=[_rows(t, d), _mod_spec(d), _resident((1, d)),
                  _resident((d, di)), _resident((d, cdim)), _resident((d, LANE)),
                  _resident((4, cdim)), _resident((1, cdim))],
        out_specs=[_rows(t, di), _rows(t, cdim), _rows(t, LANE)],
        out_shape=[jax.ShapeDtypeStruct((b, s, di), F32), jax.ShapeDtypeStruct((b, s, cdim), F32),
                   jax.ShapeDtypeStruct((b, s, LANE), F32)],
        scratch_shapes=[pltpu.VMEM((t + 8, cdim), F32)],
        compiler_params=_params(True),
        name="m_in",
    )(x, mod, nw.reshape(1, d), wz, wx, wd, conv_w, conv_b.reshape(1, cdim))

    pad1 = lambda v: jnp.pad(v, (0, LANE - nh)).reshape(1, LANE)
    expand = (jnp.arange(LANE)[:, None] == jnp.arange(di)[None, :] // HEAD_DIM).astype(BF16)
    tc = SSD_CHUNK
    return pl.pallas_call(
        functools.partial(_m_ssd_kernel, nh=nh),
        grid=(b, s // tc),
        in_specs=[_rows(tc, cdim), _rows(tc, di), _rows(tc, LANE), _rows(tc, d), _mod_spec(d),
                  _resident((1, LANE)), _resident((1, LANE)),
                  _resident((LANE, di)), _resident((1, di)), _resident((1, di)), _resident((di, d))],
        out_specs=_rows(tc, d),
        out_shape=jax.ShapeDtypeStruct(x.shape, F32),
        scratch_shapes=[pltpu.VMEM((ng, ns, gw), F32), pltpu.VMEM((tc, di), BF16)],
        compiler_params=_params(True),
        name="m_ssd",
    )(xc, gate, dtr, x, mod, pad1(dt_bias), pad1(a_log), expand,
      jnp.repeat(d_skip, HEAD_DIM).reshape(1, di), norm_w.reshape(1, di), out_w.astype(BF16))


def _r_in_kernel(x_ref, mod_ref, nw_ref, mix_ref, wr_ref, wk_ref, wv_ref, w1_ref, w2_ref, a1_ref, a2_ref,
                 g1_ref, g2_ref, vec_ref, ind_ref, indt_ref,
                 r_ref, lw_ref, k_ref, v_ref, kk_ref, bb_ref, bonus_ref, g_ref, hlast):
    t = x_ref.shape[0]

    @pl.when(pl.program_id(1) == 0)
    def _():
        hlast[...] = jnp.zeros(hlast.shape, F32)

    h = _norm_mod(x_ref[...], nw_ref[...], mod_ref[1:2, :], mod_ref[0:1, :])
    row = lax.broadcasted_iota(jnp.int32, h.shape, 0)
    hprev = jnp.where(row == 0, hlast[7:8, :], pltpu.roll(h, 1, 0))
    hlast[...] = h[t - 8:t, :]
    dx = hprev - h
    w0, a0, k_k, k_a, r_k = (vec_ref[i:i + 1, :] for i in range(5))
    ind, indt = ind_ref[...], indt_ref[...]
    headsum = lambda q: _mm_sel_r(_mm_sel_r(q, ind, 2), indt, 2)
    dotf = lambda a, b_ref: jnp.dot(a, b_ref[...], preferred_element_type=F32)
    mixed = lambda n: (h + dx * mix_ref[n:n + 1, :]).astype(BF16)

    r = dotf(mixed(0), wr_ref)
    k = dotf(mixed(1), wk_ref)
    v = dotf(mixed(2), wv_ref)
    wl = dotf(jnp.tanh(dotf(mixed(3), w1_ref)).astype(BF16), w2_ref)
    al = dotf(dotf(mixed(4), a1_ref).astype(BF16), a2_ref)
    r_ref[...] = r
    v_ref[...] = v
    g_ref[...] = dotf(_sigmoid(dotf(mixed(5), g1_ref)).astype(BF16), g2_ref)
    lw_ref[...] = -jnp.exp(-_softplus(-(w0 + wl)) - 0.5)
    a = _sigmoid(a0 + al)
    kk = k * k_k
    kk = kk / jnp.maximum(jnp.sqrt(headsum(kk * kk)), KK_EPS)
    k2 = k * (1.0 + (a - 1.0) * k_a)
    k_ref[...] = k2
    kk_ref[...] = kk
    bb_ref[...] = kk * a
    bonus_ref[...] = headsum(r * k2 * r_k) * v


def _r_wkv_kernel(r_ref, lw_ref, k_ref, v_ref, kk_ref, bb_ref, bonus_ref, g_ref, x_ref, mod_ref,
                  lnw_ref, lnb_ref, ind_ref, indt_ref, ow_ref, o_ref, state, ybuf, *, hd):
    nb, t, d = x_ref.shape
    gw = state.shape[1]
    ngr = d // gw
    hpg = gw // hd
    n = hpg * t

    @pl.when(pl.program_id(1) == 0)
    def _():
        state[...] = jnp.zeros(state.shape, F32)

    tri = jnp.where(_tril(t), 1.0, 0.0).astype(BF16)
    at, rt, kt, bt, v, ptot = [], [], [], [], [], []
    for bi in range(nb):
        lw = lw_ref[bi]
        cl = _mm_sel_l(tri, lw)
        ecl = jnp.exp(cl)
        encl = jnp.exp(-cl)
        rt.append(r_ref[bi] * ecl)
        at.append(-kk_ref[bi] * jnp.exp(cl - lw))
        kt.append(k_ref[bi] * encl)
        bt.append(bb_ref[bi] * encl)
        v.append(v_ref[bi])
        ptot.append(ecl[t - 1:t, :])

    ri = lax.broadcasted_iota(jnp.int32, (n, gw), 0)
    ci = lax.broadcasted_iota(jnp.int32, (n, gw), 1)
    headm = (ri // t) == (ci // hd)
    ri2 = lax.broadcasted_iota(jnp.int32, (n, n), 0)
    ci2 = lax.broadcasted_iota(jnp.int32, (n, n), 1)
    same = (ri2 // t) == (ci2 // t)
    tt = lax.broadcasted_iota(jnp.int32, (t, n), 0)
    ss = lax.broadcasted_iota(jnp.int32, (t, n), 1) % t
    strict = tt > ss
    incl = tt >= ss
    eye = jnp.where(tt == ss, 1.0, 0.0)
    vals_same = (lax.broadcasted_iota(jnp.int32, (gw, gw), 0) // hd) == (
        lax.broadcasted_iota(jnp.int32, (gw, gw), 1) // hd)

    def blockdiag(m, mask):
        return jnp.where(mask, jnp.concatenate([m] * hpg, axis=0), 0.0).astype(BF16)

    groups = range(nb * ngr)
    bis = [g // ngr for g in groups]
    sls = [slice((g % ngr) * gw, (g % ngr + 1) * gw) for g in groups]
    dot = lambda a, b: jnp.dot(a.astype(BF16), b, preferred_element_type=F32)
    ar = [jnp.concatenate([at[bi][:, sl], rt[bi][:, sl]], axis=0) for bi, sl in zip(bis, sls)]
    vst = [blockdiag(v[bi][:, sl], headm) for bi, sl in zip(bis, sls)]
    xbk = [jnp.concatenate([blockdiag(bt[bi][:, sl], headm), blockdiag(kt[bi][:, sl], headm)], axis=0)
           for bi, sl in zip(bis, sls)]
    gram = [_mm_nt(ar[g], xbk[g]) for g in groups]
    hs = [state[g] for g in groups]
    hproj = [_mm_nt(ar[g], hs[g]) for g in groups]
    a_ab = [jnp.where(strict, gram[g][0:t, 0:n], 0.0) for g in groups]
    rhs = [hproj[g][0:t] + dot(jnp.where(strict, gram[g][0:t, n:2 * n], 0.0), vst[g]) for g in groups]
    tinv = [eye + a_ab[g] for g in groups]
    p = [dot(a_ab[g], blockdiag(a_ab[g], same)) for g in groups]
    steps = max(t - 1, 1).bit_length() - 1
    for i in range(steps):
        last = i + 1 == steps
        for g in groups:
            tb = blockdiag(tinv[g], same)
            prod = dot(p[g], tb if last else jnp.concatenate([tb, blockdiag(p[g], same)], axis=1))
            tinv[g] = tinv[g] + prod[:, 0:n]
            if not last:
                p[g] = prod[:, n:2 * n]
    u = [dot(tinv[g], blockdiag(rhs[g], headm)) for g in groups]
    for g in groups:
        r_bk = jnp.where(jnp.concatenate([incl, incl], axis=1), gram[g][t:2 * t, :], 0.0)
        ybuf[bis[g] * t:(bis[g] + 1) * t, sls[g]] = hproj[g][t:2 * t] + dot(
            r_bk, jnp.concatenate([blockdiag(u[g], headm), vst[g]], axis=0))
    for g in groups:
        bi, sl = bis[g], sls[g]
        upd = _mm_tn(jnp.concatenate([u[g], v[bi][:, sl]], axis=0),
                     jnp.concatenate([bt[bi][:, sl], kt[bi][:, sl]], axis=0))
        state[g] = (hs[g] + jnp.where(vals_same, upd, 0.0)) * ptot[bi][:, sl]

    y = ybuf[...]
    ind, indt = ind_ref[...], indt_ref[...]
    headmean = lambda q: _mm_sel_r(_mm_sel_r(q, ind, 2), indt, 2) * (1.0 / hd)
    mu = headmean(y)
    yc = y - mu
    var = headmean(yc * yc)
    yn = yc * lax.rsqrt(var + GN_EPS) * lnw_ref[...] + lnb_ref[...]
    yo = (yn + bonus_ref[...].reshape(nb * t, d)) * g_ref[...].reshape(nb * t, d)
    out = _mm(yo, ow_ref[...])
    for bi in range(nb):
        o_ref[bi] = x_ref[bi] + (1.0 + mod_ref[bi, 2:3, :]) * out[bi * t:(bi + 1) * t]


def _rwkv(x, mod, nw, mix, w_rkv, w0, w1, w2, a0, a1, a2, g1, g2, k_k, k_a, r_k, lnx_w, lnx_b, out_w):
    b, s, d = x.shape
    hd = HEAD_DIM
    assert d % WKV_GROUP == 0 and d // hd <= LANE

    def pad_pair(p1, p2):
        r = p1.shape[1]
        rp = -(-r // LANE) * LANE
        return (jnp.pad(p1, ((0, 0), (0, rp - r))).astype(BF16), jnp.pad(p2, ((0, rp - r), (0, 0))).astype(BF16))

    w1p, w2p = pad_pair(w1, w2)
    a1p, a2p = pad_pair(a1, a2)
    g1p, g2p = pad_pair(g1, g2)
    vecs = jnp.stack([w0, a0, k_k, k_a, r_k.reshape(d), jnp.zeros_like(w0), jnp.zeros_like(w0), jnp.zeros_like(w0)])
    ind, indt = _head_indicator(d, hd)
    wb = w_rkv.astype(BF16)
    t = ROW_TILE
    res = lambda arr: _resident(arr.shape)
    outs = pl.pallas_call(
        _r_in_kernel,
        grid=(b, s // t),
        in_specs=[_rows(t, d), _mod_spec(d), _resident((1, d)), _resident((6, d)),
                  _resident((d, d)), _resident((d, d)), _resident((d, d)),
                  res(w1p), res(w2p), res(a1p), res(a2p), res(g1p), res(g2p),
                  _resident((8, d)), res(ind), res(indt)],
        out_specs=[_rows(t, d)] * 8,
        out_shape=[jax.ShapeDtypeStruct((b, s, d), F32)] * 8,
        scratch_shapes=[pltpu.VMEM((8, d), F32)],
        compiler_params=_params(True),
        name="r_in",
    )(x, mod, nw.reshape(1, d), mix, wb[0], wb[1], wb[2], w1p, w2p, a1p, a2p, g1p, g2p, vecs, ind, indt)

    tc = WKV_CHUNK
    nb = WKV_BATCH_ROWS if b % WKV_BATCH_ROWS == 0 else 1
    rows = pl.BlockSpec((nb, tc, d), lambda bi, si: (bi, si, 0))
    return pl.pallas_call(
        functools.partial(_r_wkv_kernel, hd=hd),
        grid=(b // nb, s // tc),
        in_specs=[rows] * 9 + [pl.BlockSpec((nb, 6, d), lambda bi, si: (bi, 0, 0)), _resident((1, d)),
                               _resident((1, d)), res(ind), res(indt), _resident((d, d))],
        out_specs=rows,
        out_shape=jax.ShapeDtypeStruct(x.shape, F32),
        scratch_shapes=[pltpu.VMEM((nb * d // WKV_GROUP, WKV_GROUP, WKV_GROUP), F32),
                        pltpu.VMEM((nb * tc, d), F32)],
        compiler_params=_params(True),
        name="r_wkv",
    )(*outs, x, mod, lnx_w.reshape(1, d), lnx_b.reshape(1, d), ind, indt, out_w.astype(BF16))


def _f_in_kernel(x_ref, mod_ref, nw_ref, wq_ref, wk_ref, wvt_ref, wf_ref, vec_ref, fb_ref, ind_ref, indt_ref,
                 place_ref, ones_ref, q_ref, k_ref, vt_ref, qa_ref, ka_ref, carry, *, hd):
    t = x_ref.shape[0]

    @pl.when(pl.program_id(1) == 0)
    def _():
        carry[...] = jnp.zeros(carry.shape, F32)

    h = _norm_mod(x_ref[...], nw_ref[...], mod_ref[1:2, :], mod_ref[0:1, :]).astype(BF16)
    ind, indt = ind_ref[...], indt_ref[...]

    def head_rms(q, w):
        ms = _mm_sel_r(_mm_sel_r(q * q, ind, 2), indt, 2) * (1.0 / hd)
        return q * lax.rsqrt(ms + RMS_EPS) * w

    q = head_rms(jnp.dot(h, wq_ref[...], preferred_element_type=F32), vec_ref[0:1, :])
    k = head_rms(jnp.dot(h, wk_ref[...], preferred_element_type=F32), vec_ref[1:2, :])
    q_ref[...] = (q * (hd ** -0.5 * LOG2E)).astype(BF16)
    k_ref[...] = k.astype(BF16)
    vt_ref[...] = _mm_nt(wvt_ref[...], h).astype(BF16)
    f = jnp.dot(h, wf_ref[...], preferred_element_type=F32) + fb_ref[...]
    logf = -_softplus(-f)
    cum = _mm_sel_l(jnp.where(_tril(t), 1.0, 0.0).astype(BF16), logf) + carry[0:1, :]
    carry[...] = jnp.broadcast_to(cum[t - 1:t, :], carry.shape)
    pieces = _split(cum * LOG2E, 3)
    place = lambda first: sum(jnp.dot(p, place_ref[first + i], preferred_element_type=F32)
                              for i, p in enumerate(pieces))
    qa_ref[...] = (place(0) + ones_ref[0:1, :]).astype(BF16)
    ka_ref[...] = (ones_ref[1:2, :] - place(3)).astype(BF16)


def _f_att_kernel(q_ref, k_ref, vt_ref, qa_ref, ka_ref, x_ref, mod_ref, ow_ref, o_ref,
                  obuf, qm_scr, m_scr, acc_scr, *, hd, pairs_per_loop):
    tq = x_ref.shape[0]
    d = x_ref.shape[1]
    tk = vt_ref.shape[2]
    qi = pl.program_id(1)
    lane_l = lax.broadcasted_iota(jnp.int32, (1, LANE), 1)
    first_l = lane_l < hd
    first_r = lax.broadcasted_iota(jnp.int32, (LANE, 1), 0) < hd
    on_or_below = (lax.broadcasted_iota(jnp.int32, (tk, tq), 0)
                   <= lax.broadcasted_iota(jnp.int32, (tk, tq), 1))
    zero_b = jnp.zeros((), BF16)
    one_b = jnp.ones((), BF16)

    lanes = lambda p: slice(p * LANE, (p + 1) * LANE)
    for p0 in range(0, d // LANE, pairs_per_loop):
        pairs = range(p0, p0 + pairs_per_loop)
        heads = [2 * p + i for p in pairs for i in range(2)]
        qa = qa_ref[...]
        for p in pairs:
            qp = q_ref[:, lanes(p)]
            for i, qh in enumerate((jnp.where(first_l, qp, zero_b), jnp.where(first_l, zero_b, qp))):
                h = 2 * p + i
                own = (lane_l >= BIAS_COLS * h) & (lane_l < BIAS_COLS * (h + 1))
                qm_scr[h] = jnp.concatenate([qh, jnp.where(own, qa, zero_b)], axis=1)
        for h in heads:
            m_scr[h] = jnp.full((1, tq), NEG, F32)
            acc_scr[h] = jnp.zeros((LANE, tq), F32)

        def step(j, diagonal, pairs=pairs, heads=heads):
            start = pl.multiple_of(j * tk, tk)
            sc = {}
            kaj = ka_ref[pl.ds(start, tk), :]
            for p in pairs:
                kj = jnp.concatenate([k_ref[pl.ds(start, tk), lanes(p)], kaj], axis=1)
                for h in (2 * p, 2 * p + 1):
                    sc[h] = lax.dot_general(kj, qm_scr[h], (((1,), (1,)), ((), ())), preferred_element_type=F32)
            if diagonal:
                sc = {h: jnp.where(on_or_below, sc[h], NEG) for h in heads}
            m_old = {h: m_scr[h] for h in heads}
            m_new = {h: jnp.maximum(m_old[h], jnp.max(sc[h], axis=0, keepdims=True)) for h in heads}
            pr = {h: jnp.exp2(sc[h] - m_new[h]).astype(BF16) for h in heads}
            for p in pairs:
                vtj = vt_ref[j, lanes(p), :]
                vt2 = (jnp.where(first_r, vtj, one_b), jnp.where(first_r, one_b, vtj))
                for i in range(2):
                    h = 2 * p + i
                    acc_scr[h] = (jnp.exp2(m_old[h] - m_new[h]) * acc_scr[h]
                                  + jnp.dot(vt2[i], pr[h], preferred_element_type=F32))
                    m_scr[h] = m_new[h]

        def body(j, c):
            step(j, False)
            return c

        lax.fori_loop(0, qi, body, 0)
        step(qi, True)
        for p in pairs:
            acc0, acc1 = acc_scr[2 * p], acc_scr[2 * p + 1]
            o_t = jnp.where(first_r, acc0 / acc0[hd:hd + 1, :], acc1 / acc1[0:1, :])
            obuf[:, lanes(p)] = o_t.T.astype(BF16)

    out = jnp.dot(obuf[...], ow_ref[...], preferred_element_type=F32)
    o_ref[...] = x_ref[...] + (1.0 + mod_ref[2:3, :]) * out


def _fox(x, mod, nw, qkvf_w, fgate_b, q_norm_w, k_norm_w, out_w):
    b, s, d = x.shape
    hd = HEAD_DIM
    nh = d // hd
    assert qkvf_w.shape[1] == 3 * d + nh and nh <= LANE and 2 * hd == LANE
    wq, wk, wv = (qkvf_w[:, i * d:(i + 1) * d].astype(BF16) for i in range(3))
    wf = jnp.pad(qkvf_w[:, 3 * d:], ((0, 0), (0, LANE - nh))).astype(BF16)
    fb = jnp.pad(fgate_b, (0, LANE - nh)).reshape(1, LANE)
    z = jnp.zeros((d,), F32)
    vecs = jnp.stack([jnp.tile(q_norm_w, nh), jnp.tile(k_norm_w, nh), z, z, z, z, z, z])
    ind, indt = _head_indicator(d, hd)
    t = ROW_TILE
    res = lambda arr: _resident(arr.shape)
    nt = s // t
    assert BIAS_COLS * nh <= LANE
    col = jnp.arange(LANE)[None, None, :]
    place = (col == BIAS_COLS * jnp.arange(LANE)[None, :, None] + jnp.arange(BIAS_COLS)[:, None, None])
    place = (place & (jnp.arange(LANE)[None, :, None] < nh)).astype(BF16)
    third = (jnp.arange(LANE) % BIAS_COLS) // 3
    used = jnp.arange(LANE) < BIAS_COLS * nh
    z1 = jnp.zeros((LANE,), F32)
    ones = jnp.stack([(used & (third == 1)).astype(F32), (used & (third == 0)).astype(F32), z1, z1, z1, z1, z1, z1])
    q, k, vt, qa, ka = pl.pallas_call(
        functools.partial(_f_in_kernel, hd=hd),
        grid=(b, nt),
        in_specs=[_rows(t, d), _mod_spec(d), _resident((1, d)), _resident((d, d)), _resident((d, d)),
                  _resident((d, d)), _resident((d, LANE)), _resident((8, d)), _resident((1, LANE)),
                  res(ind), res(indt), res(place), res(ones)],
        out_specs=[_rows(t, d), _rows(t, d), pl.BlockSpec((None, None, d, t), lambda bi, si: (bi, si, 0, 0)),
                   _rows(t, LANE), _rows(t, LANE)],
        out_shape=[jax.ShapeDtypeStruct((b, s, d), BF16), jax.ShapeDtypeStruct((b, s, d), BF16),
                   jax.ShapeDtypeStruct((b, nt, d, t), BF16), jax.ShapeDtypeStruct((b, s, LANE), BF16),
                   jax.ShapeDtypeStruct((b, s, LANE), BF16)],
        scratch_shapes=[pltpu.VMEM((8, LANE), F32)],
        compiler_params=_params(True),
        name="f_in",
    )(x, mod, nw.reshape(1, d), wq, wk, wv.T, wf, vecs, fb, ind, indt, place, ones)

    whole = lambda *shape: pl.BlockSpec((None,) + shape, lambda bi, si: (bi,) + (0,) * len(shape))
    return pl.pallas_call(
        functools.partial(_f_att_kernel, hd=hd, pairs_per_loop=ATT_PAIRS_PER_LOOP),
        grid=(b, nt),
        in_specs=[_rows(t, d), whole(s, d), whole(nt, d, t), _rows(t, LANE), whole(s, LANE),
                  _rows(t, d), _mod_spec(d), _resident((d, d))],
        out_specs=_rows(t, d),
        out_shape=jax.ShapeDtypeStruct(x.shape, F32),
        scratch_shapes=[pltpu.VMEM((t, d), BF16), pltpu.VMEM((nh, t, 2 * LANE), BF16),
                        pltpu.VMEM((nh, 1, t), F32), pltpu.VMEM((nh, LANE, t), F32)],
        compiler_params=_params(False),
        name="f_att",
    )(q, k, vt, qa, ka, x, mod, out_w.astype(BF16))


def kernel(x, c, ada_w, ada_b, norm1_w, norm2_w, ffn_w1, ffn_w3, ffn_w2, m_in_w, m_conv_w, m_conv_b, m_dt_bias, m_A_log, m_D, m_norm_w, m_out_w, r_mix, r_w_rkv, r_w0, r_w1, r_w2, r_a0, r_a1, r_a2, r_g1, r_g2, r_k_k, r_k_a, r_r_k, r_lnx_w, r_lnx_b, r_out_w, f_qkvf_w, f_fgate_b, f_q_norm_w, f_k_norm_w, f_out_w):
    depth = ada_w.shape[0]
    mod = _ada(c, ada_w, ada_b)
    ia = ib = ic = 0
    for i in range(depth):
        kind = i % 3
        if kind == 0:
            x = _mamba(x, mod[i], norm1_w[i], m_in_w[ia], m_conv_w[ia], m_conv_b[ia], m_dt_bias[ia],
                       m_A_log[ia], m_D[ia], m_norm_w[ia], m_out_w[ia])
            ia += 1
        elif kind == 1:
            x = _rwkv(x, mod[i], norm1_w[i], r_mix[ib], r_w_rkv[ib], r_w0[ib], r_w1[ib], r_w2[ib], r_a0[ib],
                      r_a1[ib], r_a2[ib], r_g1[ib], r_g2[ib], r_k_k[ib], r_k_a[ib], r_r_k[ib],
                      r_lnx_w[ib], r_lnx_b[ib], r_out_w[ib])
            ib += 1
        else:
            x = _fox(x, mod[i], norm1_w[i], f_qkvf_w[ic], f_fgate_b[ic], f_q_norm_w[ic], f_k_norm_w[ic],
                     f_out_w[ic])
            ic += 1
        x = _ffn(x, mod[i], norm2_w[i], ffn_w1[i], ffn_w3[i], ffn_w2[i])
    return x
```

```python
import functools

import jax
import jax.numpy as jnp
from jax import lax
from jax.experimental import pallas as pl
from jax.experimental.pallas import tpu as pltpu

F32 = jnp.float32
BF16 = jnp.bfloat16

RMS_EPS = 1e-6
GN_EPS = 64e-5
KK_EPS = 1e-12
NEG = -1e30
LOG2E = 1.4426950408889634

V7X_VMEM_LIMIT = 56 * 1024 * 1024

HEAD_DIM = 64
SSD_GROUPS = 8
SSD_STATE = 128
SSD_CHUNK = 128
WKV_CHUNK = 64
WKV_GROUP = 256
WKV_BATCH_ROWS = 2
ROW_TILE = 256
FFN_ROW_TILE = 512
ATT_PAIRS_PER_LOOP = 4
BIAS_COLS = 6
LANE = 128


def _mm(a, b):
    return jnp.dot(a.astype(BF16), b.astype(BF16), preferred_element_type=F32)


def _mm_nt(a, b):
    return lax.dot_general(a.astype(BF16), b.astype(BF16), (((1,), (1,)), ((), ())),
                           preferred_element_type=F32)


def _mm_tn(a, b):
    return jnp.dot(a.T.astype(BF16), b.astype(BF16), preferred_element_type=F32)


def _split(x, terms):
    parts = []
    for i in range(terms):
        p = x.astype(BF16)
        parts.append(p)
        if i + 1 < terms:
            x = x - p.astype(F32)
    return parts


def _mm_sel_l(sel, x, terms=3):
    return sum(jnp.dot(sel, p, preferred_element_type=F32) for p in _split(x, terms))


def _mm_sel_r(x, sel, terms=3):
    return sum(jnp.dot(p, sel, preferred_element_type=F32) for p in _split(x, terms))


def _mm_hi(a, b):
    ah = a.astype(BF16)
    al = (a - ah.astype(F32)).astype(BF16)
    bh = b.astype(BF16)
    bl = (b - bh.astype(F32)).astype(BF16)
    d = lambda p, q: jnp.dot(p, q, preferred_element_type=F32)
    return d(ah, bh) + d(ah, bl) + d(al, bh)


def _sigmoid(x):
    return jax.nn.sigmoid(x)


def _silu(x):
    hx = 0.5 * x
    return hx + hx * jnp.tanh(hx)


def _softplus(x):
    return jnp.maximum(x, 0.0) + jnp.log1p(jnp.exp(-jnp.abs(x)))


def _norm_mod(x, nw, scale, shift):
    y = x * lax.rsqrt(jnp.mean(x * x, axis=-1, keepdims=True) + RMS_EPS)
    return (y * nw) * (1.0 + scale) + shift


def _tril(n, strict=False):
    r = lax.broadcasted_iota(jnp.int32, (n, n), 0)
    c = lax.broadcasted_iota(jnp.int32, (n, n), 1)
    return (r > c) if strict else (r >= c)


def _resident(shape):
    nd = len(shape)
    return pl.BlockSpec(shape, lambda *_: (0,) * nd, pipeline_mode=pl.Buffered(1))


def _rows(t, width):
    return pl.BlockSpec((None, t, width), lambda b, s: (b, s, 0))


def _mod_spec(d):
    return pl.BlockSpec((None, 6, d), lambda b, s: (b, 0, 0))


def _params(seq_axis_carries):
    sem = ("parallel", "arbitrary") if seq_axis_carries else ("parallel", "parallel")
    return pltpu.CompilerParams(dimension_semantics=sem, vmem_limit_bytes=V7X_VMEM_LIMIT)


def _head_indicator(d, hd):
    ind = (jnp.arange(d)[:, None] // hd == jnp.arange(LANE)[None, :]).astype(BF16)
    return ind, ind.T


def _ada_kernel(c_ref, w_ref, b_ref, o_ref):
    c = c_ref[...]
    o_ref[...] = _mm_hi(_silu(c), w_ref[...]) + b_ref[...]


def _ada(c, ada_w, ada_b):
    depth, d, n6 = ada_w.shape
    b = c.shape[0]
    tn = 1536
    out = pl.pallas_call(
        _ada_kernel,
        grid=(depth, n6 // tn),
        in_specs=[pl.BlockSpec((b, d), lambda l, j: (0, 0)),
                  pl.BlockSpec((None, d, tn), lambda l, j: (l, 0, j)),
                  pl.BlockSpec((None, 1, tn), lambda l, j: (l, 0, j))],
        out_specs=pl.BlockSpec((None, b, tn), lambda l, j: (l, 0, j)),
        out_shape=jax.ShapeDtypeStruct((depth, b, n6), F32),
        compiler_params=_params(False),
        name="ada",
    )(c, ada_w, ada_b.reshape(depth, 1, n6))
    return out.reshape(depth, b, 6, d)


def _ffn_kernel(x_ref, mod_ref, nw_ref, w1_ref, w3_ref, w2_ref, o_ref):
    x = x_ref[...]
    h = _norm_mod(x, nw_ref[...], mod_ref[4:5, :], mod_ref[3:4, :]).astype(BF16)
    a = jnp.dot(h, w1_ref[...], preferred_element_type=F32)
    b = jnp.dot(h, w3_ref[...], preferred_element_type=F32)
    g = (_silu(a) * b).astype(BF16)
    y = jnp.dot(g, w2_ref[...], preferred_element_type=F32)
    o_ref[...] = x + (1.0 + mod_ref[5:6, :]) * y


def _ffn(x, mod, nw, w1, w3, w2):
    b, s, d = x.shape
    dff = w1.shape[1]
    t = FFN_ROW_TILE if s % FFN_ROW_TILE == 0 else ROW_TILE
    return pl.pallas_call(
        _ffn_kernel,
        grid=(b, s // t),
        in_specs=[_rows(t, d), _mod_spec(d), _resident((1, d)),
                  _resident((d, dff)), _resident((d, dff)), _resident((dff, d))],
        out_specs=_rows(t, d),
        out_shape=jax.ShapeDtypeStruct(x.shape, F32),
        compiler_params=_params(False),
        name="ffn",
    )(x, mod, nw.reshape(1, d), w1.astype(BF16), w3.astype(BF16), w2.astype(BF16))


def _m_in_kernel(x_ref, mod_ref, nw_ref, wz_ref, wx_ref, wd_ref, z_ref, xbc_ref, dt_ref):
    h = _norm_mod(x_ref[...], nw_ref[...], mod_ref[1:2, :], mod_ref[0:1, :]).astype(BF16)
    z_ref[...] = jnp.dot(h, wz_ref[...], preferred_element_type=F32)
    xbc_ref[...] = jnp.dot(h, wx_ref[...], preferred_element_type=F32)
    dt_ref[...] = jnp.dot(h, wd_ref[...], preferred_element_type=F32)


def _m_ssd_kernel(xbc_ref, z_ref, dt_ref, x_ref, mod_ref, cw_ref, cb_ref, dtb_ref, alog_ref, e_ref,
                  dskip_ref, gnw_ref, ow_ref, o_ref, ubuf, st, ybuf, *, nh):
    t = x_ref.shape[0]
    di = z_ref.shape[1]
    ng, ns, gw = st.shape
    hpg = nh // ng
    hd = gw // hpg

    @pl.when(pl.program_id(1) == 0)
    def _():
        ubuf[0:8, :] = jnp.zeros((8, ubuf.shape[1]), F32)
        st[...] = jnp.zeros(st.shape, F32)

    u = xbc_ref[...]
    ubuf[8:8 + t, :] = u
    acc = cb_ref[...] + cw_ref[3:4, :] * u
    for k in range(3):
        acc = acc + cw_ref[k:k + 1, :] * ubuf[5 + k:5 + k + t, :]
    ubuf[0:8, :] = u[t - 8:t, :]
    xc = _silu(acc)
    xs = xc[:, :di]

    lane = lax.broadcasted_iota(jnp.int32, (1, LANE), 1)
    dt = _softplus(dt_ref[...] + dtb_ref[...])
    a_neg = jnp.where(lane < nh, -jnp.exp(alog_ref[...]), 0.0)
    a = dt * a_neg
    tril = _tril(t)
    cs = _mm_sel_l(jnp.where(tril, 1.0, 0.0).astype(BF16), a)
    cs_t = cs.T
    e = e_ref[...]
    dt_e = _mm_sel_r(dt, e, 2)
    cs_e = _mm_sel_r(cs, e, 2)
    cs_last = cs_e[t - 1:t, :]
    xd = xs * dt_e
    ecs = jnp.exp(cs_e)
    xdd = xd * jnp.exp(cs_last - cs_e)
    cdec = jnp.exp(cs_last)
    yskip = xs * dskip_ref[...]
    gate = _silu(z_ref[...])

    rr = lax.broadcasted_iota(jnp.int32, (hpg * t, gw), 0) // t
    ll = lax.broadcasted_iota(jnp.int32, (hpg * t, gw), 1) // hd
    bd = rr == ll
    groups = range(ng)
    sls = [slice(g * gw, (g + 1) * gw) for g in groups]
    bgs = [xc[:, di + g * ns:di + (g + 1) * ns].astype(BF16) for g in groups]
    cgs = [xc[:, di + (ng + g) * ns:di + (ng + g + 1) * ns].astype(BF16) for g in groups]
    scores = [_mm_nt(cgs[g], bgs[g]) for g in groups]
    sts = [st[g] for g in groups]
    yoff = [_mm(cgs[g], sts[g]) for g in groups]
    supd = [_mm_tn(bgs[g], xdd[:, sls[g]]) for g in groups]
    ydiag = []
    for g in groups:
        parts = []
        for j in range(hpg):
            h = g * hpg + j
            diff = cs[:, h:h + 1] - cs_t[h:h + 1, :]
            parts.append((scores[g] * jnp.exp(jnp.where(tril, diff, NEG))).astype(BF16))
        lhs = jnp.concatenate(parts, axis=1)
        rhs = jnp.where(bd, jnp.concatenate([xd[:, sls[g]]] * hpg, axis=0), 0.0).astype(BF16)
        ydiag.append(jnp.dot(lhs, rhs, preferred_element_type=F32))
    for g in groups:
        sl = sls[g]
        st[g] = sts[g] * cdec[:, sl] + supd[g]
        y = (ydiag[g] + yoff[g] * ecs[:, sl] + yskip[:, sl]) * gate[:, sl]
        y = y * lax.rsqrt(jnp.mean(y * y, axis=-1, keepdims=True) + RMS_EPS) * gnw_ref[:, sl]
        ybuf[:, sl] = y.astype(BF16)

    out = jnp.dot(ybuf[...], ow_ref[...], preferred_element_type=F32)
    o_ref[...] = x_ref[...] + (1.0 + mod_ref[2:3, :]) * out


def _mamba(x, mod, nw, in_w, conv_w, conv_b, dt_bias, a_log, d_skip, norm_w, out_w):
    b, s, d = x.shape
    nh = dt_bias.shape[0]
    di = nh * HEAD_DIM
    cdim = conv_w.shape[1]
    ng, ns = SSD_GROUPS, SSD_STATE
    gw = di // ng
    assert cdim == di + 2 * ng * ns and in_w.shape[1] == di + cdim + nh and nh <= LANE
    wz = in_w[:, :di].astype(BF16)
    wx = in_w[:, di:di + cdim].astype(BF16)
    wd = jnp.pad(in_w[:, di + cdim:], ((0, 0), (0, LANE - nh))).astype(BF16)
    t = ROW_TILE
    z, xbc, dtr = pl.pallas_call(
        _m_in_kernel,
        grid=(b, s // t),
        in_specs=[_rows(t, d), _mod_spec(d), _resident((1, d)),
                  _resident((d, di)), _resident((d, cdim)), _resident((d, LANE))],
        out_specs=[_rows(t, di), _rows(t, cdim), _rows(t, LANE)],
        out_shape=[jax.ShapeDtypeStruct((b, s, di), F32), jax.ShapeDtypeStruct((b, s, cdim), F32),
                   jax.ShapeDtypeStruct((b, s, LANE), F32)],
        compiler_params=_params(False),
        name="m_in",
    )(x, mod, nw.reshape(1, d), wz, wx, wd)

    pad1 = lambda v: jnp.pad(v, (0, LANE - nh)).reshape(1, LANE)
    expand = (jnp.arange(LANE)[:, None] == jnp.arange(di)[None, :] // HEAD_DIM).astype(BF16)
    tc = SSD_CHUNK
    return pl.pallas_call(
        functools.partial(_m_ssd_kernel, nh=nh),
        grid=(b, s // tc),
        in_specs=[_rows(tc, cdim), _rows(tc, di), _rows(tc, LANE), _rows(tc, d), _mod_spec(d),
                  _resident((4, cdim)), _resident((1, cdim)), _resident((1, LANE)), _resident((1, LANE)),
                  _resident((LANE, di)), _resident((1, di)), _resident((1, di)), _resident((di, d))],
        out_specs=_rows(tc, d),
        out_shape=jax.ShapeDtypeStruct(x.shape, F32),
        scratch_shapes=[pltpu.VMEM((tc + 8, cdim), F32), pltpu.VMEM((ng, ns, gw), F32),
                        pltpu.VMEM((tc, di), BF16)],
        compiler_params=_params(True),
        name="m_ssd",
    )(xbc, z, dtr, x, mod, conv_w, conv_b.reshape(1, cdim), pad1(dt_bias), pad1(a_log), expand,
      jnp.repeat(d_skip, HEAD_DIM).reshape(1, di), norm_w.reshape(1, di), out_w.astype(BF16))


def _r_in_kernel(x_ref, mod_ref, nw_ref, mix_ref, wr_ref, wk_ref, wv_ref, w1_ref, w2_ref, a1_ref, a2_ref,
                 g1_ref, g2_ref, vec_ref, ind_ref, indt_ref,
                 r_ref, lw_ref, k_ref, v_ref, kk_ref, bb_ref, bonus_ref, g_ref, hlast):
    t = x_ref.shape[0]

    @pl.when(pl.program_id(1) == 0)
    def _():
        hlast[...] = jnp.zeros(hlast.shape, F32)

    h = _norm_mod(x_ref[...], nw_ref[...], mod_ref[1:2, :], mod_ref[0:1, :])
    row = lax.broadcasted_iota(jnp.int32, h.shape, 0)
    hprev = jnp.where(row == 0, hlast[7:8, :], pltpu.roll(h, 1, 0))
    hlast[...] = h[t - 8:t, :]
    dx = hprev - h
    w0, a0, k_k, k_a, r_k = (vec_ref[i:i + 1, :] for i in range(5))
    ind, indt = ind_ref[...], indt_ref[...]
    headsum = lambda q: _mm_sel_r(_mm_sel_r(q, ind, 2), indt, 2)
    dotf = lambda a, b_ref: jnp.dot(a, b_ref[...], preferred_element_type=F32)
    mixed = lambda n: (h + dx * mix_ref[n:n + 1, :]).astype(BF16)

    r = dotf(mixed(0), wr_ref)
    k = dotf(mixed(1), wk_ref)
    v = dotf(mixed(2), wv_ref)
    wl = dotf(jnp.tanh(dotf(mixed(3), w1_ref)).astype(BF16), w2_ref)
    al = dotf(dotf(mixed(4), a1_ref).astype(BF16), a2_ref)
    r_ref[...] = r
    v_ref[...] = v
    g_ref[...] = dotf(_sigmoid(dotf(mixed(5), g1_ref)).astype(BF16), g2_ref)
    lw_ref[...] = -jnp.exp(-_softplus(-(w0 + wl)) - 0.5)
    a = _sigmoid(a0 + al)
    kk = k * k_k
    kk = kk / jnp.maximum(jnp.sqrt(headsum(kk * kk)), KK_EPS)
    k2 = k * (1.0 + (a - 1.0) * k_a)
    k_ref[...] = k2
    kk_ref[...] = kk
    bb_ref[...] = kk * a
    bonus_ref[...] = headsum(r * k2 * r_k) * v


def _r_wkv_kernel(r_ref, lw_ref, k_ref, v_ref, kk_ref, bb_ref, bonus_ref, g_ref, x_ref, mod_ref,
                  lnw_ref, lnb_ref, ind_ref, indt_ref, ow_ref, o_ref, state, ybuf, *, hd):
    nb, t, d = x_ref.shape
    gw = state.shape[1]
    ngr = d // gw
    hpg = gw // hd
    n = hpg * t

    @pl.when(pl.program_id(1) == 0)
    def _():
        state[...] = jnp.zeros(state.shape, F32)

    tri = jnp.where(_tril(t), 1.0, 0.0).astype(BF16)
    at, rt, kt, bt, v, ptot = [], [], [], [], [], []
    for bi in range(nb):
        lw = lw_ref[bi]
        cl = _mm_sel_l(tri, lw)
        ecl = jnp.exp(cl)
        encl = jnp.exp(-cl)
        rt.append(r_ref[bi] * ecl)
        at.append(-kk_ref[bi] * jnp.exp(cl - lw))
        kt.append(k_ref[bi] * encl)
        bt.append(bb_ref[bi] * encl)
        v.append(v_ref[bi])
        ptot.append(ecl[t - 1:t, :])

    ri = lax.broadcasted_iota(jnp.int32, (n, gw), 0)
    ci = lax.broadcasted_iota(jnp.int32, (n, gw), 1)
    headm = (ri // t) == (ci // hd)
    ri2 = lax.broadcasted_iota(jnp.int32, (n, n), 0)
    ci2 = lax.broadcasted_iota(jnp.int32, (n, n), 1)
    same = (ri2 // t) == (ci2 // t)
    tt = lax.broadcasted_iota(jnp.int32, (t, n), 0)
    ss = lax.broadcasted_iota(jnp.int32, (t, n), 1) % t
    strict = tt > ss
    incl = tt >= ss
    eye = jnp.where(tt == ss, 1.0, 0.0)
    vals_same = (lax.broadcasted_iota(jnp.int32, (gw, gw), 0) // hd) == (
        lax.broadcasted_iota(jnp.int32, (gw, gw), 1) // hd)

    def blockdiag(m, mask):
        return jnp.where(mask, jnp.concatenate([m] * hpg, axis=0), 0.0).astype(BF16)

    groups = range(nb * ngr)
    bis = [g // ngr for g in groups]
    sls = [slice((g % ngr) * gw, (g % ngr + 1) * gw) for g in groups]
    dot = lambda a, b: jnp.dot(a.astype(BF16), b, preferred_element_type=F32)
    ar = [jnp.concatenate([at[bi][:, sl], rt[bi][:, sl]], axis=0) for bi, sl in zip(bis, sls)]
    vst = [blockdiag(v[bi][:, sl], headm) for bi, sl in zip(bis, sls)]
    xbk = [jnp.concatenate([blockdiag(bt[bi][:, sl], headm), blockdiag(kt[bi][:, sl], headm)], axis=0)
           for bi, sl in zip(bis, sls)]
    gram = [_mm_nt(ar[g], xbk[g]) for g in groups]
    hs = [state[g] for g in groups]
    hproj = [_mm_nt(ar[g], hs[g]) for g in groups]
    a_ab = [jnp.where(strict, gram[g][0:t, 0:n], 0.0) for g in groups]
    rhs = [hproj[g][0:t] + dot(jnp.where(strict, gram[g][0:t, n:2 * n], 0.0), vst[g]) for g in groups]
    tinv = [eye + a_ab[g] for g in groups]
    p = [dot(a_ab[g], blockdiag(a_ab[g], same)) for g in groups]
    steps = max(t - 1, 1).bit_length() - 1
    for i in range(steps):
        last = i + 1 == steps
        for g in groups:
            tb = blockdiag(tinv[g], same)
            prod = dot(p[g], tb if last else jnp.concatenate([tb, blockdiag(p[g], same)], axis=1))
            tinv[g] = tinv[g] + prod[:, 0:n]
            if not last:
                p[g] = prod[:, n:2 * n]
    u = [dot(tinv[g], blockdiag(rhs[g], headm)) for g in groups]
    for g in groups:
        r_bk = jnp.where(jnp.concatenate([incl, incl], axis=1), gram[g][t:2 * t, :], 0.0)
        ybuf[bis[g] * t:(bis[g] + 1) * t, sls[g]] = hproj[g][t:2 * t] + dot(
            r_bk, jnp.concatenate([blockdiag(u[g], headm), vst[g]], axis=0))
    for g in groups:
        bi, sl = bis[g], sls[g]
        upd = _mm_tn(jnp.concatenate([u[g], v[bi][:, sl]], axis=0),
                     jnp.concatenate([bt[bi][:, sl], kt[bi][:, sl]], axis=0))
        state[g] = (hs[g] + jnp.where(vals_same, upd, 0.0)) * ptot[bi][:, sl]

    y = ybuf[...]
    ind, indt = ind_ref[...], indt_ref[...]
    headmean = lambda q: _mm_sel_r(_mm_sel_r(q, ind, 2), indt, 2) * (1.0 / hd)
    mu = headmean(y)
    yc = y - mu
    var = headmean(yc * yc)
    yn = yc * lax.rsqrt(var + GN_EPS) * lnw_ref[...] + lnb_ref[...]
    yo = (yn + bonus_ref[...].reshape(nb * t, d)) * g_ref[...].reshape(nb * t, d)
    out = _mm(yo, ow_ref[...])
    for bi in range(nb):
        o_ref[bi] = x_ref[bi] + (1.0 + mod_ref[bi, 2:3, :]) * out[bi * t:(bi + 1) * t]


def _rwkv(x, mod, nw, mix, w_rkv, w0, w1, w2, a0, a1, a2, g1, g2, k_k, k_a, r_k, lnx_w, lnx_b, out_w):
    b, s, d = x.shape
    hd = HEAD_DIM
    assert d % WKV_GROUP == 0 and d // hd <= LANE

    def pad_pair(p1, p2):
        r = p1.shape[1]
        rp = -(-r // LANE) * LANE
        return (jnp.pad(p1, ((0, 0), (0, rp - r))).astype(BF16), jnp.pad(p2, ((0, rp - r), (0, 0))).astype(BF16))

    w1p, w2p = pad_pair(w1, w2)
    a1p, a2p = pad_pair(a1, a2)
    g1p, g2p = pad_pair(g1, g2)
    vecs = jnp.stack([w0, a0, k_k, k_a, r_k.reshape(d), jnp.zeros_like(w0), jnp.zeros_like(w0), jnp.zeros_like(w0)])
    ind, indt = _head_indicator(d, hd)
    wb = w_rkv.astype(BF16)
    t = ROW_TILE
    res = lambda arr: _resident(arr.shape)
    outs = pl.pallas_call(
        _r_in_kernel,
        grid=(b, s // t),
        in_specs=[_rows(t, d), _mod_spec(d), _resident((1, d)), _resident((6, d)),
                  _resident((d, d)), _resident((d, d)), _resident((d, d)),
                  res(w1p), res(w2p), res(a1p), res(a2p), res(g1p), res(g2p),
                  _resident((8, d)), res(ind), res(indt)],
        out_specs=[_rows(t, d)] * 8,
        out_shape=[jax.ShapeDtypeStruct((b, s, d), F32)] * 8,
        scratch_shapes=[pltpu.VMEM((8, d), F32)],
        compiler_params=_params(True),
        name="r_in",
    )(x, mod, nw.reshape(1, d), mix, wb[0], wb[1], wb[2], w1p, w2p, a1p, a2p, g1p, g2p, vecs, ind, indt)

    tc = WKV_CHUNK
    nb = WKV_BATCH_ROWS if b % WKV_BATCH_ROWS == 0 else 1
    rows = pl.BlockSpec((nb, tc, d), lambda bi, si: (bi, si, 0))
    return pl.pallas_call(
        functools.partial(_r_wkv_kernel, hd=hd),
        grid=(b // nb, s // tc),
        in_specs=[rows] * 9 + [pl.BlockSpec((nb, 6, d), lambda bi, si: (bi, 0, 0)), _resident((1, d)),
                               _resident((1, d)), res(ind), res(indt), _resident((d, d))],
        out_specs=rows,
        out_shape=jax.ShapeDtypeStruct(x.shape, F32),
        scratch_shapes=[pltpu.VMEM((nb * d // WKV_GROUP, WKV_GROUP, WKV_GROUP), F32),
                        pltpu.VMEM((nb * tc, d), F32)],
        compiler_params=_params(True),
        name="r_wkv",
    )(*outs, x, mod, lnx_w.reshape(1, d), lnx_b.reshape(1, d), ind, indt, out_w.astype(BF16))


def _f_in_kernel(x_ref, mod_ref, nw_ref, wq_ref, wk_ref, wvt_ref, wf_ref, vec_ref, fb_ref, ind_ref, indt_ref,
                 place_ref, ones_ref, q_ref, k_ref, vt_ref, qa_ref, ka_ref, carry, *, hd):
    t = x_ref.shape[0]

    @pl.when(pl.program_id(1) == 0)
    def _():
        carry[...] = jnp.zeros(carry.shape, F32)

    h = _norm_mod(x_ref[...], nw_ref[...], mod_ref[1:2, :], mod_ref[0:1, :]).astype(BF16)
    ind, indt = ind_ref[...], indt_ref[...]

    def head_rms(q, w):
        ms = _mm_sel_r(_mm_sel_r(q * q, ind, 2), indt, 2) * (1.0 / hd)
        return q * lax.rsqrt(ms + RMS_EPS) * w

    q = head_rms(jnp.dot(h, wq_ref[...], preferred_element_type=F32), vec_ref[0:1, :])
    k = head_rms(jnp.dot(h, wk_ref[...], preferred_element_type=F32), vec_ref[1:2, :])
    q_ref[...] = (q * (hd ** -0.5 * LOG2E)).astype(BF16)
    k_ref[...] = k.astype(BF16)
    vt_ref[...] = _mm_nt(wvt_ref[...], h).astype(BF16)
    f = jnp.dot(h, wf_ref[...], preferred_element_type=F32) + fb_ref[...]
    logf = -_softplus(-f)
    cum = _mm_sel_l(jnp.where(_tril(t), 1.0, 0.0).astype(BF16), logf) + carry[0:1, :]
    carry[...] = jnp.broadcast_to(cum[t - 1:t, :], carry.shape)
    pieces = _split(cum * LOG2E, 3)
    place = lambda first: sum(jnp.dot(p, place_ref[first + i], preferred_element_type=F32)
                              for i, p in enumerate(pieces))
    qa_ref[...] = (place(0) + ones_ref[0:1, :]).astype(BF16)
    ka_ref[...] = (ones_ref[1:2, :] - place(3)).astype(BF16)


def _f_att_kernel(q_ref, k_ref, vt_ref, qa_ref, ka_ref, x_ref, mod_ref, ow_ref, o_ref,
                  obuf, qm_scr, m_scr, acc_scr, *, hd, pairs_per_loop):
    tq = x_ref.shape[0]
    d = x_ref.shape[1]
    tk = vt_ref.shape[2]
    qi = pl.program_id(1)
    lane_l = lax.broadcasted_iota(jnp.int32, (1, LANE), 1)
    first_l = lane_l < hd
    first_r = lax.broadcasted_iota(jnp.int32, (LANE, 1), 0) < hd
    on_or_below = (lax.broadcasted_iota(jnp.int32, (tk, tq), 0)
                   <= lax.broadcasted_iota(jnp.int32, (tk, tq), 1))
    zero_b = jnp.zeros((), BF16)
    one_b = jnp.ones((), BF16)

    lanes = lambda p: slice(p * LANE, (p + 1) * LANE)
    for p0 in range(0, d // LANE, pairs_per_loop):
        pairs = range(p0, p0 + pairs_per_loop)
        heads = [2 * p + i for p in pairs for i in range(2)]
        qa = qa_ref[...]
        for p in pairs:
            qp = q_ref[:, lanes(p)]
            for i, qh in enumerate((jnp.where(first_l, qp, zero_b), jnp.where(first_l, zero_b, qp))):
                h = 2 * p + i
                own = (lane_l >= BIAS_COLS * h) & (lane_l < BIAS_COLS * (h + 1))
                qm_scr[h] = jnp.concatenate([qh, jnp.where(own, qa, zero_b)], axis=1)
        for h in heads:
            m_scr[h] = jnp.full((1, tq), NEG, F32)
            acc_scr[h] = jnp.zeros((LANE, tq), F32)

        def step(j, diagonal, pairs=pairs, heads=heads):
            start = pl.multiple_of(j * tk, tk)
            sc = {}
            kaj = ka_ref[pl.ds(start, tk), :]
            for p in pairs:
                kj = jnp.concatenate([k_ref[pl.ds(start, tk), lanes(p)], kaj], axis=1)
                for h in (2 * p, 2 * p + 1):
                    sc[h] = lax.dot_general(kj, qm_scr[h], (((1,), (1,)), ((), ())), preferred_element_type=F32)
            if diagonal:
                sc = {h: jnp.where(on_or_below, sc[h], NEG) for h in heads}
            m_old = {h: m_scr[h] for h in heads}
            m_new = {h: jnp.maximum(m_old[h], jnp.max(sc[h], axis=0, keepdims=True)) for h in heads}
            pr = {h: jnp.exp2(sc[h] - m_new[h]).astype(BF16) for h in heads}
            for p in pairs:
                vtj = vt_ref[j, lanes(p), :]
                vt2 = (jnp.where(first_r, vtj, one_b), jnp.where(first_r, one_b, vtj))
                for i in range(2):
                    h = 2 * p + i
                    acc_scr[h] = (jnp.exp2(m_old[h] - m_new[h]) * acc_scr[h]
                                  + jnp.dot(vt2[i], pr[h], preferred_element_type=F32))
                    m_scr[h] = m_new[h]

        def body(j, c):
            step(j, False)
            return c

        lax.fori_loop(0, qi, body, 0)
        step(qi, True)
        for p in pairs:
            acc0, acc1 = acc_scr[2 * p], acc_scr[2 * p + 1]
            o_t = jnp.where(first_r, acc0 / acc0[hd:hd + 1, :], acc1 / acc1[0:1, :])
            obuf[:, lanes(p)] = o_t.T.astype(BF16)

    out = jnp.dot(obuf[...], ow_ref[...], preferred_element_type=F32)
    o_ref[...] = x_ref[...] + (1.0 + mod_ref[2:3, :]) * out


def _fox(x, mod, nw, qkvf_w, fgate_b, q_norm_w, k_norm_w, out_w):
    b, s, d = x.shape
    hd = HEAD_DIM
    nh = d // hd
    assert qkvf_w.shape[1] == 3 * d + nh and nh <= LANE and 2 * hd == LANE
    wq, wk, wv = (qkvf_w[:, i * d:(i + 1) * d].astype(BF16) for i in range(3))
    wf = jnp.pad(qkvf_w[:, 3 * d:], ((0, 0), (0, LANE - nh))).astype(BF16)
    fb = jnp.pad(fgate_b, (0, LANE - nh)).reshape(1, LANE)
    z = jnp.zeros((d,), F32)
    vecs = jnp.stack([jnp.tile(q_norm_w, nh), jnp.tile(k_norm_w, nh), z, z, z, z, z, z])
    ind, indt = _head_indicator(d, hd)
    t = ROW_TILE
    res = lambda arr: _resident(arr.shape)
    nt = s // t
    assert BIAS_COLS * nh <= LANE
    col = jnp.arange(LANE)[None, None, :]
    place = (col == BIAS_COLS * jnp.arange(LANE)[None, :, None] + jnp.arange(BIAS_COLS)[:, None, None])
    place = (place & (jnp.arange(LANE)[None, :, None] < nh)).astype(BF16)
    third = (jnp.arange(LANE) % BIAS_COLS) // 3
    used = jnp.arange(LANE) < BIAS_COLS * nh
    z1 = jnp.zeros((LANE,), F32)
    ones = jnp.stack([(used & (third == 1)).astype(F32), (used & (third == 0)).astype(F32), z1, z1, z1, z1, z1, z1])
    q, k, vt, qa, ka = pl.pallas_call(
        functools.partial(_f_in_kernel, hd=hd),
        grid=(b, nt),
        in_specs=[_rows(t, d), _mod_spec(d), _resident((1, d)), _resident((d, d)), _resident((d, d)),
                  _resident((d, d)), _resident((d, LANE)), _resident((8, d)), _resident((1, LANE)),
                  res(ind), res(indt), res(place), res(ones)],
        out_specs=[_rows(t, d), _rows(t, d), pl.BlockSpec((None, None, d, t), lambda bi, si: (bi, si, 0, 0)),
                   _rows(t, LANE), _rows(t, LANE)],
        out_shape=[jax.ShapeDtypeStruct((b, s, d), BF16), jax.ShapeDtypeStruct((b, s, d), BF16),
                   jax.ShapeDtypeStruct((b, nt, d, t), BF16), jax.ShapeDtypeStruct((b, s, LANE), BF16),
                   jax.ShapeDtypeStruct((b, s, LANE), BF16)],
        scratch_shapes=[pltpu.VMEM((8, LANE), F32)],
        compiler_params=_params(True),
        name="f_in",
    )(x, mod, nw.reshape(1, d), wq, wk, wv.T, wf, vecs, fb, ind, indt, place, ones)

    whole = lambda *shape: pl.BlockSpec((None,) + shape, lambda bi, si: (bi,) + (0,) * len(shape))
    return pl.pallas_call(
        functools.partial(_f_att_kernel, hd=hd, pairs_per_loop=ATT_PAIRS_PER_LOOP),
        grid=(b, nt),
        in_specs=[_rows(t, d), whole(s, d), whole(nt, d, t), _rows(t, LANE), whole(s, LANE),
                  _rows(t, d), _mod_spec(d), _resident((d, d))],
        out_specs=_rows(t, d),
        out_shape=jax.ShapeDtypeStruct(x.shape, F32),
        scratch_shapes=[pltpu.VMEM((t, d), BF16), pltpu.VMEM((nh, t, 2 * LANE), BF16),
                        pltpu.VMEM((nh, 1, t), F32), pltpu.VMEM((nh, LANE, t), F32)],
        compiler_params=_params(False),
        name="f_att",
    )(q, k, vt, qa, ka, x, mod, out_w.astype(BF16))


def kernel(x, c, ada_w, ada_b, norm1_w, norm2_w, ffn_w1, ffn_w3, ffn_w2, m_in_w, m_conv_w, m_conv_b, m_dt_bias, m_A_log, m_D, m_norm_w, m_out_w, r_mix, r_w_rkv, r_w0, r_w1, r_w2, r_a0, r_a1, r_a2, r_g1, r_g2, r_k_k, r_k_a, r_r_k, r_lnx_w, r_lnx_b, r_out_w, f_qkvf_w, f_fgate_b, f_q_norm_w, f_k_norm_w, f_out_w):
    depth = ada_w.shape[0]
    mod = _ada(c, ada_w, ada_b)
    ia = ib = ic = 0
    for i in range(depth):
        kind = i % 3
        if kind == 0:
            x = _mamba(x, mod[i], norm1_w[i], m_in_w[ia], m_conv_w[ia], m_conv_b[ia], m_dt_bias[ia],
                       m_A_log[ia], m_D[ia], m_norm_w[ia], m_out_w[ia])
            ia += 1
        elif kind == 1:
            x = _rwkv(x, mod[i], norm1_w[i], r_mix[ib], r_w_rkv[ib], r_w0[ib], r_w1[ib], r_w2[ib], r_a0[ib],
                      r_a1[ib], r_a2[ib], r_g1[ib], r_g2[ib], r_k_k[ib], r_k_a[ib], r_r_k[ib],
                      r_lnx_w[ib], r_lnx_b[ib], r_out_w[ib])
            ib += 1
        else:
            x = _fox(x, mod[i], norm1_w[i], f_qkvf_w[ic], f_fgate_b[ic], f_q_norm_w[ic], f_k_norm_w[ic],
                     f_out_w[ic])
            ic += 1
        x = _ffn(x, mod[i], norm2_w[i], ffn_w1[i], ffn_w3[i], ffn_w2[i])
    return x
```

```python
import functools

import jax
import jax.numpy as jnp
from jax import lax
from jax.experimental import pallas as pl
from jax.experimental.pallas import tpu as pltpu

F32 = jnp.float32
BF16 = jnp.bfloat16

RMS_EPS = 1e-6
GN_EPS = 64e-5
KK_EPS = 1e-12
NEG = -1e30
LOG2E = 1.4426950408889634

V7X_VMEM_LIMIT = 56 * 1024 * 1024

HEAD_DIM = 64
SSD_GROUPS = 8
SSD_STATE = 128
SSD_CHUNK = 128
WKV_CHUNK = 64
WKV_GROUP = 256
WKV_BATCH_ROWS = 2
ROW_TILE = 256
FFN_ROW_TILE = 512
ATT_PAIRS_PER_LOOP = 8
BIAS_COLS = 6
LANE = 128


def _mm(a, b):
    return jnp.dot(a.astype(BF16), b.astype(BF16), preferred_element_type=F32)


def _mm_nt(a, b):
    return lax.dot_general(a.astype(BF16), b.astype(BF16), (((1,), (1,)), ((), ())),
                           preferred_element_type=F32)


def _mm_tn(a, b):
    return jnp.dot(a.T.astype(BF16), b.astype(BF16), preferred_element_type=F32)


def _split(x, terms):
    parts = []
    for i in range(terms):
        p = x.astype(BF16)
        parts.append(p)
        if i + 1 < terms:
            x = x - p.astype(F32)
    return parts


def _mm_sel_l(sel, x, terms=3):
    return sum(jnp.dot(sel, p, preferred_element_type=F32) for p in _split(x, terms))


def _mm_sel_r(x, sel, terms=3):
    return sum(jnp.dot(p, sel, preferred_element_type=F32) for p in _split(x, terms))


def _mm_hi(a, b):
    ah = a.astype(BF16)
    al = (a - ah.astype(F32)).astype(BF16)
    bh = b.astype(BF16)
    bl = (b - bh.astype(F32)).astype(BF16)
    d = lambda p, q: jnp.dot(p, q, preferred_element_type=F32)
    return d(ah, bh) + d(ah, bl) + d(al, bh)


def _sigmoid(x):
    return jax.nn.sigmoid(x)


def _silu(x):
    hx = 0.5 * x
    return hx + hx * jnp.tanh(hx)


def _softplus(x):
    return jnp.maximum(x, 0.0) + jnp.log(1.0 + jnp.exp(-jnp.abs(x)))


def _norm_mod(x, nw, scale, shift):
    y = x * lax.rsqrt(jnp.mean(x * x, axis=-1, keepdims=True) + RMS_EPS)
    return (y * nw) * (1.0 + scale) + shift


def _tril(n, strict=False):
    r = lax.broadcasted_iota(jnp.int32, (n, n), 0)
    c = lax.broadcasted_iota(jnp.int32, (n, n), 1)
    return (r > c) if strict else (r >= c)


def _resident(shape):
    nd = len(shape)
    return pl.BlockSpec(shape, lambda *_: (0,) * nd, pipeline_mode=pl.Buffered(1))


def _rows(t, width):
    return pl.BlockSpec((None, t, width), lambda b, s: (b, s, 0))


def _mod_spec(d):
    return pl.BlockSpec((None, 6, d), lambda b, s: (b, 0, 0))


def _params(seq_axis_carries):
    sem = ("parallel", "arbitrary") if seq_axis_carries else ("parallel", "parallel")
    return pltpu.CompilerParams(dimension_semantics=sem, vmem_limit_bytes=V7X_VMEM_LIMIT)


def _head_indicator(d, hd):
    ind = (jnp.arange(d)[:, None] // hd == jnp.arange(LANE)[None, :]).astype(BF16)
    return ind, ind.T


def _ada_kernel(c_ref, w_ref, b_ref, o_ref):
    c = c_ref[...]
    o_ref[...] = _mm_hi(_silu(c), w_ref[...]) + b_ref[...]


def _ada(c, ada_w, ada_b):
    depth, d, n6 = ada_w.shape
    b = c.shape[0]
    tn = 1536
    out = pl.pallas_call(
        _ada_kernel,
        grid=(depth, n6 // tn),
        in_specs=[pl.BlockSpec((b, d), lambda l, j: (0, 0)),
                  pl.BlockSpec((None, d, tn), lambda l, j: (l, 0, j)),
                  pl.BlockSpec((None, 1, tn), lambda l, j: (l, 0, j))],
        out_specs=pl.BlockSpec((None, b, tn), lambda l, j: (l, 0, j)),
        out_shape=jax.ShapeDtypeStruct((depth, b, n6), F32),
        compiler_params=_params(False),
        name="ada",
    )(c, ada_w, ada_b.reshape(depth, 1, n6))
    return out.reshape(depth, b, 6, d)


def _ffn_kernel(x_ref, mod_ref, nw_ref, w1_ref, w3_ref, w2_ref, o_ref):
    x = x_ref[...]
    h = _norm_mod(x, nw_ref[...], mod_ref[4:5, :], mod_ref[3:4, :]).astype(BF16)
    a = jnp.dot(h, w1_ref[...], preferred_element_type=F32)
    b = jnp.dot(h, w3_ref[...], preferred_element_type=F32)
    g = (_silu(a) * b).astype(BF16)
    y = jnp.dot(g, w2_ref[...], preferred_element_type=F32)
    o_ref[...] = x + (1.0 + mod_ref[5:6, :]) * y


def _ffn(x, mod, nw, w1, w3, w2):
    b, s, d = x.shape
    dff = w1.shape[1]
    t = FFN_ROW_TILE if s % FFN_ROW_TILE == 0 else ROW_TILE
    return pl.pallas_call(
        _ffn_kernel,
        grid=(b, s // t),
        in_specs=[_rows(t, d), _mod_spec(d), _resident((1, d)),
                  _resident((d, dff)), _resident((d, dff)), _resident((dff, d))],
        out_specs=_rows(t, d),
        out_shape=jax.ShapeDtypeStruct(x.shape, F32),
        compiler_params=_params(False),
        name="ffn",
    )(x, mod, nw.reshape(1, d), w1.astype(BF16), w3.astype(BF16), w2.astype(BF16))


def _m_in_kernel(x_ref, mod_ref, nw_ref, wz_ref, wx_ref, wd_ref, z_ref, xbc_ref, dt_ref):
    h = _norm_mod(x_ref[...], nw_ref[...], mod_ref[1:2, :], mod_ref[0:1, :]).astype(BF16)
    z_ref[...] = jnp.dot(h, wz_ref[...], preferred_element_type=F32)
    xbc_ref[...] = jnp.dot(h, wx_ref[...], preferred_element_type=F32)
    dt_ref[...] = jnp.dot(h, wd_ref[...], preferred_element_type=F32)


def _m_ssd_kernel(xbc_ref, z_ref, dt_ref, x_ref, mod_ref, cw_ref, cb_ref, dtb_ref, alog_ref, e_ref,
                  dskip_ref, gnw_ref, ow_ref, o_ref, ubuf, st, ybuf, *, nh):
    t = x_ref.shape[0]
    di = z_ref.shape[1]
    ng, ns, gw = st.shape
    hpg = nh // ng
    hd = gw // hpg

    @pl.when(pl.program_id(1) == 0)
    def _():
        ubuf[0:8, :] = jnp.zeros((8, ubuf.shape[1]), F32)
        st[...] = jnp.zeros(st.shape, F32)

    u = xbc_ref[...]
    ubuf[8:8 + t, :] = u
    acc = cb_ref[...] + cw_ref[3:4, :] * u
    for k in range(3):
        acc = acc + cw_ref[k:k + 1, :] * ubuf[5 + k:5 + k + t, :]
    ubuf[0:8, :] = u[t - 8:t, :]
    xc = _silu(acc)
    xs = xc[:, :di]

    lane = lax.broadcasted_iota(jnp.int32, (1, LANE), 1)
    dt = _softplus(dt_ref[...] + dtb_ref[...])
    a_neg = jnp.where(lane < nh, -jnp.exp(alog_ref[...]), 0.0)
    a = dt * a_neg
    tril = _tril(t)
    cs = _mm_sel_l(jnp.where(tril, 1.0, 0.0).astype(BF16), a)
    cs_t = cs.T
    e = e_ref[...]
    dt_e = _mm_sel_r(dt, e, 2)
    cs_e = _mm_sel_r(cs, e, 2)
    cs_last = cs_e[t - 1:t, :]
    xd = xs * dt_e
    ecs = jnp.exp(cs_e)
    xdd = xd * jnp.exp(cs_last - cs_e)
    cdec = jnp.exp(cs_last)
    yskip = xs * dskip_ref[...]
    gate = _silu(z_ref[...])

    rr = lax.broadcasted_iota(jnp.int32, (hpg * t, gw), 0) // t
    ll = lax.broadcasted_iota(jnp.int32, (hpg * t, gw), 1) // hd
    bd = rr == ll
    groups = range(ng)
    sls = [slice(g * gw, (g + 1) * gw) for g in groups]
    bgs = [xc[:, di + g * ns:di + (g + 1) * ns].astype(BF16) for g in groups]
    cgs = [xc[:, di + (ng + g) * ns:di + (ng + g + 1) * ns].astype(BF16) for g in groups]
    scores = [_mm_nt(cgs[g], bgs[g]) for g in groups]
    sts = [st[g] for g in groups]
    yoff = [_mm(cgs[g], sts[g]) for g in groups]
    supd = [_mm_tn(bgs[g], xdd[:, sls[g]]) for g in groups]
    ydiag = []
    for g in groups:
        parts = []
        for j in range(hpg):
            h = g * hpg + j
            diff = cs[:, h:h + 1] - cs_t[h:h + 1, :]
            parts.append((scores[g] * jnp.exp(jnp.where(tril, diff, NEG))).astype(BF16))
        lhs = jnp.concatenate(parts, axis=1)
        rhs = jnp.where(bd, jnp.concatenate([xd[:, sls[g]]] * hpg, axis=0), 0.0).astype(BF16)
        ydiag.append(jnp.dot(lhs, rhs, preferred_element_type=F32))
    for g in groups:
        sl = sls[g]
        st[g] = sts[g] * cdec[:, sl] + supd[g]
        y = (ydiag[g] + yoff[g] * ecs[:, sl] + yskip[:, sl]) * gate[:, sl]
        y = y * lax.rsqrt(jnp.mean(y * y, axis=-1, keepdims=True) + RMS_EPS) * gnw_ref[:, sl]
        ybuf[:, sl] = y.astype(BF16)

    out = jnp.dot(ybuf[...], ow_ref[...], preferred_element_type=F32)
    o_ref[...] = x_ref[...] + (1.0 + mod_ref[2:3, :]) * out


def _mamba(x, mod, nw, in_w, conv_w, conv_b, dt_bias, a_log, d_skip, norm_w, out_w):
    b, s, d = x.shape
    nh = dt_bias.shape[0]
    di = nh * HEAD_DIM
    cdim = conv_w.shape[1]
    ng, ns = SSD_GROUPS, SSD_STATE
    gw = di // ng
    assert cdim == di + 2 * ng * ns and in_w.shape[1] == di + cdim + nh and nh <= LANE
    wz = in_w[:, :di].astype(BF16)
    wx = in_w[:, di:di + cdim].astype(BF16)
    wd = jnp.pad(in_w[:, di + cdim:], ((0, 0), (0, LANE - nh))).astype(BF16)
    t = ROW_TILE
    z, xbc, dtr = pl.pallas_call(
        _m_in_kernel,
        grid=(b, s // t),
        in_specs=[_rows(t, d), _mod_spec(d), _resident((1, d)),
                  _resident((d, di)), _resident((d, cdim)), _resident((d, LANE))],
        out_specs=[_rows(t, di), _rows(t, cdim), _rows(t, LANE)],
        out_shape=[jax.ShapeDtypeStruct((b, s, di), F32), jax.ShapeDtypeStruct((b, s, cdim), F32),
                   jax.ShapeDtypeStruct((b, s, LANE), F32)],
        compiler_params=_params(False),
        name="m_in",
    )(x, mod, nw.reshape(1, d), wz, wx, wd)

    pad1 = lambda v: jnp.pad(v, (0, LANE - nh)).reshape(1, LANE)
    expand = (jnp.arange(LANE)[:, None] == jnp.arange(di)[None, :] // HEAD_DIM).astype(BF16)
    tc = SSD_CHUNK
    return pl.pallas_call(
        functools.partial(_m_ssd_kernel, nh=nh),
        grid=(b, s // tc),
        in_specs=[_rows(tc, cdim), _rows(tc, di), _rows(tc, LANE), _rows(tc, d), _mod_spec(d),
                  _resident((4, cdim)), _resident((1, cdim)), _resident((1, LANE)), _resident((1, LANE)),
                  _resident((LANE, di)), _resident((1, di)), _resident((1, di)), _resident((di, d))],
        out_specs=_rows(tc, d),
        out_shape=jax.ShapeDtypeStruct(x.shape, F32),
        scratch_shapes=[pltpu.VMEM((tc + 8, cdim), F32), pltpu.VMEM((ng, ns, gw), F32),
                        pltpu.VMEM((tc, di), BF16)],
        compiler_params=_params(True),
        name="m_ssd",
    )(xbc, z, dtr, x, mod, conv_w, conv_b.reshape(1, cdim), pad1(dt_bias), pad1(a_log), expand,
      jnp.repeat(d_skip, HEAD_DIM).reshape(1, di), norm_w.reshape(1, di), out_w.astype(BF16))


def _r_in_kernel(x_ref, mod_ref, nw_ref, mix_ref, wr_ref, wk_ref, wv_ref, w1_ref, w2_ref, a1_ref, a2_ref,
                 g1_ref, g2_ref, vec_ref, ind_ref, indt_ref,
                 r_ref, lw_ref, k_ref, v_ref, kk_ref, bb_ref, bonus_ref, g_ref, hbuf, dxbuf):
    t = x_ref.shape[0]

    @pl.when(pl.program_id(1) == 0)
    def _():
        hbuf[0:8, :] = jnp.zeros((8, hbuf.shape[1]), F32)

    hbuf[8:8 + t, :] = _norm_mod(x_ref[...], nw_ref[...], mod_ref[1:2, :], mod_ref[0:1, :])
    dxbuf[...] = hbuf[7:7 + t, :] - hbuf[8:8 + t, :]
    hbuf[0:8, :] = hbuf[t:t + 8, :]
    w0, a0, k_k, k_a, r_k = (vec_ref[i:i + 1, :] for i in range(5))
    ind, indt = ind_ref[...], indt_ref[...]
    headsum = lambda q: _mm_sel_r(_mm_sel_r(q, ind, 2), indt, 2)
    dotf = lambda a, b_ref: jnp.dot(a, b_ref[...], preferred_element_type=F32)
    mixed = lambda n: (hbuf[8:8 + t, :] + dxbuf[...] * mix_ref[n:n + 1, :]).astype(BF16)

    r = dotf(mixed(0), wr_ref)
    k = dotf(mixed(1), wk_ref)
    v = dotf(mixed(2), wv_ref)
    wl = dotf(jnp.tanh(dotf(mixed(3), w1_ref)).astype(BF16), w2_ref)
    al = dotf(dotf(mixed(4), a1_ref).astype(BF16), a2_ref)
    r_ref[...] = r
    v_ref[...] = v
    g_ref[...] = dotf(_sigmoid(dotf(mixed(5), g1_ref)).astype(BF16), g2_ref)
    lw_ref[...] = -jnp.exp(-_softplus(-(w0 + wl)) - 0.5)
    a = _sigmoid(a0 + al)
    kk = k * k_k
    kk = kk * lax.rsqrt(jnp.maximum(headsum(kk * kk), KK_EPS * KK_EPS))
    k2 = k * (1.0 + (a - 1.0) * k_a)
    k_ref[...] = k2
    kk_ref[...] = kk
    bb_ref[...] = kk * a
    bonus_ref[...] = headsum(r * k2 * r_k) * v


def _r_wkv_kernel(r_ref, lw_ref, k_ref, v_ref, kk_ref, bb_ref, bonus_ref, g_ref, x_ref, mod_ref,
                  lnw_ref, lnb_ref, ind_ref, indt_ref, ow_ref, o_ref, state, ybuf, *, hd):
    nb, t, d = x_ref.shape
    gw = state.shape[1]
    ngr = d // gw
    hpg = gw // hd
    n = hpg * t

    @pl.when(pl.program_id(1) == 0)
    def _():
        state[...] = jnp.zeros(state.shape, F32)

    tri = jnp.where(_tril(t), 1.0, 0.0).astype(BF16)
    at, rt, kt, bt, v, ptot = [], [], [], [], [], []
    for bi in range(nb):
        lw = lw_ref[bi]
        cl = _mm_sel_l(tri, lw)
        ecl = jnp.exp(cl)
        encl = jnp.exp(-cl)
        rt.append(r_ref[bi] * ecl)
        at.append(-kk_ref[bi] * jnp.exp(cl - lw))
        kt.append(k_ref[bi] * encl)
        bt.append(bb_ref[bi] * encl)
        v.append(v_ref[bi])
        ptot.append(ecl[t - 1:t, :])

    ri = lax.broadcasted_iota(jnp.int32, (n, gw), 0)
    ci = lax.broadcasted_iota(jnp.int32, (n, gw), 1)
    headm = (ri // t) == (ci // hd)
    ri2 = lax.broadcasted_iota(jnp.int32, (n, n), 0)
    ci2 = lax.broadcasted_iota(jnp.int32, (n, n), 1)
    same = (ri2 // t) == (ci2 // t)
    tt = lax.broadcasted_iota(jnp.int32, (t, n), 0)
    ss = lax.broadcasted_iota(jnp.int32, (t, n), 1) % t
    strict = tt > ss
    incl = tt >= ss
    eye = jnp.where(tt == ss, 1.0, 0.0)
    vals_same = (lax.broadcasted_iota(jnp.int32, (gw, gw), 0) // hd) == (
        lax.broadcasted_iota(jnp.int32, (gw, gw), 1) // hd)

    def blockdiag(m, mask):
        return jnp.where(mask, jnp.concatenate([m] * hpg, axis=0), 0.0).astype(BF16)

    groups = range(nb * ngr)
    bis = [g // ngr for g in groups]
    sls = [slice((g % ngr) * gw, (g % ngr + 1) * gw) for g in groups]
    dot = lambda a, b: jnp.dot(a.astype(BF16), b, preferred_element_type=F32)
    ar = [jnp.concatenate([at[bi][:, sl], rt[bi][:, sl]], axis=0) for bi, sl in zip(bis, sls)]
    vst = [blockdiag(v[bi][:, sl], headm) for bi, sl in zip(bis, sls)]
    xbk = [jnp.concatenate([blockdiag(bt[bi][:, sl], headm), blockdiag(kt[bi][:, sl], headm)], axis=0)
           for bi, sl in zip(bis, sls)]
    gram = [_mm_nt(ar[g], xbk[g]) for g in groups]
    hs = [state[g] for g in groups]
    hproj = [_mm_nt(ar[g], hs[g]) for g in groups]
    a_ab = [jnp.where(strict, gram[g][0:t, 0:n], 0.0) for g in groups]
    rhs = [hproj[g][0:t] + dot(jnp.where(strict, gram[g][0:t, n:2 * n], 0.0), vst[g]) for g in groups]
    tinv = [eye + a_ab[g] for g in groups]
    p = [dot(a_ab[g], blockdiag(a_ab[g], same)) for g in groups]
    steps = max(t - 1, 1).bit_length() - 1
    for i in range(steps):
        last = i + 1 == steps
        for g in groups:
            tb = blockdiag(tinv[g], same)
            prod = dot(p[g], tb if last else jnp.concatenate([tb, blockdiag(p[g], same)], axis=1))
            tinv[g] = tinv[g] + prod[:, 0:n]
            if not last:
                p[g] = prod[:, n:2 * n]
    u = [dot(tinv[g], blockdiag(rhs[g], headm)) for g in groups]
    for g in groups:
        r_bk = jnp.where(jnp.concatenate([incl, incl], axis=1), gram[g][t:2 * t, :], 0.0)
        ybuf[bis[g] * t:(bis[g] + 1) * t, sls[g]] = hproj[g][t:2 * t] + dot(
            r_bk, jnp.concatenate([blockdiag(u[g], headm), vst[g]], axis=0))
    for g in groups:
        bi, sl = bis[g], sls[g]
        upd = _mm_tn(jnp.concatenate([u[g], v[bi][:, sl]], axis=0),
                     jnp.concatenate([bt[bi][:, sl], kt[bi][:, sl]], axis=0))
        state[g] = (hs[g] + jnp.where(vals_same, upd, 0.0)) * ptot[bi][:, sl]

    y = ybuf[...]
    ind, indt = ind_ref[...], indt_ref[...]
    headmean = lambda q: _mm_sel_r(_mm_sel_r(q, ind, 2), indt, 2) * (1.0 / hd)
    mu = headmean(y)
    yc = y - mu
    var = headmean(yc * yc)
    yn = yc * lax.rsqrt(var + GN_EPS) * lnw_ref[...] + lnb_ref[...]
    yo = (yn + bonus_ref[...].reshape(nb * t, d)) * g_ref[...].reshape(nb * t, d)
    out = _mm(yo, ow_ref[...])
    for bi in range(nb):
        o_ref[bi] = x_ref[bi] + (1.0 + mod_ref[bi, 2:3, :]) * out[bi * t:(bi + 1) * t]


def _rwkv(x, mod, nw, mix, w_rkv, w0, w1, w2, a0, a1, a2, g1, g2, k_k, k_a, r_k, lnx_w, lnx_b, out_w):
    b, s, d = x.shape
    hd = HEAD_DIM
    assert d % WKV_GROUP == 0 and d // hd <= LANE

    def pad_pair(p1, p2):
        r = p1.shape[1]
        rp = -(-r // LANE) * LANE
        return (jnp.pad(p1, ((0, 0), (0, rp - r))).astype(BF16), jnp.pad(p2, ((0, rp - r), (0, 0))).astype(BF16))

    w1p, w2p = pad_pair(w1, w2)
    a1p, a2p = pad_pair(a1, a2)
    g1p, g2p = pad_pair(g1, g2)
    vecs = jnp.stack([w0, a0, k_k, k_a, r_k.reshape(d), jnp.zeros_like(w0), jnp.zeros_like(w0), jnp.zeros_like(w0)])
    ind, indt = _head_indicator(d, hd)
    wb = w_rkv.astype(BF16)
    t = ROW_TILE
    res = lambda arr: _resident(arr.shape)
    outs = pl.pallas_call(
        _r_in_kernel,
        grid=(b, s // t),
        in_specs=[_rows(t, d), _mod_spec(d), _resident((1, d)), _resident((6, d)),
                  _resident((d, d)), _resident((d, d)), _resident((d, d)),
                  res(w1p), res(w2p), res(a1p), res(a2p), res(g1p), res(g2p),
                  _resident((8, d)), res(ind), res(indt)],
        out_specs=[_rows(t, d)] * 8,
        out_shape=[jax.ShapeDtypeStruct((b, s, d), F32)] * 8,
        scratch_shapes=[pltpu.VMEM((t + 8, d), F32), pltpu.VMEM((t, d), F32)],
        compiler_params=_params(True),
        name="r_in",
    )(x, mod, nw.reshape(1, d), mix, wb[0], wb[1], wb[2], w1p, w2p, a1p, a2p, g1p, g2p, vecs, ind, indt)

    tc = WKV_CHUNK
    nb = WKV_BATCH_ROWS if b % WKV_BATCH_ROWS == 0 else 1
    rows = pl.BlockSpec((nb, tc, d), lambda bi, si: (bi, si, 0))
    return pl.pallas_call(
        functools.partial(_r_wkv_kernel, hd=hd),
        grid=(b // nb, s // tc),
        in_specs=[rows] * 9 + [pl.BlockSpec((nb, 6, d), lambda bi, si: (bi, 0, 0)), _resident((1, d)),
                               _resident((1, d)), res(ind), res(indt), _resident((d, d))],
        out_specs=rows,
        out_shape=jax.ShapeDtypeStruct(x.shape, F32),
        scratch_shapes=[pltpu.VMEM((nb * d // WKV_GROUP, WKV_GROUP, WKV_GROUP), F32),
                        pltpu.VMEM((nb * tc, d), F32)],
        compiler_params=_params(True),
        name="r_wkv",
    )(*outs, x, mod, lnx_w.reshape(1, d), lnx_b.reshape(1, d), ind, indt, out_w.astype(BF16))


def _f_in_kernel(x_ref, mod_ref, nw_ref, wq_ref, wk_ref, wvt_ref, wf_ref, vec_ref, fb_ref, ind_ref, indt_ref,
                 place_ref, ones_ref, q_ref, k_ref, vt_ref, qa_ref, ka_ref, carry, *, hd):
    t = x_ref.shape[0]

    @pl.when(pl.program_id(1) == 0)
    def _():
        carry[...] = jnp.zeros(carry.shape, F32)

    h = _norm_mod(x_ref[...], nw_ref[...], mod_ref[1:2, :], mod_ref[0:1, :]).astype(BF16)
    ind, indt = ind_ref[...], indt_ref[...]

    def head_rms(q, w):
        ms = _mm_sel_r(_mm_sel_r(q * q, ind, 2), indt, 2) * (1.0 / hd)
        return q * lax.rsqrt(ms + RMS_EPS) * w

    q = head_rms(jnp.dot(h, wq_ref[...], preferred_element_type=F32), vec_ref[0:1, :])
    k = head_rms(jnp.dot(h, wk_ref[...], preferred_element_type=F32), vec_ref[1:2, :])
    q_ref[...] = (q * (hd ** -0.5 * LOG2E)).astype(BF16)
    k_ref[...] = k.astype(BF16)
    vt_ref[...] = _mm_nt(wvt_ref[...], h).astype(BF16)
    f = jnp.dot(h, wf_ref[...], preferred_element_type=F32) + fb_ref[...]
    logf = -_softplus(-f)
    cum = _mm_sel_l(jnp.where(_tril(t), 1.0, 0.0).astype(BF16), logf) + carry[0:1, :]
    carry[...] = jnp.broadcast_to(cum[t - 1:t, :], carry.shape)
    pieces = _split(cum * LOG2E, 3)
    place = lambda first: sum(jnp.dot(p, place_ref[first + i], preferred_element_type=F32)
                              for i, p in enumerate(pieces))
    qa_ref[...] = (place(0) + ones_ref[0:1, :]).astype(BF16)
    ka_ref[...] = (ones_ref[1:2, :] - place(3)).astype(BF16)


def _f_att_kernel(q_ref, k_ref, vt_ref, qa_ref, ka_ref, x_ref, mod_ref, ow_ref, o_ref,
                  obuf, qm_scr, m_scr, acc_scr, *, hd, pairs_per_loop):
    tq = x_ref.shape[0]
    d = x_ref.shape[1]
    tk = vt_ref.shape[2]
    qi = pl.program_id(1)
    lane_l = lax.broadcasted_iota(jnp.int32, (1, LANE), 1)
    first_l = lane_l < hd
    first_r = lax.broadcasted_iota(jnp.int32, (LANE, 1), 0) < hd
    on_or_below = (lax.broadcasted_iota(jnp.int32, (tk, tq), 0)
                   <= lax.broadcasted_iota(jnp.int32, (tk, tq), 1))
    zero_b = jnp.zeros((), BF16)
    one_b = jnp.ones((), BF16)

    lanes = lambda p: slice(p * LANE, (p + 1) * LANE)
    for p0 in range(0, d // LANE, pairs_per_loop):
        pairs = range(p0, p0 + pairs_per_loop)
        heads = [2 * p + i for p in pairs for i in range(2)]
        qa = qa_ref[...]
        for p in pairs:
            qp = q_ref[:, lanes(p)]
            for i, qh in enumerate((jnp.where(first_l, qp, zero_b), jnp.where(first_l, zero_b, qp))):
                h = 2 * p + i
                own = (lane_l >= BIAS_COLS * h) & (lane_l < BIAS_COLS * (h + 1))
                qm_scr[h] = jnp.concatenate([qh, jnp.where(own, qa, zero_b)], axis=1)
        for h in heads:
            m_scr[h] = jnp.full((1, tq), NEG, F32)
            acc_scr[h] = jnp.zeros((LANE, tq), F32)

        def step(j, diagonal, pairs=pairs, heads=heads):
            start = pl.multiple_of(j * tk, tk)
            sc = {}
            kaj = ka_ref[pl.ds(start, tk), :]
            for p in pairs:
                kj = jnp.concatenate([k_ref[pl.ds(start, tk), lanes(p)], kaj], axis=1)
                for h in (2 * p, 2 * p + 1):
                    sc[h] = lax.dot_general(kj, qm_scr[h], (((1,), (1,)), ((), ())), preferred_element_type=F32)
            if diagonal:
                sc = {h: jnp.where(on_or_below, sc[h], NEG) for h in heads}
            m_old = {h: m_scr[h] for h in heads}
            m_new = {h: jnp.maximum(m_old[h], jnp.max(sc[h], axis=0, keepdims=True)) for h in heads}
            pr = {h: jnp.exp2(sc[h] - m_new[h]).astype(BF16) for h in heads}
            for p in pairs:
                vtj = vt_ref[j, lanes(p), :]
                vt2 = (jnp.where(first_r, vtj, one_b), jnp.where(first_r, one_b, vtj))
                for i in range(2):
                    h = 2 * p + i
                    acc_scr[h] = (jnp.exp2(m_old[h] - m_new[h]) * acc_scr[h]
                                  + jnp.dot(vt2[i], pr[h], preferred_element_type=F32))
                    m_scr[h] = m_new[h]

        def body(j, c):
            step(j, False)
            return c

        lax.fori_loop(0, qi, body, 0)
        step(qi, True)
        for p in pairs:
            acc0, acc1 = acc_scr[2 * p], acc_scr[2 * p + 1]
            o_t = jnp.where(first_r, acc0 / acc0[hd:hd + 1, :], acc1 / acc1[0:1, :])
            obuf[:, lanes(p)] = o_t.T.astype(BF16)

    out = jnp.dot(obuf[...], ow_ref[...], preferred_element_type=F32)
    o_ref[...] = x_ref[...] + (1.0 + mod_ref[2:3, :]) * out


def _fox(x, mod, nw, qkvf_w, fgate_b, q_norm_w, k_norm_w, out_w):
    b, s, d = x.shape
    hd = HEAD_DIM
    nh = d // hd
    assert qkvf_w.shape[1] == 3 * d + nh and nh <= LANE and 2 * hd == LANE
    wq, wk, wv = (qkvf_w[:, i * d:(i + 1) * d].astype(BF16) for i in range(3))
    wf = jnp.pad(qkvf_w[:, 3 * d:], ((0, 0), (0, LANE - nh))).astype(BF16)
    fb = jnp.pad(fgate_b, (0, LANE - nh)).reshape(1, LANE)
    z = jnp.zeros((d,), F32)
    vecs = jnp.stack([jnp.tile(q_norm_w, nh), jnp.tile(k_norm_w, nh), z, z, z, z, z, z])
    ind, indt = _head_indicator(d, hd)
    t = ROW_TILE
    res = lambda arr: _resident(arr.shape)
    nt = s // t
    assert BIAS_COLS * nh <= LANE
    col = jnp.arange(LANE)[None, None, :]
    place = (col == BIAS_COLS * jnp.arange(LANE)[None, :, None] + jnp.arange(BIAS_COLS)[:, None, None])
    place = (place & (jnp.arange(LANE)[None, :, None] < nh)).astype(BF16)
    third = (jnp.arange(LANE) % BIAS_COLS) // 3
    used = jnp.arange(LANE) < BIAS_COLS * nh
    z1 = jnp.zeros((LANE,), F32)
    ones = jnp.stack([(used & (third == 1)).astype(F32), (used & (third == 0)).astype(F32), z1, z1, z1, z1, z1, z1])
    q, k, vt, qa, ka = pl.pallas_call(
        functools.partial(_f_in_kernel, hd=hd),
        grid=(b, nt),
        in_specs=[_rows(t, d), _mod_spec(d), _resident((1, d)), _resident((d, d)), _resident((d, d)),
                  _resident((d, d)), _resident((d, LANE)), _resident((8, d)), _resident((1, LANE)),
                  res(ind), res(indt), res(place), res(ones)],
        out_specs=[_rows(t, d), _rows(t, d), pl.BlockSpec((None, None, d, t), lambda bi, si: (bi, si, 0, 0)),
                   _rows(t, LANE), _rows(t, LANE)],
        out_shape=[jax.ShapeDtypeStruct((b, s, d), BF16), jax.ShapeDtypeStruct((b, s, d), BF16),
                   jax.ShapeDtypeStruct((b, nt, d, t), BF16), jax.ShapeDtypeStruct((b, s, LANE), BF16),
                   jax.ShapeDtypeStruct((b, s, LANE), BF16)],
        scratch_shapes=[pltpu.VMEM((8, LANE), F32)],
        compiler_params=_params(True),
        name="f_in",
    )(x, mod, nw.reshape(1, d), wq, wk, wv.T, wf, vecs, fb, ind, indt, place, ones)

    whole = lambda *shape: pl.BlockSpec((None,) + shape, lambda bi, si: (bi,) + (0,) * len(shape))
    return pl.pallas_call(
        functools.partial(_f_att_kernel, hd=hd, pairs_per_loop=ATT_PAIRS_PER_LOOP),
        grid=(b, nt),
        in_specs=[_rows(t, d), whole(s, d), whole(nt, d, t), _rows(t, LANE), whole(s, LANE),
                  _rows(t, d), _mod_spec(d), _resident((d, d))],
        out_specs=_rows(t, d),
        out_shape=jax.ShapeDtypeStruct(x.shape, F32),
        scratch_shapes=[pltpu.VMEM((t, d), BF16), pltpu.VMEM((nh, t, 2 * LANE), BF16),
                        pltpu.VMEM((nh, 1, t), F32), pltpu.VMEM((nh, LANE, t), F32)],
        compiler_params=_params(False),
        name="f_att",
    )(q, k, vt, qa, ka, x, mod, out_w.astype(BF16))


def kernel(x, c, ada_w, ada_b, norm1_w, norm2_w, ffn_w1, ffn_w3, ffn_w2, m_in_w, m_conv_w, m_conv_b, m_dt_bias, m_A_log, m_D, m_norm_w, m_out_w, r_mix, r_w_rkv, r_w0, r_w1, r_w2, r_a0, r_a1, r_a2, r_g1, r_g2, r_k_k, r_k_a, r_r_k, r_lnx_w, r_lnx_b, r_out_w, f_qkvf_w, f_fgate_b, f_q_norm_w, f_k_norm_w, f_out_w):
    depth = ada_w.shape[0]
    mod = _ada(c, ada_w, ada_b)
    ia = ib = ic = 0
    for i in range(depth):
        kind = i % 3
        if kind == 0:
            x = _mamba(x, mod[i], norm1_w[i], m_in_w[ia], m_conv_w[ia], m_conv_b[ia], m_dt_bias[ia],
                       m_A_log[ia], m_D[ia], m_norm_w[ia], m_out_w[ia])
            ia += 1
        elif kind == 1:
            x = _rwkv(x, mod[i], norm1_w[i], r_mix[ib], r_w_rkv[ib], r_w0[ib], r_w1[ib], r_w2[ib], r_a0[ib],
                      r_a1[ib], r_a2[ib], r_g1[ib], r_g2[ib], r_k_k[ib], r_k_a[ib], r_r_k[ib],
                      r_lnx_w[ib], r_lnx_b[ib], r_out_w[ib])
            ib += 1
        else:
            x = _fox(x, mod[i], norm1_w[i], f_qkvf_w[ic], f_fgate_b[ic], f_q_norm_w[ic], f_k_norm_w[ic],
                     f_out_w[ic])
            ic += 1
        x = _ffn(x, mod[i], norm2_w[i], ffn_w1[i], ffn_w3[i], ffn_w2[i])
    return x
```

```python
import functools

import jax
import jax.numpy as jnp
from jax import lax
from jax.experimental import pallas as pl
from jax.experimental.pallas import tpu as pltpu

F32 = jnp.float32
BF16 = jnp.bfloat16

RMS_EPS = 1e-6
GN_EPS = 64e-5
KK_EPS = 1e-12
NEG = -1e30
LOG2E = 1.4426950408889634

V7X_VMEM_LIMIT = 56 * 1024 * 1024

HEAD_DIM = 64
SSD_GROUPS = 8
SSD_STATE = 128
SSD_CHUNK = 128
WKV_CHUNK = 64
WKV_GROUP = 128
WKV_BATCH_ROWS = 2
ROW_TILE = 256
FFN_ROW_TILE = 512
ATT_PAIRS_PER_LOOP = 8
BIAS_COLS = 6
LANE = 128


def _mm(a, b):
    return jnp.dot(a.astype(BF16), b.astype(BF16), preferred_element_type=F32)


def _mm_nt(a, b):
    return lax.dot_general(a.astype(BF16), b.astype(BF16), (((1,), (1,)), ((), ())),
                           preferred_element_type=F32)


def _mm_tn(a, b):
    return jnp.dot(a.T.astype(BF16), b.astype(BF16), preferred_element_type=F32)


def _split(x, terms):
    parts = []
    for i in range(terms):
        p = x.astype(BF16)
        parts.append(p)
        if i + 1 < terms:
            x = x - p.astype(F32)
    return parts


def _mm_sel_l(sel, x, terms=3):
    return sum(jnp.dot(sel, p, preferred_element_type=F32) for p in _split(x, terms))


def _mm_sel_r(x, sel, terms=3):
    return sum(jnp.dot(p, sel, preferred_element_type=F32) for p in _split(x, terms))


def _mm_hi(a, b):
    ah = a.astype(BF16)
    al = (a - ah.astype(F32)).astype(BF16)
    bh = b.astype(BF16)
    bl = (b - bh.astype(F32)).astype(BF16)
    d = lambda p, q: jnp.dot(p, q, preferred_element_type=F32)
    return d(ah, bh) + d(ah, bl) + d(al, bh)


def _sigmoid(x):
    return jax.nn.sigmoid(x)


def _silu(x):
    hx = 0.5 * x
    return hx + hx * jnp.tanh(hx)


def _softplus(x):
    return jnp.maximum(x, 0.0) + jnp.log(1.0 + jnp.exp(-jnp.abs(x)))


def _norm_mod(x, nw, scale, shift):
    y = x * lax.rsqrt(jnp.mean(x * x, axis=-1, keepdims=True) + RMS_EPS)
    return (y * nw) * (1.0 + scale) + shift


def _tril(n, strict=False):
    r = lax.broadcasted_iota(jnp.int32, (n, n), 0)
    c = lax.broadcasted_iota(jnp.int32, (n, n), 1)
    return (r > c) if strict else (r >= c)


def _resident(shape):
    nd = len(shape)
    return pl.BlockSpec(shape, lambda *_: (0,) * nd, pipeline_mode=pl.Buffered(1))


def _rows(t, width):
    return pl.BlockSpec((None, t, width), lambda b, s: (b, s, 0))


def _mod_spec(d):
    return pl.BlockSpec((None, 6, d), lambda b, s: (b, 0, 0))


def _params(seq_axis_carries):
    sem = ("parallel", "arbitrary") if seq_axis_carries else ("parallel", "parallel")
    return pltpu.CompilerParams(dimension_semantics=sem, vmem_limit_bytes=V7X_VMEM_LIMIT)


def _head_indicator(d, hd):
    ind = (jnp.arange(d)[:, None] // hd == jnp.arange(LANE)[None, :]).astype(BF16)
    return ind, ind.T


def _ada_kernel(c_ref, w_ref, b_ref, o_ref):
    c = c_ref[...]
    o_ref[...] = _mm_hi(_silu(c), w_ref[...]) + b_ref[...]


def _ada(c, ada_w, ada_b):
    depth, d, n6 = ada_w.shape
    b = c.shape[0]
    tn = 1536
    out = pl.pallas_call(
        _ada_kernel,
        grid=(depth, n6 // tn),
        in_specs=[pl.BlockSpec((b, d), lambda l, j: (0, 0)),
                  pl.BlockSpec((None, d, tn), lambda l, j: (l, 0, j)),
                  pl.BlockSpec((None, 1, tn), lambda l, j: (l, 0, j))],
        out_specs=pl.BlockSpec((None, b, tn), lambda l, j: (l, 0, j)),
        out_shape=jax.ShapeDtypeStruct((depth, b, n6), F32),
        compiler_params=_params(False),
        name="ada",
    )(c, ada_w, ada_b.reshape(depth, 1, n6))
    return out.reshape(depth, b, 6, d)


def _ffn_kernel(x_ref, mod_ref, nw_ref, w1_ref, w3_ref, w2_ref, o_ref):
    x = x_ref[...]
    h = _norm_mod(x, nw_ref[...], mod_ref[4:5, :], mod_ref[3:4, :]).astype(BF16)
    a = jnp.dot(h, w1_ref[...], preferred_element_type=F32)
    b = jnp.dot(h, w3_ref[...], preferred_element_type=F32)
    g = (_silu(a) * b).astype(BF16)
    y = jnp.dot(g, w2_ref[...], preferred_element_type=F32)
    o_ref[...] = x + (1.0 + mod_ref[5:6, :]) * y


def _ffn(x, mod, nw, w1, w3, w2):
    b, s, d = x.shape
    dff = w1.shape[1]
    t = FFN_ROW_TILE if s % FFN_ROW_TILE == 0 else ROW_TILE
    return pl.pallas_call(
        _ffn_kernel,
        grid=(b, s // t),
        in_specs=[_rows(t, d), _mod_spec(d), _resident((1, d)),
                  _resident((d, dff)), _resident((d, dff)), _resident((dff, d))],
        out_specs=_rows(t, d),
        out_shape=jax.ShapeDtypeStruct(x.shape, F32),
        compiler_params=_params(False),
        name="ffn",
    )(x, mod, nw.reshape(1, d), w1.astype(BF16), w3.astype(BF16), w2.astype(BF16))


def _m_in_kernel(x_ref, mod_ref, nw_ref, wz_ref, wx_ref, wd_ref, z_ref, xbc_ref, dt_ref):
    h = _norm_mod(x_ref[...], nw_ref[...], mod_ref[1:2, :], mod_ref[0:1, :]).astype(BF16)
    z_ref[...] = jnp.dot(h, wz_ref[...], preferred_element_type=F32)
    xbc_ref[...] = jnp.dot(h, wx_ref[...], preferred_element_type=F32)
    dt_ref[...] = jnp.dot(h, wd_ref[...], preferred_element_type=F32)


def _m_ssd_kernel(xbc_ref, z_ref, dt_ref, x_ref, mod_ref, cw_ref, cb_ref, dtb_ref, alog_ref, e_ref,
                  dskip_ref, gnw_ref, ow_ref, o_ref, ubuf, st, ybuf, *, nh):
    t = x_ref.shape[0]
    di = z_ref.shape[1]
    ng, ns, gw = st.shape
    hpg = nh // ng
    hd = gw // hpg

    @pl.when(pl.program_id(1) == 0)
    def _():
        ubuf[0:8, :] = jnp.zeros((8, ubuf.shape[1]), F32)
        st[...] = jnp.zeros(st.shape, F32)

    u = xbc_ref[...]
    ubuf[8:8 + t, :] = u
    acc = cb_ref[...] + cw_ref[3:4, :] * u
    for k in range(3):
        acc = acc + cw_ref[k:k + 1, :] * ubuf[5 + k:5 + k + t, :]
    ubuf[0:8, :] = u[t - 8:t, :]
    xc = _silu(acc)
    xs = xc[:, :di]

    lane = lax.broadcasted_iota(jnp.int32, (1, LANE), 1)
    dt = _softplus(dt_ref[...] + dtb_ref[...])
    a_neg = jnp.where(lane < nh, -jnp.exp(alog_ref[...]), 0.0)
    a = dt * a_neg
    tril = _tril(t)
    cs = _mm_sel_l(jnp.where(tril, 1.0, 0.0).astype(BF16), a)
    cs_t = cs.T
    e = e_ref[...]
    dt_e = _mm_sel_r(dt, e, 2)
    cs_e = _mm_sel_r(cs, e, 2)
    cs_last = cs_e[t - 1:t, :]
    xd = xs * dt_e
    ecs = jnp.exp(cs_e)
    xdd = xd * jnp.exp(cs_last - cs_e)
    cdec = jnp.exp(cs_last)
    yskip = xs * dskip_ref[...]
    gate = _silu(z_ref[...])

    rr = lax.broadcasted_iota(jnp.int32, (hpg * t, gw), 0) // t
    ll = lax.broadcasted_iota(jnp.int32, (hpg * t, gw), 1) // hd
    bd = rr == ll
    groups = range(ng)
    sls = [slice(g * gw, (g + 1) * gw) for g in groups]
    bgs = [xc[:, di + g * ns:di + (g + 1) * ns].astype(BF16) for g in groups]
    cgs = [xc[:, di + (ng + g) * ns:di + (ng + g + 1) * ns].astype(BF16) for g in groups]
    scores = [_mm_nt(cgs[g], bgs[g]) for g in groups]
    sts = [st[g] for g in groups]
    yoff = [_mm(cgs[g], sts[g]) for g in groups]
    supd = [_mm_tn(bgs[g], xdd[:, sls[g]]) for g in groups]
    ydiag = []
    for g in groups:
        parts = []
        for j in range(hpg):
            h = g * hpg + j
            diff = cs[:, h:h + 1] - cs_t[h:h + 1, :]
            parts.append((scores[g] * jnp.exp(jnp.where(tril, diff, NEG))).astype(BF16))
        lhs = jnp.concatenate(parts, axis=1)
        rhs = jnp.where(bd, jnp.concatenate([xd[:, sls[g]]] * hpg, axis=0), 0.0).astype(BF16)
        ydiag.append(jnp.dot(lhs, rhs, preferred_element_type=F32))
    for g in groups:
        sl = sls[g]
        st[g] = sts[g] * cdec[:, sl] + supd[g]
        y = (ydiag[g] + yoff[g] * ecs[:, sl] + yskip[:, sl]) * gate[:, sl]
        y = y * lax.rsqrt(jnp.mean(y * y, axis=-1, keepdims=True) + RMS_EPS) * gnw_ref[:, sl]
        ybuf[:, sl] = y.astype(BF16)

    out = jnp.dot(ybuf[...], ow_ref[...], preferred_element_type=F32)
    o_ref[...] = x_ref[...] + (1.0 + mod_ref[2:3, :]) * out


def _mamba(x, mod, nw, in_w, conv_w, conv_b, dt_bias, a_log, d_skip, norm_w, out_w):
    b, s, d = x.shape
    nh = dt_bias.shape[0]
    di = nh * HEAD_DIM
    cdim = conv_w.shape[1]
    ng, ns = SSD_GROUPS, SSD_STATE
    gw = di // ng
    assert cdim == di + 2 * ng * ns and in_w.shape[1] == di + cdim + nh and nh <= LANE
    wz = in_w[:, :di].astype(BF16)
    wx = in_w[:, di:di + cdim].astype(BF16)
    wd = jnp.pad(in_w[:, di + cdim:], ((0, 0), (0, LANE - nh))).astype(BF16)
    t = ROW_TILE
    z, xbc, dtr = pl.pallas_call(
        _m_in_kernel,
        grid=(b, s // t),
        in_specs=[_rows(t, d), _mod_spec(d), _resident((1, d)),
                  _resident((d, di)), _resident((d, cdim)), _resident((d, LANE))],
        out_specs=[_rows(t, di), _rows(t, cdim), _rows(t, LANE)],
        out_shape=[jax.ShapeDtypeStruct((b, s, di), F32), jax.ShapeDtypeStruct((b, s, cdim), F32),
                   jax.ShapeDtypeStruct((b, s, LANE), F32)],
        compiler_params=_params(False),
        name="m_in",
    )(x, mod, nw.reshape(1, d), wz, wx, wd)

    pad1 = lambda v: jnp.pad(v, (0, LANE - nh)).reshape(1, LANE)
    expand = (jnp.arange(LANE)[:, None] == jnp.arange(di)[None, :] // HEAD_DIM).astype(BF16)
    tc = SSD_CHUNK
    return pl.pallas_call(
        functools.partial(_m_ssd_kernel, nh=nh),
        grid=(b, s // tc),
        in_specs=[_rows(tc, cdim), _rows(tc, di), _rows(tc, LANE), _rows(tc, d), _mod_spec(d),
                  _resident((4, cdim)), _resident((1, cdim)), _resident((1, LANE)), _resident((1, LANE)),
                  _resident((LANE, di)), _resident((1, di)), _resident((1, di)), _resident((di, d))],
        out_specs=_rows(tc, d),
        out_shape=jax.ShapeDtypeStruct(x.shape, F32),
        scratch_shapes=[pltpu.VMEM((tc + 8, cdim), F32), pltpu.VMEM((ng, ns, gw), F32),
                        pltpu.VMEM((tc, di), BF16)],
        compiler_params=_params(True),
        name="m_ssd",
    )(xbc, z, dtr, x, mod, conv_w, conv_b.reshape(1, cdim), pad1(dt_bias), pad1(a_log), expand,
      jnp.repeat(d_skip, HEAD_DIM).reshape(1, di), norm_w.reshape(1, di), out_w.astype(BF16))


def _r_in_kernel(x_ref, mod_ref, nw_ref, mix_ref, wr_ref, wk_ref, wv_ref, w1_ref, w2_ref, a1_ref, a2_ref,
                 g1_ref, g2_ref, vec_ref, ind_ref, indt_ref,
                 r_ref, lw_ref, k_ref, v_ref, kk_ref, bb_ref, bonus_ref, g_ref, hbuf, dxbuf):
    t = x_ref.shape[0]

    @pl.when(pl.program_id(1) == 0)
    def _():
        hbuf[0:8, :] = jnp.zeros((8, hbuf.shape[1]), F32)

    hbuf[8:8 + t, :] = _norm_mod(x_ref[...], nw_ref[...], mod_ref[1:2, :], mod_ref[0:1, :])
    dxbuf[...] = hbuf[7:7 + t, :] - hbuf[8:8 + t, :]
    hbuf[0:8, :] = hbuf[t:t + 8, :]
    w0, a0, k_k, k_a, r_k = (vec_ref[i:i + 1, :] for i in range(5))
    ind, indt = ind_ref[...], indt_ref[...]
    headsum = lambda q: _mm_sel_r(_mm_sel_r(q, ind, 2), indt, 2)
    dotf = lambda a, b_ref: jnp.dot(a, b_ref[...], preferred_element_type=F32)
    mixed = lambda n: (hbuf[8:8 + t, :] + dxbuf[...] * mix_ref[n:n + 1, :]).astype(BF16)

    r = dotf(mixed(0), wr_ref)
    k = dotf(mixed(1), wk_ref)
    v = dotf(mixed(2), wv_ref)
    wl = dotf(jnp.tanh(dotf(mixed(3), w1_ref)).astype(BF16), w2_ref)
    al = dotf(dotf(mixed(4), a1_ref).astype(BF16), a2_ref)
    r_ref[...] = r
    v_ref[...] = v
    g_ref[...] = dotf(_sigmoid(dotf(mixed(5), g1_ref)).astype(BF16), g2_ref)
    lw_ref[...] = -jnp.exp(-_softplus(-(w0 + wl)) - 0.5)
    a = _sigmoid(a0 + al)
    kk = k * k_k
    kk = kk * lax.rsqrt(jnp.maximum(headsum(kk * kk), KK_EPS * KK_EPS))
    k2 = k * (1.0 + (a - 1.0) * k_a)
    k_ref[...] = k2
    kk_ref[...] = kk
    bb_ref[...] = kk * a
    bonus_ref[...] = headsum(r * k2 * r_k) * v


def _r_wkv_kernel(r_ref, lw_ref, k_ref, v_ref, kk_ref, bb_ref, bonus_ref, g_ref, x_ref, mod_ref,
                  lnw_ref, lnb_ref, ind_ref, indt_ref, ow_ref, o_ref, state, ybuf, *, hd):
    nb, t, d = x_ref.shape
    gw = state.shape[1]
    ngr = d // gw
    hpg = gw // hd
    n = hpg * t

    @pl.when(pl.program_id(1) == 0)
    def _():
        state[...] = jnp.zeros(state.shape, F32)

    tri = jnp.where(_tril(t), 1.0, 0.0).astype(BF16)
    at, rt, kt, bt, v, ptot = [], [], [], [], [], []
    for bi in range(nb):
        lw = lw_ref[bi]
        cl = _mm_sel_l(tri, lw)
        ecl = jnp.exp(cl)
        encl = jnp.exp(-cl)
        rt.append(r_ref[bi] * ecl)
        at.append(-kk_ref[bi] * jnp.exp(cl - lw))
        kt.append(k_ref[bi] * encl)
        bt.append(bb_ref[bi] * encl)
        v.append(v_ref[bi])
        ptot.append(ecl[t - 1:t, :])

    ri = lax.broadcasted_iota(jnp.int32, (n, gw), 0)
    ci = lax.broadcasted_iota(jnp.int32, (n, gw), 1)
    headm = (ri // t) == (ci // hd)
    ri2 = lax.broadcasted_iota(jnp.int32, (n, n), 0)
    ci2 = lax.broadcasted_iota(jnp.int32, (n, n), 1)
    same = (ri2 // t) == (ci2 // t)
    tt = lax.broadcasted_iota(jnp.int32, (t, n), 0)
    ss = lax.broadcasted_iota(jnp.int32, (t, n), 1) % t
    strict = tt > ss
    incl = tt >= ss
    eye = jnp.where(tt == ss, 1.0, 0.0)
    vals_same = (lax.broadcasted_iota(jnp.int32, (gw, gw), 0) // hd) == (
        lax.broadcasted_iota(jnp.int32, (gw, gw), 1) // hd)

    def blockdiag(m, mask):
        return jnp.where(mask, jnp.concatenate([m] * hpg, axis=0), 0.0).astype(BF16)

    groups = range(nb * ngr)
    bis = [g // ngr for g in groups]
    sls = [slice((g % ngr) * gw, (g % ngr + 1) * gw) for g in groups]
    dot = lambda a, b: jnp.dot(a.astype(BF16), b, preferred_element_type=F32)
    ar = [jnp.concatenate([at[bi][:, sl], rt[bi][:, sl]], axis=0) for bi, sl in zip(bis, sls)]
    vst = [blockdiag(v[bi][:, sl], headm) for bi, sl in zip(bis, sls)]
    xbk = [jnp.concatenate([blockdiag(bt[bi][:, sl], headm), blockdiag(kt[bi][:, sl], headm)], axis=0)
           for bi, sl in zip(bis, sls)]
    gram = [_mm_nt(ar[g], xbk[g]) for g in groups]
    hs = [state[g] for g in groups]
    hproj = [_mm_nt(ar[g], hs[g]) for g in groups]
    a_ab = [jnp.where(strict, gram[g][0:t, 0:n], 0.0) for g in groups]
    rhs = [hproj[g][0:t] + dot(jnp.where(strict, gram[g][0:t, n:2 * n], 0.0), vst[g]) for g in groups]
    tinv = [eye + a_ab[g] for g in groups]
    p = [dot(a_ab[g], blockdiag(a_ab[g], same)) for g in groups]
    steps = max(t - 1, 1).bit_length() - 1
    for i in range(steps):
        last = i + 1 == steps
        for g in groups:
            tb = blockdiag(tinv[g], same)
            prod = dot(p[g], tb if last else jnp.concatenate([tb, blockdiag(p[g], same)], axis=1))
            tinv[g] = tinv[g] + prod[:, 0:n]
            if not last:
                p[g] = prod[:, n:2 * n]
    u = [dot(tinv[g], blockdiag(rhs[g], headm)) for g in groups]
    for g in groups:
        r_bk = jnp.where(jnp.concatenate([incl, incl], axis=1), gram[g][t:2 * t, :], 0.0)
        ybuf[bis[g] * t:(bis[g] + 1) * t, sls[g]] = hproj[g][t:2 * t] + dot(
            r_bk, jnp.concatenate([blockdiag(u[g], headm), vst[g]], axis=0))
    for g in groups:
        bi, sl = bis[g], sls[g]
        upd = _mm_tn(jnp.concatenate([u[g], v[bi][:, sl]], axis=0),
                     jnp.concatenate([bt[bi][:, sl], kt[bi][:, sl]], axis=0))
        state[g] = (hs[g] + jnp.where(vals_same, upd, 0.0)) * ptot[bi][:, sl]

    y = ybuf[...]
    ind, indt = ind_ref[...], indt_ref[...]
    headmean = lambda q: _mm_sel_r(_mm_sel_r(q, ind, 2), indt, 2) * (1.0 / hd)
    mu = headmean(y)
    yc = y - mu
    var = headmean(yc * yc)
    yn = yc * lax.rsqrt(var + GN_EPS) * lnw_ref[...] + lnb_ref[...]
    yo = (yn + bonus_ref[...].reshape(nb * t, d)) * g_ref[...].reshape(nb * t, d)
    out = _mm(yo, ow_ref[...])
    for bi in range(nb):
        o_ref[bi] = x_ref[bi] + (1.0 + mod_ref[bi, 2:3, :]) * out[bi * t:(bi + 1) * t]


def _rwkv(x, mod, nw, mix, w_rkv, w0, w1, w2, a0, a1, a2, g1, g2, k_k, k_a, r_k, lnx_w, lnx_b, out_w):
    b, s, d = x.shape
    hd = HEAD_DIM
    assert d % WKV_GROUP == 0 and d // hd <= LANE

    def pad_pair(p1, p2):
        r = p1.shape[1]
        rp = -(-r // LANE) * LANE
        return (jnp.pad(p1, ((0, 0), (0, rp - r))).astype(BF16), jnp.pad(p2, ((0, rp - r), (0, 0))).astype(BF16))

    w1p, w2p = pad_pair(w1, w2)
    a1p, a2p = pad_pair(a1, a2)
    g1p, g2p = pad_pair(g1, g2)
    vecs = jnp.stack([w0, a0, k_k, k_a, r_k.reshape(d), jnp.zeros_like(w0), jnp.zeros_like(w0), jnp.zeros_like(w0)])
    ind, indt = _head_indicator(d, hd)
    wb = w_rkv.astype(BF16)
    t = ROW_TILE
    res = lambda arr: _resident(arr.shape)
    outs = pl.pallas_call(
        _r_in_kernel,
        grid=(b, s // t),
        in_specs=[_rows(t, d), _mod_spec(d), _resident((1, d)), _resident((6, d)),
                  _resident((d, d)), _resident((d, d)), _resident((d, d)),
                  res(w1p), res(w2p), res(a1p), res(a2p), res(g1p), res(g2p),
                  _resident((8, d)), res(ind), res(indt)],
        out_specs=[_rows(t, d)] * 8,
        out_shape=[jax.ShapeDtypeStruct((b, s, d), F32)] * 8,
        scratch_shapes=[pltpu.VMEM((t + 8, d), F32), pltpu.VMEM((t, d), F32)],
        compiler_params=_params(True),
        name="r_in",
    )(x, mod, nw.reshape(1, d), mix, wb[0], wb[1], wb[2], w1p, w2p, a1p, a2p, g1p, g2p, vecs, ind, indt)

    tc = WKV_CHUNK
    nb = WKV_BATCH_ROWS if b % WKV_BATCH_ROWS == 0 else 1
    rows = pl.BlockSpec((nb, tc, d), lambda bi, si: (bi, si, 0))
    return pl.pallas_call(
        functools.partial(_r_wkv_kernel, hd=hd),
        grid=(b // nb, s // tc),
        in_specs=[rows] * 9 + [pl.BlockSpec((nb, 6, d), lambda bi, si: (bi, 0, 0)), _resident((1, d)),
                               _resident((1, d)), res(ind), res(indt), _resident((d, d))],
        out_specs=rows,
        out_shape=jax.ShapeDtypeStruct(x.shape, F32),
        scratch_shapes=[pltpu.VMEM((nb * d // WKV_GROUP, WKV_GROUP, WKV_GROUP), F32),
                        pltpu.VMEM((nb * tc, d), F32)],
        compiler_params=_params(True),
        name="r_wkv",
    )(*outs, x, mod, lnx_w.reshape(1, d), lnx_b.reshape(1, d), ind, indt, out_w.astype(BF16))


def _f_in_kernel(x_ref, mod_ref, nw_ref, wq_ref, wk_ref, wvt_ref, wf_ref, vec_ref, fb_ref, ind_ref, indt_ref,
                 place_ref, ones_ref, q_ref, k_ref, vt_ref, qa_ref, ka_ref, carry, *, hd):
    t = x_ref.shape[0]

    @pl.when(pl.program_id(1) == 0)
    def _():
        carry[...] = jnp.zeros(carry.shape, F32)

    h = _norm_mod(x_ref[...], nw_ref[...], mod_ref[1:2, :], mod_ref[0:1, :]).astype(BF16)
    ind, indt = ind_ref[...], indt_ref[...]

    def head_rms(q, w):
        ms = _mm_sel_r(_mm_sel_r(q * q, ind, 2), indt, 2) * (1.0 / hd)
        return q * lax.rsqrt(ms + RMS_EPS) * w

    q = head_rms(jnp.dot(h, wq_ref[...], preferred_element_type=F32), vec_ref[0:1, :])
    k = head_rms(jnp.dot(h, wk_ref[...], preferred_element_type=F32), vec_ref[1:2, :])
    q_ref[...] = (q * (hd ** -0.5 * LOG2E)).astype(BF16)
    k_ref[...] = k.astype(BF16)
    vt_ref[...] = _mm_nt(wvt_ref[...], h).astype(BF16)
    f = jnp.dot(h, wf_ref[...], preferred_element_type=F32) + fb_ref[...]
    logf = -_softplus(-f)
    cum = _mm_sel_l(jnp.where(_tril(t), 1.0, 0.0).astype(BF16), logf) + carry[0:1, :]
    carry[...] = jnp.broadcast_to(cum[t - 1:t, :], carry.shape)
    pieces = _split(cum * LOG2E, 3)
    place = lambda first: sum(jnp.dot(p, place_ref[first + i], preferred_element_type=F32)
                              for i, p in enumerate(pieces))
    qa_ref[...] = (place(0) + ones_ref[0:1, :]).astype(BF16)
    ka_ref[...] = (ones_ref[1:2, :] - place(3)).astype(BF16)


def _f_att_kernel(q_ref, k_ref, vt_ref, qa_ref, ka_ref, x_ref, mod_ref, ow_ref, o_ref,
                  obuf, qm_scr, m_scr, acc_scr, *, hd, pairs_per_loop):
    tq = x_ref.shape[0]
    d = x_ref.shape[1]
    tk = vt_ref.shape[2]
    qi = pl.program_id(1)
    lane_l = lax.broadcasted_iota(jnp.int32, (1, LANE), 1)
    first_l = lane_l < hd
    first_r = lax.broadcasted_iota(jnp.int32, (LANE, 1), 0) < hd
    on_or_below = (lax.broadcasted_iota(jnp.int32, (tk, tq), 0)
                   <= lax.broadcasted_iota(jnp.int32, (tk, tq), 1))
    zero_b = jnp.zeros((), BF16)
    one_b = jnp.ones((), BF16)

    lanes = lambda p: slice(p * LANE, (p + 1) * LANE)
    for p0 in range(0, d // LANE, pairs_per_loop):
        pairs = range(p0, p0 + pairs_per_loop)
        heads = [2 * p + i for p in pairs for i in range(2)]
        qa = qa_ref[...]
        for p in pairs:
            qp = q_ref[:, lanes(p)]
            for i, qh in enumerate((jnp.where(first_l, qp, zero_b), jnp.where(first_l, zero_b, qp))):
                h = 2 * p + i
                own = (lane_l >= BIAS_COLS * h) & (lane_l < BIAS_COLS * (h + 1))
                qm_scr[h] = jnp.concatenate([qh, jnp.where(own, qa, zero_b)], axis=1)
        for h in heads:
            m_scr[h] = jnp.full((1, tq), NEG, F32)
            acc_scr[h] = jnp.zeros((LANE, tq), F32)

        def step(j, diagonal, pairs=pairs, heads=heads):
            start = pl.multiple_of(j * tk, tk)
            sc = {}
            kaj = ka_ref[pl.ds(start, tk), :]
            for p in pairs:
                kj = jnp.concatenate([k_ref[pl.ds(start, tk), lanes(p)], kaj], axis=1)
                for h in (2 * p, 2 * p + 1):
                    sc[h] = lax.dot_general(kj, qm_scr[h], (((1,), (1,)), ((), ())), preferred_element_type=F32)
            if diagonal:
                sc = {h: jnp.where(on_or_below, sc[h], NEG) for h in heads}
            m_old = {h: m_scr[h] for h in heads}
            m_new = {h: jnp.maximum(m_old[h], jnp.max(sc[h], axis=0, keepdims=True)) for h in heads}
            pr = {h: jnp.exp2(sc[h] - m_new[h]).astype(BF16) for h in heads}
            for p in pairs:
                vtj = vt_ref[j, lanes(p), :]
                vt2 = (jnp.where(first_r, vtj, one_b), jnp.where(first_r, one_b, vtj))
                for i in range(2):
                    h = 2 * p + i
                    acc_scr[h] = (jnp.exp2(m_old[h] - m_new[h]) * acc_scr[h]
                                  + jnp.dot(vt2[i], pr[h], preferred_element_type=F32))
                    m_scr[h] = m_new[h]

        def body(j, c):
            step(j, False)
            return c

        lax.fori_loop(0, qi, body, 0)
        step(qi, True)
        for p in pairs:
            acc0, acc1 = acc_scr[2 * p], acc_scr[2 * p + 1]
            o_t = jnp.where(first_r, acc0 / acc0[hd:hd + 1, :], acc1 / acc1[0:1, :])
            obuf[:, lanes(p)] = o_t.T.astype(BF16)

    out = jnp.dot(obuf[...], ow_ref[...], preferred_element_type=F32)
    o_ref[...] = x_ref[...] + (1.0 + mod_ref[2:3, :]) * out


def _fox(x, mod, nw, qkvf_w, fgate_b, q_norm_w, k_norm_w, out_w):
    b, s, d = x.shape
    hd = HEAD_DIM
    nh = d // hd
    assert qkvf_w.shape[1] == 3 * d + nh and nh <= LANE and 2 * hd == LANE
    wq, wk, wv = (qkvf_w[:, i * d:(i + 1) * d].astype(BF16) for i in range(3))
    wf = jnp.pad(qkvf_w[:, 3 * d:], ((0, 0), (0, LANE - nh))).astype(BF16)
    fb = jnp.pad(fgate_b, (0, LANE - nh)).reshape(1, LANE)
    z = jnp.zeros((d,), F32)
    vecs = jnp.stack([jnp.tile(q_norm_w, nh), jnp.tile(k_norm_w, nh), z, z, z, z, z, z])
    ind, indt = _head_indicator(d, hd)
    t = ROW_TILE
    res = lambda arr: _resident(arr.shape)
    nt = s // t
    assert BIAS_COLS * nh <= LANE
    col = jnp.arange(LANE)[None, None, :]
    place = (col == BIAS_COLS * jnp.arange(LANE)[None, :, None] + jnp.arange(BIAS_COLS)[:, None, None])
    place = (place & (jnp.arange(LANE)[None, :, None] < nh)).astype(BF16)
    third = (jnp.arange(LANE) % BIAS_COLS) // 3
    used = jnp.arange(LANE) < BIAS_COLS * nh
    z1 = jnp.zeros((LANE,), F32)
    ones = jnp.stack([(used & (third == 1)).astype(F32), (used & (third == 0)).astype(F32), z1, z1, z1, z1, z1, z1])
    q, k, vt, qa, ka = pl.pallas_call(
        functools.partial(_f_in_kernel, hd=hd),
        grid=(b, nt),
        in_specs=[_rows(t, d), _mod_spec(d), _resident((1, d)), _resident((d, d)), _resident((d, d)),
                  _resident((d, d)), _resident((d, LANE)), _resident((8, d)), _resident((1, LANE)),
                  res(ind), res(indt), res(place), res(ones)],
        out_specs=[_rows(t, d), _rows(t, d), pl.BlockSpec((None, None, d, t), lambda bi, si: (bi, si, 0, 0)),
                   _rows(t, LANE), _rows(t, LANE)],
        out_shape=[jax.ShapeDtypeStruct((b, s, d), BF16), jax.ShapeDtypeStruct((b, s, d), BF16),
                   jax.ShapeDtypeStruct((b, nt, d, t), BF16), jax.ShapeDtypeStruct((b, s, LANE), BF16),
                   jax.ShapeDtypeStruct((b, s, LANE), BF16)],
        scratch_shapes=[pltpu.VMEM((8, LANE), F32)],
        compiler_params=_params(True),
        name="f_in",
    )(x, mod, nw.reshape(1, d), wq, wk, wv.T, wf, vecs, fb, ind, indt, place, ones)

    whole = lambda *shape: pl.BlockSpec((None,) + shape, lambda bi, si: (bi,) + (0,) * len(shape))
    return pl.pallas_call(
        functools.partial(_f_att_kernel, hd=hd, pairs_per_loop=ATT_PAIRS_PER_LOOP),
        grid=(b, nt),
        in_specs=[_rows(t, d), whole(s, d), whole(nt, d, t), _rows(t, LANE), whole(s, LANE),
                  _rows(t, d), _mod_spec(d), _resident((d, d))],
        out_specs=_rows(t, d),
        out_shape=jax.ShapeDtypeStruct(x.shape, F32),
        scratch_shapes=[pltpu.VMEM((t, d), BF16), pltpu.VMEM((nh, t, 2 * LANE), BF16),
                        pltpu.VMEM((nh, 1, t), F32), pltpu.VMEM((nh, LANE, t), F32)],
        compiler_params=_params(False),
        name="f_att",
    )(q, k, vt, qa, ka, x, mod, out_w.astype(BF16))


def kernel(x, c, ada_w, ada_b, norm1_w, norm2_w, ffn_w1, ffn_w3, ffn_w2, m_in_w, m_conv_w, m_conv_b, m_dt_bias, m_A_log, m_D, m_norm_w, m_out_w, r_mix, r_w_rkv, r_w0, r_w1, r_w2, r_a0, r_a1, r_a2, r_g1, r_g2, r_k_k, r_k_a, r_r_k, r_lnx_w, r_lnx_b, r_out_w, f_qkvf_w, f_fgate_b, f_q_norm_w, f_k_norm_w, f_out_w):
    depth = ada_w.shape[0]
    mod = _ada(c, ada_w, ada_b)
    ia = ib = ic = 0
    for i in range(depth):
        kind = i % 3
        if kind == 0:
            x = _mamba(x, mod[i], norm1_w[i], m_in_w[ia], m_conv_w[ia], m_conv_b[ia], m_dt_bias[ia],
                       m_A_log[ia], m_D[ia], m_norm_w[ia], m_out_w[ia])
            ia += 1
        elif kind == 1:
            x = _rwkv(x, mod[i], norm1_w[i], r_mix[ib], r_w_rkv[ib], r_w0[ib], r_w1[ib], r_w2[ib], r_a0[ib],
                      r_a1[ib], r_a2[ib], r_g1[ib], r_g2[ib], r_k_k[ib], r_k_a[ib], r_r_k[ib],
                      r_lnx_w[ib], r_lnx_b[ib], r_out_w[ib])
            ib += 1
        else:
            x = _fox(x, mod[i], norm1_w[i], f_qkvf_w[ic], f_fgate_b[ic], f_q_norm_w[ic], f_k_norm_w[ic],
                     f_out_w[ic])
            ic += 1
        x = _ffn(x, mod[i], norm2_w[i], ffn_w1[i], ffn_w3[i], ffn_w2[i])
    return x
```

```python
import functools

import jax
import jax.numpy as jnp
from jax import lax
from jax.experimental import pallas as pl
from jax.experimental.pallas import tpu as pltpu

F32 = jnp.float32
BF16 = jnp.bfloat16

RMS_EPS = 1e-6
GN_EPS = 64e-5
KK_EPS = 1e-12
NEG = -1e30
LOG2E = 1.4426950408889634

V7X_VMEM_LIMIT = 56 * 1024 * 1024

HEAD_DIM = 64
SSD_GROUPS = 8
SSD_STATE = 128
SSD_CHUNK = 128
WKV_CHUNK = 64
WKV_GROUP = 128
WKV_BATCH_ROWS = 4
ROW_TILE = 256
FFN_ROW_TILE = 512
ATT_PAIRS_PER_LOOP = 8
BIAS_COLS = 6
LANE = 128


def _mm(a, b):
    return jnp.dot(a.astype(BF16), b.astype(BF16), preferred_element_type=F32)


def _mm_nt(a, b):
    return lax.dot_general(a.astype(BF16), b.astype(BF16), (((1,), (1,)), ((), ())),
                           preferred_element_type=F32)


def _mm_tn(a, b):
    return jnp.dot(a.T.astype(BF16), b.astype(BF16), preferred_element_type=F32)


def _split(x, terms):
    parts = []
    for i in range(terms):
        p = x.astype(BF16)
        parts.append(p)
        if i + 1 < terms:
            x = x - p.astype(F32)
    return parts


def _mm_sel_l(sel, x, terms=3):
    return sum(jnp.dot(sel, p, preferred_element_type=F32) for p in _split(x, terms))


def _mm_sel_r(x, sel, terms=3):
    return sum(jnp.dot(p, sel, preferred_element_type=F32) for p in _split(x, terms))


def _mm_hi(a, b):
    ah = a.astype(BF16)
    al = (a - ah.astype(F32)).astype(BF16)
    bh = b.astype(BF16)
    bl = (b - bh.astype(F32)).astype(BF16)
    d = lambda p, q: jnp.dot(p, q, preferred_element_type=F32)
    return d(ah, bh) + d(ah, bl) + d(al, bh)


def _sigmoid(x):
    return jax.nn.sigmoid(x)


def _silu(x):
    hx = 0.5 * x
    return hx + hx * jnp.tanh(hx)


def _softplus(x):
    return jnp.maximum(x, 0.0) + jnp.log(1.0 + jnp.exp(-jnp.abs(x)))


def _norm_mod(x, nw, scale, shift):
    y = x * lax.rsqrt(jnp.mean(x * x, axis=-1, keepdims=True) + RMS_EPS)
    return (y * nw) * (1.0 + scale) + shift


def _tril(n, strict=False):
    r = lax.broadcasted_iota(jnp.int32, (n, n), 0)
    c = lax.broadcasted_iota(jnp.int32, (n, n), 1)
    return (r > c) if strict else (r >= c)


def _resident(shape):
    nd = len(shape)
    return pl.BlockSpec(shape, lambda *_: (0,) * nd, pipeline_mode=pl.Buffered(1))


def _resident_layer(shape, layer):
    nd = len(shape)
    return pl.BlockSpec((None,) + tuple(shape), lambda *_: (layer,) + (0,) * nd, pipeline_mode=pl.Buffered(1))


def _rows(t, width):
    return pl.BlockSpec((None, t, width), lambda b, s: (b, s, 0))


def _mod_spec(d):
    return pl.BlockSpec((None, 6, d), lambda b, s: (b, 0, 0))


def _params(seq_axis_carries):
    sem = ("parallel", "arbitrary") if seq_axis_carries else ("parallel", "parallel")
    return pltpu.CompilerParams(dimension_semantics=sem, vmem_limit_bytes=V7X_VMEM_LIMIT)


def _head_indicator(d, hd):
    ind = (jnp.arange(d)[:, None] // hd == jnp.arange(LANE)[None, :]).astype(BF16)
    return ind, ind.T


def _ada_kernel(c_ref, w_ref, b_ref, o_ref):
    c = c_ref[...]
    o_ref[...] = _mm_hi(_silu(c), w_ref[...]) + b_ref[...]


def _ada(c, ada_w, ada_b):
    depth, d, n6 = ada_w.shape
    b = c.shape[0]
    tn = 1536
    out = pl.pallas_call(
        _ada_kernel,
        grid=(depth, n6 // tn),
        in_specs=[pl.BlockSpec((b, d), lambda l, j: (0, 0)),
                  pl.BlockSpec((None, d, tn), lambda l, j: (l, 0, j)),
                  pl.BlockSpec((None, 1, tn), lambda l, j: (l, 0, j))],
        out_specs=pl.BlockSpec((None, b, tn), lambda l, j: (l, 0, j)),
        out_shape=jax.ShapeDtypeStruct((depth, b, n6), F32),
        compiler_params=_params(False),
        name="ada",
    )(c, ada_w, ada_b.reshape(depth, 1, n6))
    return out.reshape(depth, b, 6, d)


def _ffn_kernel(x_ref, mod_ref, nw_ref, w1_ref, w3_ref, w2_ref, o_ref):
    x = x_ref[...]
    h = _norm_mod(x, nw_ref[...], mod_ref[4:5, :], mod_ref[3:4, :]).astype(BF16)
    a = jnp.dot(h, w1_ref[...], preferred_element_type=F32)
    b = jnp.dot(h, w3_ref[...], preferred_element_type=F32)
    g = (_silu(a) * b).astype(BF16)
    y = jnp.dot(g, w2_ref[...], preferred_element_type=F32)
    o_ref[...] = x + (1.0 + mod_ref[5:6, :]) * y


def _ffn(x, mod, nw, w1, w3, w2, layer):
    b, s, d = x.shape
    dff = w1.shape[2]
    t = FFN_ROW_TILE if s % FFN_ROW_TILE == 0 else ROW_TILE
    return pl.pallas_call(
        _ffn_kernel,
        grid=(b, s // t),
        in_specs=[_rows(t, d), _mod_spec(d), _resident((1, d)), _resident_layer((d, dff), layer),
                  _resident_layer((d, dff), layer), _resident_layer((dff, d), layer)],
        out_specs=_rows(t, d),
        out_shape=jax.ShapeDtypeStruct(x.shape, F32),
        compiler_params=_params(False),
        name="ffn",
    )(x, mod, nw.reshape(1, d), w1, w3, w2)


def _m_in_kernel(x_ref, mod_ref, nw_ref, w_ref, z_ref, xbc_ref, dt_ref):
    h = _norm_mod(x_ref[...], nw_ref[...], mod_ref[1:2, :], mod_ref[0:1, :]).astype(BF16)
    di, cdim = z_ref.shape[1], xbc_ref.shape[1]
    z_ref[...] = jnp.dot(h, w_ref[:, 0:di], preferred_element_type=F32)
    xbc_ref[...] = jnp.dot(h, w_ref[:, di:di + cdim], preferred_element_type=F32)
    dt_ref[...] = jnp.dot(h, w_ref[:, di + cdim:], preferred_element_type=F32)


def _m_ssd_kernel(xbc_ref, z_ref, dt_ref, x_ref, mod_ref, cw_ref, cb_ref, dtb_ref, alog_ref, e_ref,
                  dskip_ref, gnw_ref, ow_ref, o_ref, ubuf, st, ybuf, *, nh):
    t = x_ref.shape[0]
    di = z_ref.shape[1]
    ng, ns, gw = st.shape
    hpg = nh // ng
    hd = gw // hpg

    @pl.when(pl.program_id(1) == 0)
    def _():
        ubuf[0:8, :] = jnp.zeros((8, ubuf.shape[1]), F32)
        st[...] = jnp.zeros(st.shape, F32)

    u = xbc_ref[...]
    ubuf[8:8 + t, :] = u
    acc = cb_ref[...] + cw_ref[3:4, :] * u
    for k in range(3):
        acc = acc + cw_ref[k:k + 1, :] * ubuf[5 + k:5 + k + t, :]
    ubuf[0:8, :] = u[t - 8:t, :]
    xc = _silu(acc)
    xs = xc[:, :di]

    lane = lax.broadcasted_iota(jnp.int32, (1, LANE), 1)
    dt = _softplus(dt_ref[...] + dtb_ref[...])
    a_neg = jnp.where(lane < nh, -jnp.exp(alog_ref[...]), 0.0)
    a = dt * a_neg
    tril = _tril(t)
    cs = _mm_sel_l(jnp.where(tril, 1.0, 0.0).astype(BF16), a)
    cs_t = cs.T
    e = e_ref[...]
    dt_e = _mm_sel_r(dt, e, 2)
    cs_e = _mm_sel_r(cs, e, 2)
    cs_last = cs_e[t - 1:t, :]
    xd = xs * dt_e
    ecs = jnp.exp(cs_e)
    xdd = xd * jnp.exp(cs_last - cs_e)
    cdec = jnp.exp(cs_last)
    yskip = xs * dskip_ref[...]
    gate = _silu(z_ref[...])

    rr = lax.broadcasted_iota(jnp.int32, (hpg * t, gw), 0) // t
    ll = lax.broadcasted_iota(jnp.int32, (hpg * t, gw), 1) // hd
    bd = rr == ll
    groups = range(ng)
    sls = [slice(g * gw, (g + 1) * gw) for g in groups]
    bgs = [xc[:, di + g * ns:di + (g + 1) * ns].astype(BF16) for g in groups]
    cgs = [xc[:, di + (ng + g) * ns:di + (ng + g + 1) * ns].astype(BF16) for g in groups]
    scores = [_mm_nt(cgs[g], bgs[g]) for g in groups]
    sts = [st[g] for g in groups]
    yoff = [_mm(cgs[g], sts[g]) for g in groups]
    supd = [_mm_tn(bgs[g], xdd[:, sls[g]]) for g in groups]
    ydiag = []
    for g in groups:
        parts = []
        for j in range(hpg):
            h = g * hpg + j
            diff = cs[:, h:h + 1] - cs_t[h:h + 1, :]
            parts.append((scores[g] * jnp.exp(jnp.where(tril, diff, NEG))).astype(BF16))
        lhs = jnp.concatenate(parts, axis=1)
        rhs = jnp.where(bd, jnp.concatenate([xd[:, sls[g]]] * hpg, axis=0), 0.0).astype(BF16)
        ydiag.append(jnp.dot(lhs, rhs, preferred_element_type=F32))
    for g in groups:
        sl = sls[g]
        st[g] = sts[g] * cdec[:, sl] + supd[g]
        y = (ydiag[g] + yoff[g] * ecs[:, sl] + yskip[:, sl]) * gate[:, sl]
        y = y * lax.rsqrt(jnp.mean(y * y, axis=-1, keepdims=True) + RMS_EPS) * gnw_ref[:, sl]
        ybuf[:, sl] = y.astype(BF16)

    out = jnp.dot(ybuf[...], ow_ref[...], preferred_element_type=F32)
    o_ref[...] = x_ref[...] + (1.0 + mod_ref[2:3, :]) * out


def _mamba(x, mod, nw, in_w, conv_w, conv_b, dt_bias, a_log, d_skip, norm_w, out_w, layer):
    b, s, d = x.shape
    nh = dt_bias.shape[0]
    di = nh * HEAD_DIM
    cdim = conv_w.shape[1]
    ng, ns = SSD_GROUPS, SSD_STATE
    gw = di // ng
    assert cdim == di + 2 * ng * ns and in_w.shape[2] == di + cdim + LANE and nh <= LANE
    t = ROW_TILE
    z, xbc, dtr = pl.pallas_call(
        _m_in_kernel,
        grid=(b, s // t),
        in_specs=[_rows(t, d), _mod_spec(d), _resident((1, d)),
                  _resident_layer((d, di + cdim + LANE), layer)],
        out_specs=[_rows(t, di), _rows(t, cdim), _rows(t, LANE)],
        out_shape=[jax.ShapeDtypeStruct((b, s, di), F32), jax.ShapeDtypeStruct((b, s, cdim), F32),
                   jax.ShapeDtypeStruct((b, s, LANE), F32)],
        compiler_params=_params(False),
        name="m_in",
    )(x, mod, nw.reshape(1, d), in_w)

    pad1 = lambda v: jnp.pad(v, (0, LANE - nh)).reshape(1, LANE)
    expand = (jnp.arange(LANE)[:, None] == jnp.arange(di)[None, :] // HEAD_DIM).astype(BF16)
    tc = SSD_CHUNK
    return pl.pallas_call(
        functools.partial(_m_ssd_kernel, nh=nh),
        grid=(b, s // tc),
        in_specs=[_rows(tc, cdim), _rows(tc, di), _rows(tc, LANE), _rows(tc, d), _mod_spec(d),
                  _resident((4, cdim)), _resident((1, cdim)), _resident((1, LANE)), _resident((1, LANE)),
                  _resident((LANE, di)), _resident((1, di)), _resident((1, di)),
                  _resident_layer((di, d), layer)],
        out_specs=_rows(tc, d),
        out_shape=jax.ShapeDtypeStruct(x.shape, F32),
        scratch_shapes=[pltpu.VMEM((tc + 8, cdim), F32), pltpu.VMEM((ng, ns, gw), F32),
                        pltpu.VMEM((tc, di), BF16)],
        compiler_params=_params(True),
        name="m_ssd",
    )(xbc, z, dtr, x, mod, conv_w, conv_b.reshape(1, cdim), pad1(dt_bias), pad1(a_log), expand,
      jnp.repeat(d_skip, HEAD_DIM).reshape(1, di), norm_w.reshape(1, di), out_w)


def _r_in_kernel(x_ref, mod_ref, nw_ref, mix_ref, wrkv_ref, w1_ref, w2_ref, a1_ref, a2_ref,
                 g1_ref, g2_ref, vec_ref, ind_ref, indt_ref,
                 r_ref, lw_ref, k_ref, v_ref, kk_ref, bb_ref, bonus_ref, g_ref, hbuf, dxbuf):
    t = x_ref.shape[0]

    @pl.when(pl.program_id(1) == 0)
    def _():
        hbuf[0:8, :] = jnp.zeros((8, hbuf.shape[1]), F32)

    hbuf[8:8 + t, :] = _norm_mod(x_ref[...], nw_ref[...], mod_ref[1:2, :], mod_ref[0:1, :])
    dxbuf[...] = hbuf[7:7 + t, :] - hbuf[8:8 + t, :]
    hbuf[0:8, :] = hbuf[t:t + 8, :]
    w0, a0, k_k, k_a, r_k = (vec_ref[i:i + 1, :] for i in range(5))
    ind, indt = ind_ref[...], indt_ref[...]
    headsum = lambda q: _mm_sel_r(_mm_sel_r(q, ind, 2), indt, 2)
    dotf = lambda a, b_ref: jnp.dot(a, b_ref[...], preferred_element_type=F32)
    mixed = lambda n: (hbuf[8:8 + t, :] + dxbuf[...] * mix_ref[n:n + 1, :]).astype(BF16)

    r = dotf(mixed(0), wrkv_ref.at[0])
    k = dotf(mixed(1), wrkv_ref.at[1])
    v = dotf(mixed(2), wrkv_ref.at[2])
    wl = dotf(jnp.tanh(dotf(mixed(3), w1_ref)).astype(BF16), w2_ref)
    al = dotf(dotf(mixed(4), a1_ref).astype(BF16), a2_ref)
    r_ref[...] = r
    v_ref[...] = v
    g_ref[...] = dotf(_sigmoid(dotf(mixed(5), g1_ref)).astype(BF16), g2_ref)
    lw_ref[...] = -jnp.exp(-_softplus(-(w0 + wl)) - 0.5)
    a = _sigmoid(a0 + al)
    kk = k * k_k
    kk = kk * lax.rsqrt(jnp.maximum(headsum(kk * kk), KK_EPS * KK_EPS))
    k2 = k * (1.0 + (a - 1.0) * k_a)
    k_ref[...] = k2
    kk_ref[...] = kk
    bb_ref[...] = kk * a
    bonus_ref[...] = headsum(r * k2 * r_k) * v


def _r_wkv_kernel(r_ref, lw_ref, k_ref, v_ref, kk_ref, bb_ref, bonus_ref, g_ref, x_ref, mod_ref,
                  lnw_ref, lnb_ref, ind_ref, indt_ref, ow_ref, o_ref, state, ybuf, *, hd):
    nb, t, d = x_ref.shape
    gw = state.shape[1]
    ngr = d // gw
    hpg = gw // hd
    n = hpg * t

    @pl.when(pl.program_id(1) == 0)
    def _():
        state[...] = jnp.zeros(state.shape, F32)

    tri = jnp.where(_tril(t), 1.0, 0.0).astype(BF16)
    at, rt, kt, bt, v, ptot = [], [], [], [], [], []
    for bi in range(nb):
        lw = lw_ref[bi]
        cl = _mm_sel_l(tri, lw)
        ecl = jnp.exp(cl)
        encl = jnp.exp(-cl)
        rt.append(r_ref[bi] * ecl)
        at.append(-kk_ref[bi] * jnp.exp(cl - lw))
        kt.append(k_ref[bi] * encl)
        bt.append(bb_ref[bi] * encl)
        v.append(v_ref[bi])
        ptot.append(ecl[t - 1:t, :])

    ri = lax.broadcasted_iota(jnp.int32, (n, gw), 0)
    ci = lax.broadcasted_iota(jnp.int32, (n, gw), 1)
    headm = (ri // t) == (ci // hd)
    ri2 = lax.broadcasted_iota(jnp.int32, (n, n), 0)
    ci2 = lax.broadcasted_iota(jnp.int32, (n, n), 1)
    same = (ri2 // t) == (ci2 // t)
    tt = lax.broadcasted_iota(jnp.int32, (t, n), 0)
    ss = lax.broadcasted_iota(jnp.int32, (t, n), 1) % t
    strict = tt > ss
    incl = tt >= ss
    eye = jnp.where(tt == ss, 1.0, 0.0)
    vals_same = (lax.broadcasted_iota(jnp.int32, (gw, gw), 0) // hd) == (
        lax.broadcasted_iota(jnp.int32, (gw, gw), 1) // hd)

    def blockdiag(m, mask):
        return jnp.where(mask, jnp.concatenate([m] * hpg, axis=0), 0.0).astype(BF16)

    groups = range(nb * ngr)
    bis = [g // ngr for g in groups]
    sls = [slice((g % ngr) * gw, (g % ngr + 1) * gw) for g in groups]
    dot = lambda a, b: jnp.dot(a.astype(BF16), b, preferred_element_type=F32)
    ar = [jnp.concatenate([at[bi][:, sl], rt[bi][:, sl]], axis=0) for bi, sl in zip(bis, sls)]
    vst = [blockdiag(v[bi][:, sl], headm) for bi, sl in zip(bis, sls)]
    xbk = [jnp.concatenate([blockdiag(bt[bi][:, sl], headm), blockdiag(kt[bi][:, sl], headm)], axis=0)
           for bi, sl in zip(bis, sls)]
    gram = [_mm_nt(ar[g], xbk[g]) for g in groups]
    hs = [state[g] for g in groups]
    hproj = [_mm_nt(ar[g], hs[g]) for g in groups]
    a_ab = [jnp.where(strict, gram[g][0:t, 0:n], 0.0) for g in groups]
    rhs = [hproj[g][0:t] + dot(jnp.where(strict, gram[g][0:t, n:2 * n], 0.0), vst[g]) for g in groups]
    tinv = [eye + a_ab[g] for g in groups]
    p = [dot(a_ab[g], blockdiag(a_ab[g], same)) for g in groups]
    steps = max(t - 1, 1).bit_length() - 1
    for i in range(steps):
        last = i + 1 == steps
        for g in groups:
            tb = blockdiag(tinv[g], same)
            prod = dot(p[g], tb if last else jnp.concatenate([tb, blockdiag(p[g], same)], axis=1))
            tinv[g] = tinv[g] + prod[:, 0:n]
            if not last:
                p[g] = prod[:, n:2 * n]
    u = [dot(tinv[g], blockdiag(rhs[g], headm)) for g in groups]
    for g in groups:
        r_bk = jnp.where(jnp.concatenate([incl, incl], axis=1), gram[g][t:2 * t, :], 0.0)
        ybuf[bis[g] * t:(bis[g] + 1) * t, sls[g]] = hproj[g][t:2 * t] + dot(
            r_bk, jnp.concatenate([blockdiag(u[g], headm), vst[g]], axis=0))
    for g in groups:
        bi, sl = bis[g], sls[g]
        upd = _mm_tn(jnp.concatenate([u[g], v[bi][:, sl]], axis=0),
                     jnp.concatenate([bt[bi][:, sl], kt[bi][:, sl]], axis=0))
        state[g] = (hs[g] + jnp.where(vals_same, upd, 0.0)) * ptot[bi][:, sl]

    y = ybuf[...]
    ind, indt = ind_ref[...], indt_ref[...]
    headmean = lambda q: _mm_sel_r(_mm_sel_r(q, ind, 2), indt, 2) * (1.0 / hd)
    mu = headmean(y)
    yc = y - mu
    var = headmean(yc * yc)
    yn = yc * lax.rsqrt(var + GN_EPS) * lnw_ref[...] + lnb_ref[...]
    yo = (yn + bonus_ref[...].reshape(nb * t, d)) * g_ref[...].reshape(nb * t, d)
    out = _mm(yo, ow_ref[...])
    for bi in range(nb):
        o_ref[bi] = x_ref[bi] + (1.0 + mod_ref[bi, 2:3, :]) * out[bi * t:(bi + 1) * t]


def _rwkv(x, mod, nw, mix, w_rkv, w0, w1, w2, a0, a1, a2, g1, g2, k_k, k_a, r_k, lnx_w, lnx_b, out_w):
    b, s, d = x.shape
    hd = HEAD_DIM
    assert d % WKV_GROUP == 0 and d // hd <= LANE

    def pad_pair(p1, p2):
        r = p1.shape[1]
        rp = -(-r // LANE) * LANE
        return (jnp.pad(p1, ((0, 0), (0, rp - r))).astype(BF16), jnp.pad(p2, ((0, rp - r), (0, 0))).astype(BF16))

    w1p, w2p = pad_pair(w1, w2)
    a1p, a2p = pad_pair(a1, a2)
    g1p, g2p = pad_pair(g1, g2)
    vecs = jnp.stack([w0, a0, k_k, k_a, r_k.reshape(d), jnp.zeros_like(w0), jnp.zeros_like(w0), jnp.zeros_like(w0)])
    ind, indt = _head_indicator(d, hd)
    wb = w_rkv.astype(BF16)
    t = ROW_TILE
    res = lambda arr: _resident(arr.shape)
    outs = pl.pallas_call(
        _r_in_kernel,
        grid=(b, s // t),
        in_specs=[_rows(t, d), _mod_spec(d), _resident((1, d)), _resident((6, d)), _resident((3, d, d)),
                  res(w1p), res(w2p), res(a1p), res(a2p), res(g1p), res(g2p),
                  _resident((8, d)), res(ind), res(indt)],
        out_specs=[_rows(t, d)] * 8,
        out_shape=[jax.ShapeDtypeStruct((b, s, d), F32)] * 8,
        scratch_shapes=[pltpu.VMEM((t + 8, d), F32), pltpu.VMEM((t, d), F32)],
        compiler_params=_params(True),
        name="r_in",
    )(x, mod, nw.reshape(1, d), mix, wb, w1p, w2p, a1p, a2p, g1p, g2p, vecs, ind, indt)

    tc = WKV_CHUNK
    nb = WKV_BATCH_ROWS if b % WKV_BATCH_ROWS == 0 else 1
    rows = pl.BlockSpec((nb, tc, d), lambda bi, si: (bi, si, 0))
    return pl.pallas_call(
        functools.partial(_r_wkv_kernel, hd=hd),
        grid=(b // nb, s // tc),
        in_specs=[rows] * 9 + [pl.BlockSpec((nb, 6, d), lambda bi, si: (bi, 0, 0)), _resident((1, d)),
                               _resident((1, d)), res(ind), res(indt), _resident((d, d))],
        out_specs=rows,
        out_shape=jax.ShapeDtypeStruct(x.shape, F32),
        scratch_shapes=[pltpu.VMEM((nb * d // WKV_GROUP, WKV_GROUP, WKV_GROUP), F32),
                        pltpu.VMEM((nb * tc, d), F32)],
        compiler_params=_params(True),
        name="r_wkv",
    )(*outs, x, mod, lnx_w.reshape(1, d), lnx_b.reshape(1, d), ind, indt, out_w.astype(BF16))


def _f_in_kernel(x_ref, mod_ref, nw_ref, w_ref, wvt_ref, vec_ref, fb_ref, ind_ref, indt_ref,
                 place_ref, ones_ref, q_ref, k_ref, vt_ref, qa_ref, ka_ref, carry, *, hd):
    t = x_ref.shape[0]

    @pl.when(pl.program_id(1) == 0)
    def _():
        carry[...] = jnp.zeros(carry.shape, F32)

    h = _norm_mod(x_ref[...], nw_ref[...], mod_ref[1:2, :], mod_ref[0:1, :]).astype(BF16)
    ind, indt = ind_ref[...], indt_ref[...]

    def head_rms(q, w):
        ms = _mm_sel_r(_mm_sel_r(q * q, ind, 2), indt, 2) * (1.0 / hd)
        return q * lax.rsqrt(ms + RMS_EPS) * w

    d = x_ref.shape[1]
    q = head_rms(jnp.dot(h, w_ref[:, 0:d], preferred_element_type=F32), vec_ref[0:1, :])
    k = head_rms(jnp.dot(h, w_ref[:, d:2 * d], preferred_element_type=F32), vec_ref[1:2, :])
    q_ref[...] = (q * (hd ** -0.5 * LOG2E)).astype(BF16)
    k_ref[...] = k.astype(BF16)
    vt_ref[...] = _mm_nt(wvt_ref[...], h).astype(BF16)
    f = jnp.dot(h, w_ref[:, 3 * d:], preferred_element_type=F32) + fb_ref[...]
    logf = -_softplus(-f)
    cum = _mm_sel_l(jnp.where(_tril(t), 1.0, 0.0).astype(BF16), logf) + carry[0:1, :]
    carry[...] = jnp.broadcast_to(cum[t - 1:t, :], carry.shape)
    pieces = _split(cum * LOG2E, 3)
    place = lambda first: sum(jnp.dot(p, place_ref[first + i], preferred_element_type=F32)
                              for i, p in enumerate(pieces))
    qa_ref[...] = (place(0) + ones_ref[0:1, :]).astype(BF16)
    ka_ref[...] = (ones_ref[1:2, :] - place(3)).astype(BF16)


def _f_att_kernel(q_ref, k_ref, vt_ref, qa_ref, ka_ref, x_ref, mod_ref, ow_ref, o_ref,
                  obuf, qm_scr, m_scr, acc_scr, *, hd, pairs_per_loop):
    tq = x_ref.shape[0]
    d = x_ref.shape[1]
    tk = vt_ref.shape[2]
    qi = pl.program_id(1)
    lane_l = lax.broadcasted_iota(jnp.int32, (1, LANE), 1)
    first_l = lane_l < hd
    first_r = lax.broadcasted_iota(jnp.int32, (LANE, 1), 0) < hd
    on_or_below = (lax.broadcasted_iota(jnp.int32, (tk, tq), 0)
                   <= lax.broadcasted_iota(jnp.int32, (tk, tq), 1))
    zero_b = jnp.zeros((), BF16)
    one_b = jnp.ones((), BF16)

    lanes = lambda p: slice(p * LANE, (p + 1) * LANE)
    for p0 in range(0, d // LANE, pairs_per_loop):
        pairs = range(p0, p0 + pairs_per_loop)
        heads = [2 * p + i for p in pairs for i in range(2)]
        qa = qa_ref[...]
        for p in pairs:
            qp = q_ref[:, lanes(p)]
            for i, qh in enumerate((jnp.where(first_l, qp, zero_b), jnp.where(first_l, zero_b, qp))):
                h = 2 * p + i
                own = (lane_l >= BIAS_COLS * h) & (lane_l < BIAS_COLS * (h + 1))
                qm_scr[h] = jnp.concatenate([qh, jnp.where(own, qa, zero_b)], axis=1)
        for h in heads:
            m_scr[h] = jnp.full((1, tq), NEG, F32)
            acc_scr[h] = jnp.zeros((LANE, tq), F32)

        def step(j, diagonal, pairs=pairs, heads=heads):
            start = pl.multiple_of(j * tk, tk)
            sc = {}
            kaj = ka_ref[pl.ds(start, tk), :]
            for p in pairs:
                kj = jnp.concatenate([k_ref[pl.ds(start, tk), lanes(p)], kaj], axis=1)
                for h in (2 * p, 2 * p + 1):
                    sc[h] = lax.dot_general(kj, qm_scr[h], (((1,), (1,)), ((), ())), preferred_element_type=F32)
            if diagonal:
                sc = {h: jnp.where(on_or_below, sc[h], NEG) for h in heads}
            m_old = {h: m_scr[h] for h in heads}
            m_new = {h: jnp.maximum(m_old[h], jnp.max(sc[h], axis=0, keepdims=True)) for h in heads}
            pr = {h: jnp.exp2(sc[h] - m_new[h]).astype(BF16) for h in heads}
            for p in pairs:
                vtj = vt_ref[j, lanes(p), :]
                vt2 = (jnp.where(first_r, vtj, one_b), jnp.where(first_r, one_b, vtj))
                for i in range(2):
                    h = 2 * p + i
                    acc_scr[h] = (jnp.exp2(m_old[h] - m_new[h]) * acc_scr[h]
                                  + jnp.dot(vt2[i], pr[h], preferred_element_type=F32))
                    m_scr[h] = m_new[h]

        def body(j, c):
            step(j, False)
            return c

        lax.fori_loop(0, qi, body, 0)
        step(qi, True)
        for p in pairs:
            acc0, acc1 = acc_scr[2 * p], acc_scr[2 * p + 1]
            o_t = jnp.where(first_r, acc0 / acc0[hd:hd + 1, :], acc1 / acc1[0:1, :])
            obuf[:, lanes(p)] = o_t.T.astype(BF16)

    out = jnp.dot(obuf[...], ow_ref[...], preferred_element_type=F32)
    o_ref[...] = x_ref[...] + (1.0 + mod_ref[2:3, :]) * out


def _fox(x, mod, nw, qkvf_w, fgate_b, q_norm_w, k_norm_w, out_w):
    b, s, d = x.shape
    hd = HEAD_DIM
    nh = d // hd
    assert qkvf_w.shape[1] == 3 * d + nh and nh <= LANE and 2 * hd == LANE
    w_all = jnp.pad(qkvf_w.astype(BF16), ((0, 0), (0, LANE - nh)))
    fb = jnp.pad(fgate_b, (0, LANE - nh)).reshape(1, LANE)
    z = jnp.zeros((d,), F32)
    vecs = jnp.stack([jnp.tile(q_norm_w, nh), jnp.tile(k_norm_w, nh), z, z, z, z, z, z])
    ind, indt = _head_indicator(d, hd)
    t = ROW_TILE
    res = lambda arr: _resident(arr.shape)
    nt = s // t
    assert BIAS_COLS * nh <= LANE
    col = jnp.arange(LANE)[None, None, :]
    place = (col == BIAS_COLS * jnp.arange(LANE)[None, :, None] + jnp.arange(BIAS_COLS)[:, None, None])
    place = (place & (jnp.arange(LANE)[None, :, None] < nh)).astype(BF16)
    third = (jnp.arange(LANE) % BIAS_COLS) // 3
    used = jnp.arange(LANE) < BIAS_COLS * nh
    z1 = jnp.zeros((LANE,), F32)
    ones = jnp.stack([(used & (third == 1)).astype(F32), (used & (third == 0)).astype(F32), z1, z1, z1, z1, z1, z1])
    q, k, vt, qa, ka = pl.pallas_call(
        functools.partial(_f_in_kernel, hd=hd),
        grid=(b, nt),
        in_specs=[_rows(t, d), _mod_spec(d), _resident((1, d)), res(w_all),
                  _resident((d, d)), _resident((8, d)), _resident((1, LANE)),
                  res(ind), res(indt), res(place), res(ones)],
        out_specs=[_rows(t, d), _rows(t, d), pl.BlockSpec((None, None, d, t), lambda bi, si: (bi, si, 0, 0)),
                   _rows(t, LANE), _rows(t, LANE)],
        out_shape=[jax.ShapeDtypeStruct((b, s, d), BF16), jax.ShapeDtypeStruct((b, s, d), BF16),
                   jax.ShapeDtypeStruct((b, nt, d, t), BF16), jax.ShapeDtypeStruct((b, s, LANE), BF16),
                   jax.ShapeDtypeStruct((b, s, LANE), BF16)],
        scratch_shapes=[pltpu.VMEM((8, LANE), F32)],
        compiler_params=_params(True),
        name="f_in",
    )(x, mod, nw.reshape(1, d), w_all, w_all[:, 2 * d:3 * d].T, vecs, fb, ind, indt, place, ones)

    whole = lambda *shape: pl.BlockSpec((None,) + shape, lambda bi, si: (bi,) + (0,) * len(shape))
    return pl.pallas_call(
        functools.partial(_f_att_kernel, hd=hd, pairs_per_loop=ATT_PAIRS_PER_LOOP),
        grid=(b, nt),
        in_specs=[_rows(t, d), whole(s, d), whole(nt, d, t), _rows(t, LANE), whole(s, LANE),
                  _rows(t, d), _mod_spec(d), _resident((d, d))],
        out_specs=_rows(t, d),
        out_shape=jax.ShapeDtypeStruct(x.shape, F32),
        scratch_shapes=[pltpu.VMEM((t, d), BF16), pltpu.VMEM((nh, t, 2 * LANE), BF16),
                        pltpu.VMEM((nh, 1, t), F32), pltpu.VMEM((nh, LANE, t), F32)],
        compiler_params=_params(False),
        name="f_att",
    )(q, k, vt, qa, ka, x, mod, out_w.astype(BF16))


def kernel(x, c, ada_w, ada_b, norm1_w, norm2_w, ffn_w1, ffn_w3, ffn_w2, m_in_w, m_conv_w, m_conv_b, m_dt_bias, m_A_log, m_D, m_norm_w, m_out_w, r_mix, r_w_rkv, r_w0, r_w1, r_w2, r_a0, r_a1, r_a2, r_g1, r_g2, r_k_k, r_k_a, r_r_k, r_lnx_w, r_lnx_b, r_out_w, f_qkvf_w, f_fgate_b, f_q_norm_w, f_k_norm_w, f_out_w):
    depth = ada_w.shape[0]
    mod = _ada(c, ada_w, ada_b)
    w1b, w3b, w2b = ffn_w1.astype(BF16), ffn_w3.astype(BF16), ffn_w2.astype(BF16)
    m_pad = LANE - m_dt_bias.shape[1]
    m_in_b = jnp.pad(m_in_w.astype(BF16), ((0, 0), (0, 0), (0, m_pad)))
    m_out_b = m_out_w.astype(BF16)
    ia = ib = ic = 0
    for i in range(depth):
        kind = i % 3
        if kind == 0:
            x = _mamba(x, mod[i], norm1_w[i], m_in_b, m_conv_w[ia], m_conv_b[ia], m_dt_bias[ia],
                       m_A_log[ia], m_D[ia], m_norm_w[ia], m_out_b, ia)
            ia += 1
        elif kind == 1:
            x = _rwkv(x, mod[i], norm1_w[i], r_mix[ib], r_w_rkv[ib], r_w0[ib], r_w1[ib], r_w2[ib], r_a0[ib],
                      r_a1[ib], r_a2[ib], r_g1[ib], r_g2[ib], r_k_k[ib], r_k_a[ib], r_r_k[ib],
                      r_lnx_w[ib], r_lnx_b[ib], r_out_w[ib])
            ib += 1
        else:
            x = _fox(x, mod[i], norm1_w[i], f_qkvf_w[ic], f_fgate_b[ic], f_q_norm_w[ic], f_k_norm_w[ic],
                     f_out_w[ic])
            ic += 1
        x = _ffn(x, mod[i], norm2_w[i], w1b, w3b, w2b, i)
    return x
```

```python
import functools

import jax
import jax.numpy as jnp
from jax import lax
from jax.experimental import pallas as pl
from jax.experimental.pallas import tpu as pltpu

F32 = jnp.float32
BF16 = jnp.bfloat16

RMS_EPS = 1e-6
GN_EPS = 64e-5
KK_EPS = 1e-12
NEG = -1e30
LOG2E = 1.4426950408889634

V7X_VMEM_LIMIT = 56 * 1024 * 1024

HEAD_DIM = 64
SSD_GROUPS = 8
SSD_STATE = 128
SSD_CHUNK = 128
WKV_CHUNK = 64
WKV_GROUP = 128
WKV_BATCH_ROWS = 4
ROW_TILE = 256
FFN_ROW_TILE = 512
ATT_PAIRS_PER_LOOP = 8
BIAS_COLS = 6
LANE = 128


def _mm(a, b):
    return jnp.dot(a.astype(BF16), b.astype(BF16), preferred_element_type=F32)


def _mm_nt(a, b):
    return lax.dot_general(a.astype(BF16), b.astype(BF16), (((1,), (1,)), ((), ())),
                           preferred_element_type=F32)


def _mm_tn(a, b):
    return jnp.dot(a.T.astype(BF16), b.astype(BF16), preferred_element_type=F32)


def _split(x, terms):
    parts = []
    for i in range(terms):
        p = x.astype(BF16)
        parts.append(p)
        if i + 1 < terms:
            x = x - p.astype(F32)
    return parts


def _mm_sel_l(sel, x, terms=3):
    return sum(jnp.dot(sel, p, preferred_element_type=F32) for p in _split(x, terms))


def _mm_sel_r(x, sel, terms=3):
    return sum(jnp.dot(p, sel, preferred_element_type=F32) for p in _split(x, terms))


def _mm_hi(a, b):
    ah = a.astype(BF16)
    al = (a - ah.astype(F32)).astype(BF16)
    bh = b.astype(BF16)
    bl = (b - bh.astype(F32)).astype(BF16)
    d = lambda p, q: jnp.dot(p, q, preferred_element_type=F32)
    return d(ah, bh) + d(ah, bl) + d(al, bh)


def _sigmoid(x):
    return jax.nn.sigmoid(x)


def _silu(x):
    hx = 0.5 * x
    return hx + hx * jnp.tanh(hx)


def _softplus(x):
    return jnp.maximum(x, 0.0) + jnp.log(1.0 + jnp.exp(-jnp.abs(x)))


def _norm_mod(x, nw, scale, shift):
    y = x * lax.rsqrt(jnp.mean(x * x, axis=-1, keepdims=True) + RMS_EPS)
    return (y * nw) * (1.0 + scale) + shift


def _tril(n, strict=False):
    r = lax.broadcasted_iota(jnp.int32, (n, n), 0)
    c = lax.broadcasted_iota(jnp.int32, (n, n), 1)
    return (r > c) if strict else (r >= c)


def _resident(shape):
    nd = len(shape)
    return pl.BlockSpec(shape, lambda *_: (0,) * nd, pipeline_mode=pl.Buffered(1))


def _resident_layer(shape, layer):
    nd = len(shape)
    return pl.BlockSpec((None,) + tuple(shape), lambda *_: (layer,) + (0,) * nd, pipeline_mode=pl.Buffered(1))


def _rows(t, width):
    return pl.BlockSpec((None, t, width), lambda b, s: (b, s, 0))


def _mod_spec(d):
    return pl.BlockSpec((None, 6, d), lambda b, s: (b, 0, 0))


def _params(seq_axis_carries):
    sem = ("parallel", "arbitrary") if seq_axis_carries else ("parallel", "parallel")
    return pltpu.CompilerParams(dimension_semantics=sem, vmem_limit_bytes=V7X_VMEM_LIMIT)


def _head_indicator(d, hd):
    ind = (jnp.arange(d)[:, None] // hd == jnp.arange(LANE)[None, :]).astype(BF16)
    return ind, ind.T


def _ada_kernel(c_ref, w_ref, b_ref, o_ref):
    c = c_ref[...]
    o_ref[...] = _mm_hi(_silu(c), w_ref[...]) + b_ref[...]


def _ada(c, ada_w, ada_b):
    depth, d, n6 = ada_w.shape
    b = c.shape[0]
    tn = 1536
    out = pl.pallas_call(
        _ada_kernel,
        grid=(depth, n6 // tn),
        in_specs=[pl.BlockSpec((b, d), lambda l, j: (0, 0)),
                  pl.BlockSpec((None, d, tn), lambda l, j: (l, 0, j)),
                  pl.BlockSpec((None, 1, tn), lambda l, j: (l, 0, j))],
        out_specs=pl.BlockSpec((None, b, tn), lambda l, j: (l, 0, j)),
        out_shape=jax.ShapeDtypeStruct((depth, b, n6), F32),
        compiler_params=_params(False),
        name="ada",
    )(c, ada_w, ada_b.reshape(depth, 1, n6))
    return out.reshape(depth, b, 6, d)


def _ffn_kernel(x_ref, mod_ref, nw_ref, w1_ref, w3_ref, w2_ref, o_ref):
    x = x_ref[...]
    h = _norm_mod(x, nw_ref[...], mod_ref[4:5, :], mod_ref[3:4, :]).astype(BF16)
    a = jnp.dot(h, w1_ref[...], preferred_element_type=F32)
    b = jnp.dot(h, w3_ref[...], preferred_element_type=F32)
    g = (_silu(a) * b).astype(BF16)
    y = jnp.dot(g, w2_ref[...], preferred_element_type=F32)
    o_ref[...] = x + (1.0 + mod_ref[5:6, :]) * y


def _ffn(x, mod, nw, w1, w3, w2, layer):
    b, s, d = x.shape
    dff = w1.shape[2]
    t = FFN_ROW_TILE if s % FFN_ROW_TILE == 0 else ROW_TILE
    return pl.pallas_call(
        _ffn_kernel,
        grid=(b, s // t),
        in_specs=[_rows(t, d), _mod_spec(d), _resident((1, d)), _resident_layer((d, dff), layer),
                  _resident_layer((d, dff), layer), _resident_layer((dff, d), layer)],
        out_specs=_rows(t, d),
        out_shape=jax.ShapeDtypeStruct(x.shape, F32),
        compiler_params=_params(False),
        name="ffn",
    )(x, mod, nw.reshape(1, d), w1, w3, w2)


def _m_in_kernel(x_ref, mod_ref, nw_ref, w_ref, wd_ref, z_ref, xbc_ref, dt_ref):
    h = _norm_mod(x_ref[...], nw_ref[...], mod_ref[1:2, :], mod_ref[0:1, :]).astype(BF16)
    di, cdim = z_ref.shape[1], xbc_ref.shape[1]
    z_ref[...] = jnp.dot(h, w_ref[:, 0:di], preferred_element_type=F32)
    xbc_ref[...] = jnp.dot(h, w_ref[:, di:di + cdim], preferred_element_type=F32)
    dt_ref[...] = jnp.dot(h, wd_ref[...], preferred_element_type=F32)


def _m_ssd_kernel(xbc_ref, z_ref, dt_ref, x_ref, mod_ref, cw_ref, cb_ref, dtb_ref, alog_ref, e_ref,
                  dskip_ref, gnw_ref, ow_ref, o_ref, ubuf, st, ybuf, *, nh):
    t = x_ref.shape[0]
    di = z_ref.shape[1]
    ng, ns, gw = st.shape
    hpg = nh // ng
    hd = gw // hpg

    @pl.when(pl.program_id(1) == 0)
    def _():
        ubuf[0:8, :] = jnp.zeros((8, ubuf.shape[1]), F32)
        st[...] = jnp.zeros(st.shape, F32)

    u = xbc_ref[...]
    ubuf[8:8 + t, :] = u
    acc = cb_ref[...] + cw_ref[3:4, :] * u
    for k in range(3):
        acc = acc + cw_ref[k:k + 1, :] * ubuf[5 + k:5 + k + t, :]
    ubuf[0:8, :] = u[t - 8:t, :]
    xc = _silu(acc)
    xs = xc[:, :di]

    lane = lax.broadcasted_iota(jnp.int32, (1, LANE), 1)
    dt = _softplus(dt_ref[...] + dtb_ref[...])
    a_neg = jnp.where(lane < nh, -jnp.exp(alog_ref[...]), 0.0)
    a = dt * a_neg
    tril = _tril(t)
    cs = _mm_sel_l(jnp.where(tril, 1.0, 0.0).astype(BF16), a)
    cs_t = cs.T
    e = e_ref[...]
    dt_e = _mm_sel_r(dt, e, 2)
    cs_e = _mm_sel_r(cs, e, 2)
    cs_last = cs_e[t - 1:t, :]
    xd = xs * dt_e
    ecs = jnp.exp(cs_e)
    xdd = xd * jnp.exp(cs_last - cs_e)
    cdec = jnp.exp(cs_last)
    yskip = xs * dskip_ref[...]
    gate = _silu(z_ref[...])

    rr = lax.broadcasted_iota(jnp.int32, (hpg * t, gw), 0) // t
    ll = lax.broadcasted_iota(jnp.int32, (hpg * t, gw), 1) // hd
    bd = rr == ll
    groups = range(ng)
    sls = [slice(g * gw, (g + 1) * gw) for g in groups]
    bgs = [xc[:, di + g * ns:di + (g + 1) * ns].astype(BF16) for g in groups]
    cgs = [xc[:, di + (ng + g) * ns:di + (ng + g + 1) * ns].astype(BF16) for g in groups]
    scores = [_mm_nt(cgs[g], bgs[g]) for g in groups]
    sts = [st[g] for g in groups]
    yoff = [_mm(cgs[g], sts[g]) for g in groups]
    supd = [_mm_tn(bgs[g], xdd[:, sls[g]]) for g in groups]
    ydiag = []
    for g in groups:
        parts = []
        for j in range(hpg):
            h = g * hpg + j
            diff = cs[:, h:h + 1] - cs_t[h:h + 1, :]
            parts.append((scores[g] * jnp.exp(jnp.where(tril, diff, NEG))).astype(BF16))
        lhs = jnp.concatenate(parts, axis=1)
        rhs = jnp.where(bd, jnp.concatenate([xd[:, sls[g]]] * hpg, axis=0), 0.0).astype(BF16)
        ydiag.append(jnp.dot(lhs, rhs, preferred_element_type=F32))
    for g in groups:
        sl = sls[g]
        st[g] = sts[g] * cdec[:, sl] + supd[g]
        y = (ydiag[g] + yoff[g] * ecs[:, sl] + yskip[:, sl]) * gate[:, sl]
        y = y * lax.rsqrt(jnp.mean(y * y, axis=-1, keepdims=True) + RMS_EPS) * gnw_ref[:, sl]
        ybuf[:, sl] = y.astype(BF16)

    out = jnp.dot(ybuf[...], ow_ref[...], preferred_element_type=F32)
    o_ref[...] = x_ref[...] + (1.0 + mod_ref[2:3, :]) * out


def _mamba(x, mod, nw, in_w, dt_w, conv_w, conv_b, dt_bias, a_log, d_skip, norm_w, out_w, layer):
    b, s, d = x.shape
    nh = dt_bias.shape[0]
    di = nh * HEAD_DIM
    cdim = conv_w.shape[1]
    ng, ns = SSD_GROUPS, SSD_STATE
    gw = di // ng
    assert cdim == di + 2 * ng * ns and in_w.shape[2] == di + cdim + nh and nh <= LANE
    t = ROW_TILE
    z, xbc, dtr = pl.pallas_call(
        _m_in_kernel,
        grid=(b, s // t),
        in_specs=[_rows(t, d), _mod_spec(d), _resident((1, d)),
                  _resident_layer((d, di + cdim + nh), layer), _resident_layer((d, LANE), layer)],
        out_specs=[_rows(t, di), _rows(t, cdim), _rows(t, LANE)],
        out_shape=[jax.ShapeDtypeStruct((b, s, di), F32), jax.ShapeDtypeStruct((b, s, cdim), F32),
                   jax.ShapeDtypeStruct((b, s, LANE), F32)],
        compiler_params=_params(False),
        name="m_in",
    )(x, mod, nw.reshape(1, d), in_w, dt_w)

    pad1 = lambda v: jnp.pad(v, (0, LANE - nh)).reshape(1, LANE)
    expand = (jnp.arange(LANE)[:, None] == jnp.arange(di)[None, :] // HEAD_DIM).astype(BF16)
    tc = SSD_CHUNK
    return pl.pallas_call(
        functools.partial(_m_ssd_kernel, nh=nh),
        grid=(b, s // tc),
        in_specs=[_rows(tc, cdim), _rows(tc, di), _rows(tc, LANE), _rows(tc, d), _mod_spec(d),
                  _resident((4, cdim)), _resident((1, cdim)), _resident((1, LANE)), _resident((1, LANE)),
                  _resident((LANE, di)), _resident((1, di)), _resident((1, di)),
                  _resident_layer((di, d), layer)],
        out_specs=_rows(tc, d),
        out_shape=jax.ShapeDtypeStruct(x.shape, F32),
        scratch_shapes=[pltpu.VMEM((tc + 8, cdim), F32), pltpu.VMEM((ng, ns, gw), F32),
                        pltpu.VMEM((tc, di), BF16)],
        compiler_params=_params(True),
        name="m_ssd",
    )(xbc, z, dtr, x, mod, conv_w, conv_b.reshape(1, cdim), pad1(dt_bias), pad1(a_log), expand,
      jnp.repeat(d_skip, HEAD_DIM).reshape(1, di), norm_w.reshape(1, di), out_w)


def _r_in_kernel(x_ref, mod_ref, nw_ref, mix_ref, wrkv_ref, w1_ref, w2_ref, a1_ref, a2_ref,
                 g1_ref, g2_ref, vec_ref, ind_ref, indt_ref,
                 r_ref, lw_ref, k_ref, v_ref, kk_ref, bb_ref, bonus_ref, g_ref, hbuf, dxbuf):
    t = x_ref.shape[0]

    @pl.when(pl.program_id(1) == 0)
    def _():
        hbuf[0:8, :] = jnp.zeros((8, hbuf.shape[1]), F32)

    hbuf[8:8 + t, :] = _norm_mod(x_ref[...], nw_ref[...], mod_ref[1:2, :], mod_ref[0:1, :])
    dxbuf[...] = hbuf[7:7 + t, :] - hbuf[8:8 + t, :]
    hbuf[0:8, :] = hbuf[t:t + 8, :]
    w0, a0, k_k, k_a, r_k = (vec_ref[i:i + 1, :] for i in range(5))
    ind, indt = ind_ref[...], indt_ref[...]
    headsum = lambda q: _mm_sel_r(_mm_sel_r(q, ind, 2), indt, 2)
    dotf = lambda a, b_ref: jnp.dot(a, b_ref[...], preferred_element_type=F32)
    mixed = lambda n: (hbuf[8:8 + t, :] + dxbuf[...] * mix_ref[n:n + 1, :]).astype(BF16)

    r = dotf(mixed(0), wrkv_ref.at[0])
    k = dotf(mixed(1), wrkv_ref.at[1])
    v = dotf(mixed(2), wrkv_ref.at[2])
    wl = dotf(jnp.tanh(dotf(mixed(3), w1_ref)).astype(BF16), w2_ref)
    al = dotf(dotf(mixed(4), a1_ref).astype(BF16), a2_ref)
    r_ref[...] = r
    v_ref[...] = v
    g_ref[...] = dotf(_sigmoid(dotf(mixed(5), g1_ref)).astype(BF16), g2_ref)
    lw_ref[...] = -jnp.exp(-_softplus(-(w0 + wl)) - 0.5)
    a = _sigmoid(a0 + al)
    kk = k * k_k
    kk = kk * lax.rsqrt(jnp.maximum(headsum(kk * kk), KK_EPS * KK_EPS))
    k2 = k * (1.0 + (a - 1.0) * k_a)
    k_ref[...] = k2
    kk_ref[...] = kk
    bb_ref[...] = kk * a
    bonus_ref[...] = headsum(r * k2 * r_k) * v


def _r_wkv_kernel(r_ref, lw_ref, k_ref, v_ref, kk_ref, bb_ref, bonus_ref, g_ref, x_ref, mod_ref,
                  lnw_ref, lnb_ref, ind_ref, indt_ref, ow_ref, o_ref, state, ybuf, *, hd):
    nb, t, d = x_ref.shape
    gw = state.shape[1]
    ngr = d // gw
    hpg = gw // hd
    n = hpg * t

    @pl.when(pl.program_id(1) == 0)
    def _():
        state[...] = jnp.zeros(state.shape, F32)

    tri = jnp.where(_tril(t), 1.0, 0.0).astype(BF16)
    at, rt, kt, bt, v, ptot = [], [], [], [], [], []
    for bi in range(nb):
        lw = lw_ref[bi]
        cl = _mm_sel_l(tri, lw)
        ecl = jnp.exp(cl)
        encl = jnp.exp(-cl)
        rt.append(r_ref[bi] * ecl)
        at.append(-kk_ref[bi] * jnp.exp(cl - lw))
        kt.append(k_ref[bi] * encl)
        bt.append(bb_ref[bi] * encl)
        v.append(v_ref[bi])
        ptot.append(ecl[t - 1:t, :])

    ri = lax.broadcasted_iota(jnp.int32, (n, gw), 0)
    ci = lax.broadcasted_iota(jnp.int32, (n, gw), 1)
    headm = (ri // t) == (ci // hd)
    ri2 = lax.broadcasted_iota(jnp.int32, (n, n), 0)
    ci2 = lax.broadcasted_iota(jnp.int32, (n, n), 1)
    same = (ri2 // t) == (ci2 // t)
    tt = lax.broadcasted_iota(jnp.int32, (t, n), 0)
    ss = lax.broadcasted_iota(jnp.int32, (t, n), 1) % t
    strict = tt > ss
    incl = tt >= ss
    eye = jnp.where(tt == ss, 1.0, 0.0)
    vals_same = (lax.broadcasted_iota(jnp.int32, (gw, gw), 0) // hd) == (
        lax.broadcasted_iota(jnp.int32, (gw, gw), 1) // hd)

    def blockdiag(m, mask):
        return jnp.where(mask, jnp.concatenate([m] * hpg, axis=0), 0.0).astype(BF16)

    groups = range(nb * ngr)
    bis = [g // ngr for g in groups]
    sls = [slice((g % ngr) * gw, (g % ngr + 1) * gw) for g in groups]
    dot = lambda a, b: jnp.dot(a.astype(BF16), b, preferred_element_type=F32)
    ar = [jnp.concatenate([at[bi][:, sl], rt[bi][:, sl]], axis=0) for bi, sl in zip(bis, sls)]
    vst = [blockdiag(v[bi][:, sl], headm) for bi, sl in zip(bis, sls)]
    xbk = [jnp.concatenate([blockdiag(bt[bi][:, sl], headm), blockdiag(kt[bi][:, sl], headm)], axis=0)
           for bi, sl in zip(bis, sls)]
    gram = [_mm_nt(ar[g], xbk[g]) for g in groups]
    hs = [state[g] for g in groups]
    hproj = [_mm_nt(ar[g], hs[g]) for g in groups]
    a_ab = [jnp.where(strict, gram[g][0:t, 0:n], 0.0) for g in groups]
    rhs = [hproj[g][0:t] + dot(jnp.where(strict, gram[g][0:t, n:2 * n], 0.0), vst[g]) for g in groups]
    tinv = [eye + a_ab[g] for g in groups]
    p = [dot(a_ab[g], blockdiag(a_ab[g], same)) for g in groups]
    steps = max(t - 1, 1).bit_length() - 1
    for i in range(steps):
        last = i + 1 == steps
        for g in groups:
            tb = blockdiag(tinv[g], same)
            prod = dot(p[g], tb if last else jnp.concatenate([tb, blockdiag(p[g], same)], axis=1))
            tinv[g] = tinv[g] + prod[:, 0:n]
            if not last:
                p[g] = prod[:, n:2 * n]
    u = [dot(tinv[g], blockdiag(rhs[g], headm)) for g in groups]
    for g in groups:
        r_bk = jnp.where(jnp.concatenate([incl, incl], axis=1), gram[g][t:2 * t, :], 0.0)
        ybuf[bis[g] * t:(bis[g] + 1) * t, sls[g]] = hproj[g][t:2 * t] + dot(
            r_bk, jnp.concatenate([blockdiag(u[g], headm), vst[g]], axis=0))
    for g in groups:
        bi, sl = bis[g], sls[g]
        upd = _mm_tn(jnp.concatenate([u[g], v[bi][:, sl]], axis=0),
                     jnp.concatenate([bt[bi][:, sl], kt[bi][:, sl]], axis=0))
        state[g] = (hs[g] + jnp.where(vals_same, upd, 0.0)) * ptot[bi][:, sl]

    y = ybuf[...]
    ind, indt = ind_ref[...], indt_ref[...]
    headmean = lambda q: _mm_sel_r(_mm_sel_r(q, ind, 2), indt, 2) * (1.0 / hd)
    mu = headmean(y)
    yc = y - mu
    var = headmean(yc * yc)
    yn = yc * lax.rsqrt(var + GN_EPS) * lnw_ref[...] + lnb_ref[...]
    yo = (yn + bonus_ref[...].reshape(nb * t, d)) * g_ref[...].reshape(nb * t, d)
    out = _mm(yo, ow_ref[...])
    for bi in range(nb):
        o_ref[bi] = x_ref[bi] + (1.0 + mod_ref[bi, 2:3, :]) * out[bi * t:(bi + 1) * t]


def _rwkv(x, mod, nw, mix, w_rkv, w0, w1, w2, a0, a1, a2, g1, g2, k_k, k_a, r_k, lnx_w, lnx_b, out_w):
    b, s, d = x.shape
    hd = HEAD_DIM
    assert d % WKV_GROUP == 0 and d // hd <= LANE

    def pad_pair(p1, p2):
        r = p1.shape[1]
        rp = -(-r // LANE) * LANE
        return (jnp.pad(p1, ((0, 0), (0, rp - r))).astype(BF16), jnp.pad(p2, ((0, rp - r), (0, 0))).astype(BF16))

    w1p, w2p = pad_pair(w1, w2)
    a1p, a2p = pad_pair(a1, a2)
    g1p, g2p = pad_pair(g1, g2)
    vecs = jnp.stack([w0, a0, k_k, k_a, r_k.reshape(d), jnp.zeros_like(w0), jnp.zeros_like(w0), jnp.zeros_like(w0)])
    ind, indt = _head_indicator(d, hd)
    wb = w_rkv.astype(BF16)
    t = ROW_TILE
    res = lambda arr: _resident(arr.shape)
    outs = pl.pallas_call(
        _r_in_kernel,
        grid=(b, s // t),
        in_specs=[_rows(t, d), _mod_spec(d), _resident((1, d)), _resident((6, d)), _resident((3, d, d)),
                  res(w1p), res(w2p), res(a1p), res(a2p), res(g1p), res(g2p),
                  _resident((8, d)), res(ind), res(indt)],
        out_specs=[_rows(t, d)] * 8,
        out_shape=[jax.ShapeDtypeStruct((b, s, d), F32)] * 8,
        scratch_shapes=[pltpu.VMEM((t + 8, d), F32), pltpu.VMEM((t, d), F32)],
        compiler_params=_params(True),
        name="r_in",
    )(x, mod, nw.reshape(1, d), mix, wb, w1p, w2p, a1p, a2p, g1p, g2p, vecs, ind, indt)

    tc = WKV_CHUNK
    nb = WKV_BATCH_ROWS if b % WKV_BATCH_ROWS == 0 else 1
    rows = pl.BlockSpec((nb, tc, d), lambda bi, si: (bi, si, 0))
    return pl.pallas_call(
        functools.partial(_r_wkv_kernel, hd=hd),
        grid=(b // nb, s // tc),
        in_specs=[rows] * 9 + [pl.BlockSpec((nb, 6, d), lambda bi, si: (bi, 0, 0)), _resident((1, d)),
                               _resident((1, d)), res(ind), res(indt), _resident((d, d))],
        out_specs=rows,
        out_shape=jax.ShapeDtypeStruct(x.shape, F32),
        scratch_shapes=[pltpu.VMEM((nb * d // WKV_GROUP, WKV_GROUP, WKV_GROUP), F32),
                        pltpu.VMEM((nb * tc, d), F32)],
        compiler_params=_params(True),
        name="r_wkv",
    )(*outs, x, mod, lnx_w.reshape(1, d), lnx_b.reshape(1, d), ind, indt, out_w.astype(BF16))


def _f_in_kernel(x_ref, mod_ref, nw_ref, w_ref, wvt_ref, wf_ref, vec_ref, fb_ref, ind_ref, indt_ref,
                 place_ref, ones_ref, q_ref, k_ref, vt_ref, qa_ref, ka_ref, carry, *, hd):
    t = x_ref.shape[0]

    @pl.when(pl.program_id(1) == 0)
    def _():
        carry[...] = jnp.zeros(carry.shape, F32)

    h = _norm_mod(x_ref[...], nw_ref[...], mod_ref[1:2, :], mod_ref[0:1, :]).astype(BF16)
    ind, indt = ind_ref[...], indt_ref[...]

    def head_rms(q, w):
        ms = _mm_sel_r(_mm_sel_r(q * q, ind, 2), indt, 2) * (1.0 / hd)
        return q * lax.rsqrt(ms + RMS_EPS) * w

    d = x_ref.shape[1]
    q = head_rms(jnp.dot(h, w_ref[:, 0:d], preferred_element_type=F32), vec_ref[0:1, :])
    k = head_rms(jnp.dot(h, w_ref[:, d:2 * d], preferred_element_type=F32), vec_ref[1:2, :])
    q_ref[...] = (q * (hd ** -0.5 * LOG2E)).astype(BF16)
    k_ref[...] = k.astype(BF16)
    vt_ref[...] = _mm_nt(wvt_ref[...], h).astype(BF16)
    f = jnp.dot(h, wf_ref[...], preferred_element_type=F32) + fb_ref[...]
    logf = -_softplus(-f)
    cum = _mm_sel_l(jnp.where(_tril(t), 1.0, 0.0).astype(BF16), logf) + carry[0:1, :]
    carry[...] = jnp.broadcast_to(cum[t - 1:t, :], carry.shape)
    pieces = _split(cum * LOG2E, 3)
    place = lambda first: sum(jnp.dot(p, place_ref[first + i], preferred_element_type=F32)
                              for i, p in enumerate(pieces))
    qa_ref[...] = (place(0) + ones_ref[0:1, :]).astype(BF16)
    ka_ref[...] = (ones_ref[1:2, :] - place(3)).astype(BF16)


def _f_att_kernel(q_ref, k_ref, vt_ref, qa_ref, ka_ref, x_ref, mod_ref, ow_ref, o_ref,
                  obuf, qm_scr, m_scr, acc_scr, *, hd, pairs_per_loop):
    tq = x_ref.shape[0]
    d = x_ref.shape[1]
    tk = vt_ref.shape[2]
    qi = pl.program_id(1)
    lane_l = lax.broadcasted_iota(jnp.int32, (1, LANE), 1)
    first_l = lane_l < hd
    first_r = lax.broadcasted_iota(jnp.int32, (LANE, 1), 0) < hd
    on_or_below = (lax.broadcasted_iota(jnp.int32, (tk, tq), 0)
                   <= lax.broadcasted_iota(jnp.int32, (tk, tq), 1))
    zero_b = jnp.zeros((), BF16)
    one_b = jnp.ones((), BF16)

    lanes = lambda p: slice(p * LANE, (p + 1) * LANE)
    for p0 in range(0, d // LANE, pairs_per_loop):
        pairs = range(p0, p0 + pairs_per_loop)
        heads = [2 * p + i for p in pairs for i in range(2)]
        qa = qa_ref[...]
        for p in pairs:
            qp = q_ref[:, lanes(p)]
            for i, qh in enumerate((jnp.where(first_l, qp, zero_b), jnp.where(first_l, zero_b, qp))):
                h = 2 * p + i
                own = (lane_l >= BIAS_COLS * h) & (lane_l < BIAS_COLS * (h + 1))
                qm_scr[h] = jnp.concatenate([qh, jnp.where(own, qa, zero_b)], axis=1)
        for h in heads:
            m_scr[h] = jnp.full((1, tq), NEG, F32)
            acc_scr[h] = jnp.zeros((LANE, tq), F32)

        def step(j, diagonal, pairs=pairs, heads=heads):
            start = pl.multiple_of(j * tk, tk)
            sc = {}
            kaj = ka_ref[pl.ds(start, tk), :]
            for p in pairs:
                kj = jnp.concatenate([k_ref[pl.ds(start, tk), lanes(p)], kaj], axis=1)
                for h in (2 * p, 2 * p + 1):
                    sc[h] = lax.dot_general(kj, qm_scr[h], (((1,), (1,)), ((), ())), preferred_element_type=F32)
            if diagonal:
                sc = {h: jnp.where(on_or_below, sc[h], NEG) for h in heads}
            m_old = {h: m_scr[h] for h in heads}
            m_new = {h: jnp.maximum(m_old[h], jnp.max(sc[h], axis=0, keepdims=True)) for h in heads}
            pr = {h: jnp.exp2(sc[h] - m_new[h]).astype(BF16) for h in heads}
            for p in pairs:
                vtj = vt_ref[j, lanes(p), :]
                vt2 = (jnp.where(first_r, vtj, one_b), jnp.where(first_r, one_b, vtj))
                for i in range(2):
                    h = 2 * p + i
                    acc_scr[h] = (jnp.exp2(m_old[h] - m_new[h]) * acc_scr[h]
                                  + jnp.dot(vt2[i], pr[h], preferred_element_type=F32))
                    m_scr[h] = m_new[h]

        def body(j, c):
            step(j, False)
            return c

        lax.fori_loop(0, qi, body, 0)
        step(qi, True)
        for p in pairs:
            acc0, acc1 = acc_scr[2 * p], acc_scr[2 * p + 1]
            o_t = jnp.where(first_r, acc0 / acc0[hd:hd + 1, :], acc1 / acc1[0:1, :])
            obuf[:, lanes(p)] = o_t.T.astype(BF16)

    out = jnp.dot(obuf[...], ow_ref[...], preferred_element_type=F32)
    o_ref[...] = x_ref[...] + (1.0 + mod_ref[2:3, :]) * out


def _fox(x, mod, nw, qkvf_w, fgate_b, q_norm_w, k_norm_w, out_w):
    b, s, d = x.shape
    hd = HEAD_DIM
    nh = d // hd
    assert qkvf_w.shape[1] == 3 * d + nh and nh <= LANE and 2 * hd == LANE
    w_all = qkvf_w.astype(BF16)
    wf = jnp.pad(qkvf_w[:, 3 * d:], ((0, 0), (0, LANE - nh))).astype(BF16)
    fb = jnp.pad(fgate_b, (0, LANE - nh)).reshape(1, LANE)
    z = jnp.zeros((d,), F32)
    vecs = jnp.stack([jnp.tile(q_norm_w, nh), jnp.tile(k_norm_w, nh), z, z, z, z, z, z])
    ind, indt = _head_indicator(d, hd)
    t = ROW_TILE
    res = lambda arr: _resident(arr.shape)
    nt = s // t
    assert BIAS_COLS * nh <= LANE
    col = jnp.arange(LANE)[None, None, :]
    place = (col == BIAS_COLS * jnp.arange(LANE)[None, :, None] + jnp.arange(BIAS_COLS)[:, None, None])
    place = (place & (jnp.arange(LANE)[None, :, None] < nh)).astype(BF16)
    third = (jnp.arange(LANE) % BIAS_COLS) // 3
    used = jnp.arange(LANE) < BIAS_COLS * nh
    z1 = jnp.zeros((LANE,), F32)
    ones = jnp.stack([(used & (third == 1)).astype(F32), (used & (third == 0)).astype(F32), z1, z1, z1, z1, z1, z1])
    q, k, vt, qa, ka = pl.pallas_call(
        functools.partial(_f_in_kernel, hd=hd),
        grid=(b, nt),
        in_specs=[_rows(t, d), _mod_spec(d), _resident((1, d)), res(w_all),
                  _resident((d, d)), res(wf), _resident((8, d)), _resident((1, LANE)),
                  res(ind), res(indt), res(place), res(ones)],
        out_specs=[_rows(t, d), _rows(t, d), pl.BlockSpec((None, None, d, t), lambda bi, si: (bi, si, 0, 0)),
                   _rows(t, LANE), _rows(t, LANE)],
        out_shape=[jax.ShapeDtypeStruct((b, s, d), BF16), jax.ShapeDtypeStruct((b, s, d), BF16),
                   jax.ShapeDtypeStruct((b, nt, d, t), BF16), jax.ShapeDtypeStruct((b, s, LANE), BF16),
                   jax.ShapeDtypeStruct((b, s, LANE), BF16)],
        scratch_shapes=[pltpu.VMEM((8, LANE), F32)],
        compiler_params=_params(True),
        name="f_in",
    )(x, mod, nw.reshape(1, d), w_all, w_all[:, 2 * d:3 * d].T, wf, vecs, fb, ind, indt, place, ones)

    whole = lambda *shape: pl.BlockSpec((None,) + shape, lambda bi, si: (bi,) + (0,) * len(shape))
    return pl.pallas_call(
        functools.partial(_f_att_kernel, hd=hd, pairs_per_loop=ATT_PAIRS_PER_LOOP),
        grid=(b, nt),
        in_specs=[_rows(t, d), whole(s, d), whole(nt, d, t), _rows(t, LANE), whole(s, LANE),
                  _rows(t, d), _mod_spec(d), _resident((d, d))],
        out_specs=_rows(t, d),
        out_shape=jax.ShapeDtypeStruct(x.shape, F32),
        scratch_shapes=[pltpu.VMEM((t, d), BF16), pltpu.VMEM((nh, t, 2 * LANE), BF16),
                        pltpu.VMEM((nh, 1, t), F32), pltpu.VMEM((nh, LANE, t), F32)],
        compiler_params=_params(False),
        name="f_att",
    )(q, k, vt, qa, ka, x, mod, out_w.astype(BF16))


def kernel(x, c, ada_w, ada_b, norm1_w, norm2_w, ffn_w1, ffn_w3, ffn_w2, m_in_w, m_conv_w, m_conv_b, m_dt_bias, m_A_log, m_D, m_norm_w, m_out_w, r_mix, r_w_rkv, r_w0, r_w1, r_w2, r_a0, r_a1, r_a2, r_g1, r_g2, r_k_k, r_k_a, r_r_k, r_lnx_w, r_lnx_b, r_out_w, f_qkvf_w, f_fgate_b, f_q_norm_w, f_k_norm_w, f_out_w):
    depth = ada_w.shape[0]
    mod = _ada(c, ada_w, ada_b)
    w1b, w3b, w2b = ffn_w1.astype(BF16), ffn_w3.astype(BF16), ffn_w2.astype(BF16)
    m_nh = m_dt_bias.shape[1]
    m_in_b = m_in_w.astype(BF16)
    m_dt_b = jnp.pad(m_in_w[:, :, m_in_w.shape[2] - m_nh:], ((0, 0), (0, 0), (0, LANE - m_nh))).astype(BF16)
    m_out_b = m_out_w.astype(BF16)
    ia = ib = ic = 0
    for i in range(depth):
        kind = i % 3
        if kind == 0:
            x = _mamba(x, mod[i], norm1_w[i], m_in_b, m_dt_b, m_conv_w[ia], m_conv_b[ia], m_dt_bias[ia],
                       m_A_log[ia], m_D[ia], m_norm_w[ia], m_out_b, ia)
            ia += 1
        elif kind == 1:
            x = _rwkv(x, mod[i], norm1_w[i], r_mix[ib], r_w_rkv[ib], r_w0[ib], r_w1[ib], r_w2[ib], r_a0[ib],
                      r_a1[ib], r_a2[ib], r_g1[ib], r_g2[ib], r_k_k[ib], r_k_a[ib], r_r_k[ib],
                      r_lnx_w[ib], r_lnx_b[ib], r_out_w[ib])
            ib += 1
        else:
            x = _fox(x, mod[i], norm1_w[i], f_qkvf_w[ic], f_fgate_b[ic], f_q_norm_w[ic], f_k_norm_w[ic],
                     f_out_w[ic])
            ic += 1
        x = _ffn(x, mod[i], norm2_w[i], w1b, w3b, w2b, i)
    return x
```

```python
import functools

import jax
import jax.numpy as jnp
from jax import lax
from jax.experimental import pallas as pl
from jax.experimental.pallas import tpu as pltpu

F32 = jnp.float32
BF16 = jnp.bfloat16

RMS_EPS = 1e-6
GN_EPS = 64e-5
KK_EPS = 1e-12
NEG = -1e30
LOG2E = 1.4426950408889634

V7X_VMEM_LIMIT = 56 * 1024 * 1024

HEAD_DIM = 64
SSD_GROUPS = 8
SSD_STATE = 128
SSD_CHUNK = 128
WKV_CHUNK = 64
WKV_GROUP = 128
WKV_BATCH_ROWS = 4
ROW_TILE = 256
FFN_ROW_TILE = 512
ATT_PAIRS_PER_LOOP = 8
BIAS_COLS = 6
LANE = 128


def _mm(a, b):
    return jnp.dot(a.astype(BF16), b.astype(BF16), preferred_element_type=F32)


def _mm_nt(a, b):
    return lax.dot_general(a.astype(BF16), b.astype(BF16), (((1,), (1,)), ((), ())),
                           preferred_element_type=F32)


def _mm_tn(a, b):
    return jnp.dot(a.T.astype(BF16), b.astype(BF16), preferred_element_type=F32)


def _split(x, terms):
    parts = []
    for i in range(terms):
        p = x.astype(BF16)
        parts.append(p)
        if i + 1 < terms:
            x = x - p.astype(F32)
    return parts


def _mm_sel_l(sel, x, terms=3):
    return sum(jnp.dot(sel, p, preferred_element_type=F32) for p in _split(x, terms))


def _mm_sel_r(x, sel, terms=3):
    return sum(jnp.dot(p, sel, preferred_element_type=F32) for p in _split(x, terms))


def _mm_hi(a, b):
    ah = a.astype(BF16)
    al = (a - ah.astype(F32)).astype(BF16)
    bh = b.astype(BF16)
    bl = (b - bh.astype(F32)).astype(BF16)
    d = lambda p, q: jnp.dot(p, q, preferred_element_type=F32)
    return d(ah, bh) + d(ah, bl) + d(al, bh)


def _sigmoid(x):
    return jax.nn.sigmoid(x)


def _silu(x):
    hx = 0.5 * x
    return hx + hx * jnp.tanh(hx)


def _softplus(x):
    return jnp.maximum(x, 0.0) + jnp.log(1.0 + jnp.exp(-jnp.abs(x)))


def _norm_mod(x, nw, scale, shift):
    y = x * lax.rsqrt(jnp.mean(x * x, axis=-1, keepdims=True) + RMS_EPS)
    return (y * nw) * (1.0 + scale) + shift


def _tril(n, strict=False):
    r = lax.broadcasted_iota(jnp.int32, (n, n), 0)
    c = lax.broadcasted_iota(jnp.int32, (n, n), 1)
    return (r > c) if strict else (r >= c)


def _resident(shape):
    nd = len(shape)
    return pl.BlockSpec(shape, lambda *_: (0,) * nd, pipeline_mode=pl.Buffered(1))


def _resident_layer(shape, layer):
    nd = len(shape)
    return pl.BlockSpec((None,) + tuple(shape), lambda *_: (layer,) + (0,) * nd, pipeline_mode=pl.Buffered(1))


def _rows(t, width):
    return pl.BlockSpec((None, t, width), lambda b, s: (b, s, 0))


def _mod_spec(d):
    return pl.BlockSpec((None, 6, d), lambda b, s: (b, 0, 0))


def _params(seq_axis_carries):
    sem = ("parallel", "arbitrary") if seq_axis_carries else ("parallel", "parallel")
    return pltpu.CompilerParams(dimension_semantics=sem, vmem_limit_bytes=V7X_VMEM_LIMIT)


def _head_indicator(d, hd):
    ind = (jnp.arange(d)[:, None] // hd == jnp.arange(LANE)[None, :]).astype(BF16)
    return ind, ind.T


def _ada_kernel(c_ref, w_ref, b_ref, o_ref):
    c = c_ref[...]
    o_ref[...] = _mm_hi(_silu(c), w_ref[...]) + b_ref[...]


def _ada(c, ada_w, ada_b):
    depth, d, n6 = ada_w.shape
    b = c.shape[0]
    tn = 1536
    out = pl.pallas_call(
        _ada_kernel,
        grid=(depth, n6 // tn),
        in_specs=[pl.BlockSpec((b, d), lambda l, j: (0, 0)),
                  pl.BlockSpec((None, d, tn), lambda l, j: (l, 0, j)),
                  pl.BlockSpec((None, 1, tn), lambda l, j: (l, 0, j))],
        out_specs=pl.BlockSpec((None, b, tn), lambda l, j: (l, 0, j)),
        out_shape=jax.ShapeDtypeStruct((depth, b, n6), F32),
        compiler_params=_params(False),
        name="ada",
    )(c, ada_w, ada_b.reshape(depth, 1, n6))
    return out.reshape(depth, b, 6, d)


def _ffn_kernel(x_ref, mod_ref, nw_ref, w1_ref, w3_ref, w2_ref, o_ref):
    x = x_ref[...]
    h = _norm_mod(x, nw_ref[...], mod_ref[4:5, :], mod_ref[3:4, :]).astype(BF16)
    a = jnp.dot(h, w1_ref[...], preferred_element_type=F32)
    b = jnp.dot(h, w3_ref[...], preferred_element_type=F32)
    g = (_silu(a) * b).astype(BF16)
    y = jnp.dot(g, w2_ref[...], preferred_element_type=F32)
    o_ref[...] = x + (1.0 + mod_ref[5:6, :]) * y


def _ffn(x, mod, nw, w1, w3, w2, layer):
    b, s, d = x.shape
    dff = w1.shape[2]
    t = FFN_ROW_TILE if s % FFN_ROW_TILE == 0 else ROW_TILE
    return pl.pallas_call(
        _ffn_kernel,
        grid=(b, s // t),
        in_specs=[_rows(t, d), _mod_spec(d), _resident((1, d)), _resident_layer((d, dff), layer),
                  _resident_layer((d, dff), layer), _resident_layer((dff, d), layer)],
        out_specs=_rows(t, d),
        out_shape=jax.ShapeDtypeStruct(x.shape, F32),
        compiler_params=_params(False),
        name="ffn",
    )(x, mod, nw.reshape(1, d), w1, w3, w2)


def _m_in_kernel(x_ref, mod_ref, nw_ref, w_ref, wd_ref, z_ref, xbc_ref, dt_ref):
    h = _norm_mod(x_ref[...], nw_ref[...], mod_ref[1:2, :], mod_ref[0:1, :]).astype(BF16)
    di, cdim = z_ref.shape[1], xbc_ref.shape[1]
    z_ref[...] = jnp.dot(h, w_ref[:, 0:di], preferred_element_type=F32)
    xbc_ref[...] = jnp.dot(h, w_ref[:, di:di + cdim], preferred_element_type=F32)
    dt_ref[...] = jnp.dot(h, wd_ref[...], preferred_element_type=F32)


def _m_ssd_kernel(xbc_ref, z_ref, dt_ref, x_ref, mod_ref, cw_ref, cb_ref, dtb_ref, alog_ref, e_ref,
                  dskip_ref, gnw_ref, ow_ref, o_ref, ubuf, st, ybuf, *, nh):
    t = x_ref.shape[0]
    di = z_ref.shape[1]
    ng, ns, gw = st.shape
    hpg = nh // ng
    hd = gw // hpg

    @pl.when(pl.program_id(1) == 0)
    def _():
        ubuf[0:8, :] = jnp.zeros((8, ubuf.shape[1]), F32)
        st[...] = jnp.zeros(st.shape, F32)

    u = xbc_ref[...]
    ubuf[8:8 + t, :] = u
    acc = cb_ref[...] + cw_ref[3:4, :] * u
    for k in range(3):
        acc = acc + cw_ref[k:k + 1, :] * ubuf[5 + k:5 + k + t, :]
    ubuf[0:8, :] = u[t - 8:t, :]
    xc = _silu(acc)
    xs = xc[:, :di]

    lane = lax.broadcasted_iota(jnp.int32, (1, LANE), 1)
    dt = _softplus(dt_ref[...] + dtb_ref[...])
    a_neg = jnp.where(lane < nh, -jnp.exp(alog_ref[...]), 0.0)
    a = dt * a_neg
    tril = _tril(t)
    cs = _mm_sel_l(jnp.where(tril, 1.0, 0.0).astype(BF16), a)
    cs_t = cs.T
    e = e_ref[...]
    dt_e = _mm_sel_r(dt, e, 2)
    cs_e = _mm_sel_r(cs, e, 2)
    cs_last = cs_e[t - 1:t, :]
    xd = xs * dt_e
    ecs = jnp.exp(cs_e)
    xdd = xd * jnp.exp(cs_last - cs_e)
    cdec = jnp.exp(cs_last)
    yskip = xs * dskip_ref[...]
    gate = _silu(z_ref[...])

    rr = lax.broadcasted_iota(jnp.int32, (hpg * t, gw), 0) // t
    ll = lax.broadcasted_iota(jnp.int32, (hpg * t, gw), 1) // hd
    bd = rr == ll
    groups = range(ng)
    sls = [slice(g * gw, (g + 1) * gw) for g in groups]
    bgs = [xc[:, di + g * ns:di + (g + 1) * ns].astype(BF16) for g in groups]
    cgs = [xc[:, di + (ng + g) * ns:di + (ng + g + 1) * ns].astype(BF16) for g in groups]
    scores = [_mm_nt(cgs[g], bgs[g]) for g in groups]
    sts = [st[g] for g in groups]
    yoff = [_mm(cgs[g], sts[g]) for g in groups]
    supd = [_mm_tn(bgs[g], xdd[:, sls[g]]) for g in groups]
    ydiag = []
    for g in groups:
        parts = []
        for j in range(hpg):
            h = g * hpg + j
            diff = cs[:, h:h + 1] - cs_t[h:h + 1, :]
            parts.append((scores[g] * jnp.exp(jnp.where(tril, diff, NEG))).astype(BF16))
        lhs = jnp.concatenate(parts, axis=1)
        rhs = jnp.where(bd, jnp.concatenate([xd[:, sls[g]]] * hpg, axis=0), 0.0).astype(BF16)
        ydiag.append(jnp.dot(lhs, rhs, preferred_element_type=F32))
    for g in groups:
        sl = sls[g]
        st[g] = sts[g] * cdec[:, sl] + supd[g]
        y = (ydiag[g] + yoff[g] * ecs[:, sl] + yskip[:, sl]) * gate[:, sl]
        y = y * lax.rsqrt(jnp.mean(y * y, axis=-1, keepdims=True) + RMS_EPS) * gnw_ref[:, sl]
        ybuf[:, sl] = y.astype(BF16)

    out = jnp.dot(ybuf[...], ow_ref[...], preferred_element_type=F32)
    o_ref[...] = x_ref[...] + (1.0 + mod_ref[2:3, :]) * out


def _mamba(x, mod, nw, in_w, dt_w, conv_w, conv_b, dt_bias, a_log, d_skip, norm_w, out_w, layer):
    b, s, d = x.shape
    nh = dt_bias.shape[0]
    di = nh * HEAD_DIM
    cdim = conv_w.shape[1]
    ng, ns = SSD_GROUPS, SSD_STATE
    gw = di // ng
    assert cdim == di + 2 * ng * ns and in_w.shape[2] == di + cdim and nh <= LANE
    t = FFN_ROW_TILE if s % FFN_ROW_TILE == 0 else ROW_TILE
    z, xbc, dtr = pl.pallas_call(
        _m_in_kernel,
        grid=(b, s // t),
        in_specs=[_rows(t, d), _mod_spec(d), _resident((1, d)),
                  _resident_layer((d, di + cdim), layer), _resident_layer((d, LANE), layer)],
        out_specs=[_rows(t, di), _rows(t, cdim), _rows(t, LANE)],
        out_shape=[jax.ShapeDtypeStruct((b, s, di), F32), jax.ShapeDtypeStruct((b, s, cdim), F32),
                   jax.ShapeDtypeStruct((b, s, LANE), F32)],
        compiler_params=_params(False),
        name="m_in",
    )(x, mod, nw.reshape(1, d), in_w, dt_w)

    pad1 = lambda v: jnp.pad(v, (0, LANE - nh)).reshape(1, LANE)
    expand = (jnp.arange(LANE)[:, None] == jnp.arange(di)[None, :] // HEAD_DIM).astype(BF16)
    tc = SSD_CHUNK
    return pl.pallas_call(
        functools.partial(_m_ssd_kernel, nh=nh),
        grid=(b, s // tc),
        in_specs=[_rows(tc, cdim), _rows(tc, di), _rows(tc, LANE), _rows(tc, d), _mod_spec(d),
                  _resident((4, cdim)), _resident((1, cdim)), _resident((1, LANE)), _resident((1, LANE)),
                  _resident((LANE, di)), _resident((1, di)), _resident((1, di)),
                  _resident_layer((di, d), layer)],
        out_specs=_rows(tc, d),
        out_shape=jax.ShapeDtypeStruct(x.shape, F32),
        scratch_shapes=[pltpu.VMEM((tc + 8, cdim), F32), pltpu.VMEM((ng, ns, gw), F32),
                        pltpu.VMEM((tc, di), BF16)],
        compiler_params=_params(True),
        name="m_ssd",
    )(xbc, z, dtr, x, mod, conv_w, conv_b.reshape(1, cdim), pad1(dt_bias), pad1(a_log), expand,
      jnp.repeat(d_skip, HEAD_DIM).reshape(1, di), norm_w.reshape(1, di), out_w)


def _r_in_kernel(x_ref, mod_ref, nw_ref, mix_ref, wrkv_ref, w1_ref, w2_ref, a1_ref, a2_ref,
                 g1_ref, g2_ref, vec_ref, ind_ref, indt_ref,
                 r_ref, lw_ref, k_ref, v_ref, kk_ref, bb_ref, bonus_ref, g_ref, hbuf, dxbuf):
    t = x_ref.shape[0]

    @pl.when(pl.program_id(1) == 0)
    def _():
        hbuf[0:8, :] = jnp.zeros((8, hbuf.shape[1]), F32)

    hbuf[8:8 + t, :] = _norm_mod(x_ref[...], nw_ref[...], mod_ref[1:2, :], mod_ref[0:1, :])
    dxbuf[...] = hbuf[7:7 + t, :] - hbuf[8:8 + t, :]
    hbuf[0:8, :] = hbuf[t:t + 8, :]
    w0, a0, k_k, k_a, r_k = (vec_ref[i:i + 1, :] for i in range(5))
    ind, indt = ind_ref[...], indt_ref[...]
    headsum = lambda q: _mm_sel_r(_mm_sel_r(q, ind, 2), indt, 2)
    dotf = lambda a, b_ref: jnp.dot(a, b_ref[...], preferred_element_type=F32)
    mixed = lambda n: (hbuf[8:8 + t, :] + dxbuf[...] * mix_ref[n:n + 1, :]).astype(BF16)

    r = dotf(mixed(0), wrkv_ref.at[0])
    k = dotf(mixed(1), wrkv_ref.at[1])
    v = dotf(mixed(2), wrkv_ref.at[2])
    wl = dotf(jnp.tanh(dotf(mixed(3), w1_ref)).astype(BF16), w2_ref)
    al = dotf(dotf(mixed(4), a1_ref).astype(BF16), a2_ref)
    r_ref[...] = r
    v_ref[...] = v
    g_ref[...] = dotf(_sigmoid(dotf(mixed(5), g1_ref)).astype(BF16), g2_ref)
    lw_ref[...] = -jnp.exp(-_softplus(-(w0 + wl)) - 0.5)
    a = _sigmoid(a0 + al)
    kk = k * k_k
    kk = kk * lax.rsqrt(jnp.maximum(headsum(kk * kk), KK_EPS * KK_EPS))
    k2 = k * (1.0 + (a - 1.0) * k_a)
    k_ref[...] = k2
    kk_ref[...] = kk
    bb_ref[...] = kk * a
    bonus_ref[...] = headsum(r * k2 * r_k) * v


def _r_wkv_kernel(r_ref, lw_ref, k_ref, v_ref, kk_ref, bb_ref, bonus_ref, g_ref, x_ref, mod_ref,
                  lnw_ref, lnb_ref, ind_ref, indt_ref, ow_ref, o_ref, state, ybuf, *, hd):
    nb, t, d = x_ref.shape
    gw = state.shape[1]
    ngr = d // gw
    hpg = gw // hd
    n = hpg * t

    @pl.when(pl.program_id(1) == 0)
    def _():
        state[...] = jnp.zeros(state.shape, F32)

    tri = jnp.where(_tril(t), 1.0, 0.0).astype(BF16)
    at, rt, kt, bt, v, ptot = [], [], [], [], [], []
    for bi in range(nb):
        lw = lw_ref[bi]
        cl = _mm_sel_l(tri, lw)
        ecl = jnp.exp(cl)
        encl = jnp.exp(-cl)
        rt.append(r_ref[bi] * ecl)
        at.append(-kk_ref[bi] * jnp.exp(cl - lw))
        kt.append(k_ref[bi] * encl)
        bt.append(bb_ref[bi] * encl)
        v.append(v_ref[bi])
        ptot.append(ecl[t - 1:t, :])

    ri = lax.broadcasted_iota(jnp.int32, (n, gw), 0)
    ci = lax.broadcasted_iota(jnp.int32, (n, gw), 1)
    headm = (ri // t) == (ci // hd)
    ri2 = lax.broadcasted_iota(jnp.int32, (n, n), 0)
    ci2 = lax.broadcasted_iota(jnp.int32, (n, n), 1)
    same = (ri2 // t) == (ci2 // t)
    tt = lax.broadcasted_iota(jnp.int32, (t, n), 0)
    ss = lax.broadcasted_iota(jnp.int32, (t, n), 1) % t
    strict = tt > ss
    incl = tt >= ss
    eye = jnp.where(tt == ss, 1.0, 0.0)
    vals_same = (lax.broadcasted_iota(jnp.int32, (gw, gw), 0) // hd) == (
        lax.broadcasted_iota(jnp.int32, (gw, gw), 1) // hd)

    def blockdiag(m, mask):
        return jnp.where(mask, jnp.concatenate([m] * hpg, axis=0), 0.0).astype(BF16)

    groups = range(nb * ngr)
    bis = [g // ngr for g in groups]
    sls = [slice((g % ngr) * gw, (g % ngr + 1) * gw) for g in groups]
    dot = lambda a, b: jnp.dot(a.astype(BF16), b, preferred_element_type=F32)
    ar = [jnp.concatenate([at[bi][:, sl], rt[bi][:, sl]], axis=0) for bi, sl in zip(bis, sls)]
    vst = [blockdiag(v[bi][:, sl], headm) for bi, sl in zip(bis, sls)]
    xbk = [jnp.concatenate([blockdiag(bt[bi][:, sl], headm), blockdiag(kt[bi][:, sl], headm)], axis=0)
           for bi, sl in zip(bis, sls)]
    gram = [_mm_nt(ar[g], xbk[g]) for g in groups]
    hs = [state[g] for g in groups]
    hproj = [_mm_nt(ar[g], hs[g]) for g in groups]
    a_ab = [jnp.where(strict, gram[g][0:t, 0:n], 0.0) for g in groups]
    rhs = [hproj[g][0:t] + dot(jnp.where(strict, gram[g][0:t, n:2 * n], 0.0), vst[g]) for g in groups]
    tinv = [eye + a_ab[g] for g in groups]
    p = [dot(a_ab[g], blockdiag(a_ab[g], same)) for g in groups]
    steps = max(t - 1, 1).bit_length() - 1
    for i in range(steps):
        last = i + 1 == steps
        for g in groups:
            tb = blockdiag(tinv[g], same)
            prod = dot(p[g], tb if last else jnp.concatenate([tb, blockdiag(p[g], same)], axis=1))
            tinv[g] = tinv[g] + prod[:, 0:n]
            if not last:
                p[g] = prod[:, n:2 * n]
    u = [dot(tinv[g], blockdiag(rhs[g], headm)) for g in groups]
    for g in groups:
        r_bk = jnp.where(jnp.concatenate([incl, incl], axis=1), gram[g][t:2 * t, :], 0.0)
        ybuf[bis[g] * t:(bis[g] + 1) * t, sls[g]] = hproj[g][t:2 * t] + dot(
            r_bk, jnp.concatenate([blockdiag(u[g], headm), vst[g]], axis=0))
    for g in groups:
        bi, sl = bis[g], sls[g]
        upd = _mm_tn(jnp.concatenate([u[g], v[bi][:, sl]], axis=0),
                     jnp.concatenate([bt[bi][:, sl], kt[bi][:, sl]], axis=0))
        state[g] = (hs[g] + jnp.where(vals_same, upd, 0.0)) * ptot[bi][:, sl]

    y = ybuf[...]
    ind, indt = ind_ref[...], indt_ref[...]
    headmean = lambda q: _mm_sel_r(_mm_sel_r(q, ind, 2), indt, 2) * (1.0 / hd)
    mu = headmean(y)
    yc = y - mu
    var = headmean(yc * yc)
    yn = yc * lax.rsqrt(var + GN_EPS) * lnw_ref[...] + lnb_ref[...]
    yo = (yn + bonus_ref[...].reshape(nb * t, d)) * g_ref[...].reshape(nb * t, d)
    out = _mm(yo, ow_ref[...])
    for bi in range(nb):
        o_ref[bi] = x_ref[bi] + (1.0 + mod_ref[bi, 2:3, :]) * out[bi * t:(bi + 1) * t]


def _rwkv(x, mod, nw, mix, w_rkv, w0, w1, w2, a0, a1, a2, g1, g2, k_k, k_a, r_k, lnx_w, lnx_b, out_w):
    b, s, d = x.shape
    hd = HEAD_DIM
    assert d % WKV_GROUP == 0 and d // hd <= LANE

    def pad_pair(p1, p2):
        r = p1.shape[1]
        rp = -(-r // LANE) * LANE
        return (jnp.pad(p1, ((0, 0), (0, rp - r))).astype(BF16), jnp.pad(p2, ((0, rp - r), (0, 0))).astype(BF16))

    w1p, w2p = pad_pair(w1, w2)
    a1p, a2p = pad_pair(a1, a2)
    g1p, g2p = pad_pair(g1, g2)
    vecs = jnp.stack([w0, a0, k_k, k_a, r_k.reshape(d), jnp.zeros_like(w0), jnp.zeros_like(w0), jnp.zeros_like(w0)])
    ind, indt = _head_indicator(d, hd)
    wb = w_rkv.astype(BF16)
    t = ROW_TILE
    res = lambda arr: _resident(arr.shape)
    outs = pl.pallas_call(
        _r_in_kernel,
        grid=(b, s // t),
        in_specs=[_rows(t, d), _mod_spec(d), _resident((1, d)), _resident((6, d)), _resident((3, d, d)),
                  res(w1p), res(w2p), res(a1p), res(a2p), res(g1p), res(g2p),
                  _resident((8, d)), res(ind), res(indt)],
        out_specs=[_rows(t, d)] * 8,
        out_shape=[jax.ShapeDtypeStruct((b, s, d), F32)] * 8,
        scratch_shapes=[pltpu.VMEM((t + 8, d), F32), pltpu.VMEM((t, d), F32)],
        compiler_params=_params(True),
        name="r_in",
    )(x, mod, nw.reshape(1, d), mix, wb, w1p, w2p, a1p, a2p, g1p, g2p, vecs, ind, indt)

    tc = WKV_CHUNK
    nb = WKV_BATCH_ROWS if b % WKV_BATCH_ROWS == 0 else 1
    rows = pl.BlockSpec((nb, tc, d), lambda bi, si: (bi, si, 0))
    return pl.pallas_call(
        functools.partial(_r_wkv_kernel, hd=hd),
        grid=(b // nb, s // tc),
        in_specs=[rows] * 9 + [pl.BlockSpec((nb, 6, d), lambda bi, si: (bi, 0, 0)), _resident((1, d)),
                               _resident((1, d)), res(ind), res(indt), _resident((d, d))],
        out_specs=rows,
        out_shape=jax.ShapeDtypeStruct(x.shape, F32),
        scratch_shapes=[pltpu.VMEM((nb * d // WKV_GROUP, WKV_GROUP, WKV_GROUP), F32),
                        pltpu.VMEM((nb * tc, d), F32)],
        compiler_params=_params(True),
        name="r_wkv",
    )(*outs, x, mod, lnx_w.reshape(1, d), lnx_b.reshape(1, d), ind, indt, out_w.astype(BF16))


def _f_in_kernel(x_ref, mod_ref, nw_ref, w_ref, wvt_ref, wf_ref, vec_ref, fb_ref, ind_ref, indt_ref,
                 place_ref, ones_ref, q_ref, k_ref, vt_ref, qa_ref, ka_ref, carry, *, hd):
    t = x_ref.shape[0]

    @pl.when(pl.program_id(1) == 0)
    def _():
        carry[...] = jnp.zeros(carry.shape, F32)

    h = _norm_mod(x_ref[...], nw_ref[...], mod_ref[1:2, :], mod_ref[0:1, :]).astype(BF16)
    ind, indt = ind_ref[...], indt_ref[...]

    def head_rms(q, w):
        ms = _mm_sel_r(_mm_sel_r(q * q, ind, 2), indt, 2) * (1.0 / hd)
        return q * lax.rsqrt(ms + RMS_EPS) * w

    d = x_ref.shape[1]
    q = head_rms(jnp.dot(h, w_ref[:, 0:d], preferred_element_type=F32), vec_ref[0:1, :])
    k = head_rms(jnp.dot(h, w_ref[:, d:2 * d], preferred_element_type=F32), vec_ref[1:2, :])
    q_ref[...] = (q * (hd ** -0.5 * LOG2E)).astype(BF16)
    k_ref[...] = k.astype(BF16)
    vt_ref[...] = _mm_nt(wvt_ref[...], h).astype(BF16)
    f = jnp.dot(h, wf_ref[...], preferred_element_type=F32) + fb_ref[...]
    logf = -_softplus(-f)
    cum = _mm_sel_l(jnp.where(_tril(t), 1.0, 0.0).astype(BF16), logf) + carry[0:1, :]
    carry[...] = jnp.broadcast_to(cum[t - 1:t, :], carry.shape)
    pieces = _split(cum * LOG2E, 3)
    place = lambda first: sum(jnp.dot(p, place_ref[first + i], preferred_element_type=F32)
                              for i, p in enumerate(pieces))
    qa_ref[...] = (place(0) + ones_ref[0:1, :]).astype(BF16)
    ka_ref[...] = (ones_ref[1:2, :] - place(3)).astype(BF16)


def _f_att_kernel(q_ref, k_ref, vt_ref, qa_ref, ka_ref, x_ref, mod_ref, ow_ref, o_ref,
                  obuf, qm_scr, m_scr, acc_scr, *, hd, pairs_per_loop):
    tq = x_ref.shape[0]
    d = x_ref.shape[1]
    tk = vt_ref.shape[2]
    qi = pl.program_id(1)
    lane_l = lax.broadcasted_iota(jnp.int32, (1, LANE), 1)
    first_l = lane_l < hd
    first_r = lax.broadcasted_iota(jnp.int32, (LANE, 1), 0) < hd
    on_or_below = (lax.broadcasted_iota(jnp.int32, (tk, tq), 0)
                   <= lax.broadcasted_iota(jnp.int32, (tk, tq), 1))
    zero_b = jnp.zeros((), BF16)
    one_b = jnp.ones((), BF16)

    lanes = lambda p: slice(p * LANE, (p + 1) * LANE)
    for p0 in range(0, d // LANE, pairs_per_loop):
        pairs = range(p0, p0 + pairs_per_loop)
        heads = [2 * p + i for p in pairs for i in range(2)]
        qa = qa_ref[...]
        for p in pairs:
            qp = q_ref[:, lanes(p)]
            for i, qh in enumerate((jnp.where(first_l, qp, zero_b), jnp.where(first_l, zero_b, qp))):
                h = 2 * p + i
                own = (lane_l >= BIAS_COLS * h) & (lane_l < BIAS_COLS * (h + 1))
                qm_scr[h] = jnp.concatenate([qh, jnp.where(own, qa, zero_b)], axis=1)
        for h in heads:
            m_scr[h] = jnp.full((1, tq), NEG, F32)
            acc_scr[h] = jnp.zeros((LANE, tq), F32)

        def step(j, diagonal, pairs=pairs, heads=heads):
            start = pl.multiple_of(j * tk, tk)
            sc = {}
            kaj = ka_ref[pl.ds(start, tk), :]
            for p in pairs:
                kj = jnp.concatenate([k_ref[pl.ds(start, tk), lanes(p)], kaj], axis=1)
                for h in (2 * p, 2 * p + 1):
                    sc[h] = lax.dot_general(kj, qm_scr[h], (((1,), (1,)), ((), ())), preferred_element_type=F32)
            if diagonal:
                sc = {h: jnp.where(on_or_below, sc[h], NEG) for h in heads}
            m_old = {h: m_scr[h] for h in heads}
            m_new = {h: jnp.maximum(m_old[h], jnp.max(sc[h], axis=0, keepdims=True)) for h in heads}
            pr = {h: jnp.exp2(sc[h] - m_new[h]).astype(BF16) for h in heads}
            for p in pairs:
                vtj = vt_ref[j, lanes(p), :]
                vt2 = (jnp.where(first_r, vtj, one_b), jnp.where(first_r, one_b, vtj))
                for i in range(2):
                    h = 2 * p + i
                    acc_scr[h] = (jnp.exp2(m_old[h] - m_new[h]) * acc_scr[h]
                                  + jnp.dot(vt2[i], pr[h], preferred_element_type=F32))
                    m_scr[h] = m_new[h]

        def body(j, c):
            step(j, False)
            return c

        lax.fori_loop(0, qi, body, 0)
        step(qi, True)
        for p in pairs:
            acc0, acc1 = acc_scr[2 * p], acc_scr[2 * p + 1]
            o_t = jnp.where(first_r, acc0 / acc0[hd:hd + 1, :], acc1 / acc1[0:1, :])
            obuf[:, lanes(p)] = o_t.T.astype(BF16)

    out = jnp.dot(obuf[...], ow_ref[...], preferred_element_type=F32)
    o_ref[...] = x_ref[...] + (1.0 + mod_ref[2:3, :]) * out


def _fox(x, mod, nw, qkvf_w, fgate_b, q_norm_w, k_norm_w, out_w):
    b, s, d = x.shape
    hd = HEAD_DIM
    nh = d // hd
    assert qkvf_w.shape[1] == 3 * d + nh and nh <= LANE and 2 * hd == LANE
    w_all = qkvf_w[:, :3 * d].astype(BF16)
    wf = jnp.pad(qkvf_w[:, 3 * d:], ((0, 0), (0, LANE - nh))).astype(BF16)
    fb = jnp.pad(fgate_b, (0, LANE - nh)).reshape(1, LANE)
    z = jnp.zeros((d,), F32)
    vecs = jnp.stack([jnp.tile(q_norm_w, nh), jnp.tile(k_norm_w, nh), z, z, z, z, z, z])
    ind, indt = _head_indicator(d, hd)
    t = ROW_TILE
    res = lambda arr: _resident(arr.shape)
    nt = s // t
    assert BIAS_COLS * nh <= LANE
    col = jnp.arange(LANE)[None, None, :]
    place = (col == BIAS_COLS * jnp.arange(LANE)[None, :, None] + jnp.arange(BIAS_COLS)[:, None, None])
    place = (place & (jnp.arange(LANE)[None, :, None] < nh)).astype(BF16)
    third = (jnp.arange(LANE) % BIAS_COLS) // 3
    used = jnp.arange(LANE) < BIAS_COLS * nh
    z1 = jnp.zeros((LANE,), F32)
    ones = jnp.stack([(used & (third == 1)).astype(F32), (used & (third == 0)).astype(F32), z1, z1, z1, z1, z1, z1])
    q, k, vt, qa, ka = pl.pallas_call(
        functools.partial(_f_in_kernel, hd=hd),
        grid=(b, nt),
        in_specs=[_rows(t, d), _mod_spec(d), _resident((1, d)), res(w_all),
                  _resident((d, d)), res(wf), _resident((8, d)), _resident((1, LANE)),
                  res(ind), res(indt), res(place), res(ones)],
        out_specs=[_rows(t, d), _rows(t, d), pl.BlockSpec((None, None, d, t), lambda bi, si: (bi, si, 0, 0)),
                   _rows(t, LANE), _rows(t, LANE)],
        out_shape=[jax.ShapeDtypeStruct((b, s, d), BF16), jax.ShapeDtypeStruct((b, s, d), BF16),
                   jax.ShapeDtypeStruct((b, nt, d, t), BF16), jax.ShapeDtypeStruct((b, s, LANE), BF16),
                   jax.ShapeDtypeStruct((b, s, LANE), BF16)],
        scratch_shapes=[pltpu.VMEM((8, LANE), F32)],
        compiler_params=_params(True),
        name="f_in",
    )(x, mod, nw.reshape(1, d), w_all, w_all[:, 2 * d:3 * d].T, wf, vecs, fb, ind, indt, place, ones)

    whole = lambda *shape: pl.BlockSpec((None,) + shape, lambda bi, si: (bi,) + (0,) * len(shape))
    return pl.pallas_call(
        functools.partial(_f_att_kernel, hd=hd, pairs_per_loop=ATT_PAIRS_PER_LOOP),
        grid=(b, nt),
        in_specs=[_rows(t, d), whole(s, d), whole(nt, d, t), _rows(t, LANE), whole(s, LANE),
                  _rows(t, d), _mod_spec(d), _resident((d, d))],
        out_specs=_rows(t, d),
        out_shape=jax.ShapeDtypeStruct(x.shape, F32),
        scratch_shapes=[pltpu.VMEM((t, d), BF16), pltpu.VMEM((nh, t, 2 * LANE), BF16),
                        pltpu.VMEM((nh, 1, t), F32), pltpu.VMEM((nh, LANE, t), F32)],
        compiler_params=_params(False),
        name="f_att",
    )(q, k, vt, qa, ka, x, mod, out_w.astype(BF16))


def kernel(x, c, ada_w, ada_b, norm1_w, norm2_w, ffn_w1, ffn_w3, ffn_w2, m_in_w, m_conv_w, m_conv_b, m_dt_bias, m_A_log, m_D, m_norm_w, m_out_w, r_mix, r_w_rkv, r_w0, r_w1, r_w2, r_a0, r_a1, r_a2, r_g1, r_g2, r_k_k, r_k_a, r_r_k, r_lnx_w, r_lnx_b, r_out_w, f_qkvf_w, f_fgate_b, f_q_norm_w, f_k_norm_w, f_out_w):
    depth = ada_w.shape[0]
    mod = _ada(c, ada_w, ada_b)
    w1b, w3b, w2b = ffn_w1.astype(BF16), ffn_w3.astype(BF16), ffn_w2.astype(BF16)
    m_nh = m_dt_bias.shape[1]
    m_wide = m_in_w.shape[2] - m_nh
    m_in_b = m_in_w[:, :, :m_wide].astype(BF16)
    m_dt_b = jnp.pad(m_in_w[:, :, m_wide:], ((0, 0), (0, 0), (0, LANE - m_nh))).astype(BF16)
    m_out_b = m_out_w.astype(BF16)
    ia = ib = ic = 0
    for i in range(depth):
        kind = i % 3
        if kind == 0:
            x = _mamba(x, mod[i], norm1_w[i], m_in_b, m_dt_b, m_conv_w[ia], m_conv_b[ia], m_dt_bias[ia],
                       m_A_log[ia], m_D[ia], m_norm_w[ia], m_out_b, ia)
            ia += 1
        elif kind == 1:
            x = _rwkv(x, mod[i], norm1_w[i], r_mix[ib], r_w_rkv[ib], r_w0[ib], r_w1[ib], r_w2[ib], r_a0[ib],
                      r_a1[ib], r_a2[ib], r_g1[ib], r_g2[ib], r_k_k[ib], r_k_a[ib], r_r_k[ib],
                      r_lnx_w[ib], r_lnx_b[ib], r_out_w[ib])
            ib += 1
        else:
            x = _fox(x, mod[i], norm1_w[i], f_qkvf_w[ic], f_fgate_b[ic], f_q_norm_w[ic], f_k_norm_w[ic],
                     f_out_w[ic])
            ic += 1
        x = _ffn(x, mod[i], norm2_w[i], w1b, w3b, w2b, i)
    return x
```

```python
import functools

import jax
import jax.numpy as jnp
from jax import lax
from jax.experimental import pallas as pl
from jax.experimental.pallas import tpu as pltpu

F32 = jnp.float32
BF16 = jnp.bfloat16

RMS_EPS = 1e-6
GN_EPS = 64e-5
KK_EPS = 1e-12
NEG = -1e30
LOG2E = 1.4426950408889634

V7X_VMEM_LIMIT = 56 * 1024 * 1024

HEAD_DIM = 64
SSD_GROUPS = 8
SSD_STATE = 128
SSD_CHUNK = 128
WKV_CHUNK = 64
WKV_GROUP = 128
WKV_BATCH_ROWS = 4
ROW_TILE = 256
FFN_ROW_TILE = 512
ATT_PAIRS_PER_LOOP = 8
BIAS_COLS = 6
LANE = 128


def _mm(a, b):
    return jnp.dot(a.astype(BF16), b.astype(BF16), preferred_element_type=F32)


def _mm_nt(a, b):
    return lax.dot_general(a.astype(BF16), b.astype(BF16), (((1,), (1,)), ((), ())),
                           preferred_element_type=F32)


def _mm_tn(a, b):
    return jnp.dot(a.T.astype(BF16), b.astype(BF16), preferred_element_type=F32)


def _split(x, terms):
    parts = []
    for i in range(terms):
        p = x.astype(BF16)
        parts.append(p)
        if i + 1 < terms:
            x = x - p.astype(F32)
    return parts


def _mm_sel_l(sel, x, terms=3):
    return sum(jnp.dot(sel, p, preferred_element_type=F32) for p in _split(x, terms))


def _mm_sel_r(x, sel, terms=3):
    return sum(jnp.dot(p, sel, preferred_element_type=F32) for p in _split(x, terms))


def _mm_hi(a, b):
    ah = a.astype(BF16)
    al = (a - ah.astype(F32)).astype(BF16)
    bh = b.astype(BF16)
    bl = (b - bh.astype(F32)).astype(BF16)
    d = lambda p, q: jnp.dot(p, q, preferred_element_type=F32)
    return d(ah, bh) + d(ah, bl) + d(al, bh)


def _sigmoid(x):
    return jax.nn.sigmoid(x)


def _silu(x):
    hx = 0.5 * x
    return hx + hx * jnp.tanh(hx)


def _softplus(x):
    return jnp.maximum(x, 0.0) + jnp.log(1.0 + jnp.exp(-jnp.abs(x)))


def _norm_mod(x, nw, scale, shift):
    y = x * lax.rsqrt(jnp.mean(x * x, axis=-1, keepdims=True) + RMS_EPS)
    return (y * nw) * (1.0 + scale) + shift


def _tril(n, strict=False):
    r = lax.broadcasted_iota(jnp.int32, (n, n), 0)
    c = lax.broadcasted_iota(jnp.int32, (n, n), 1)
    return (r > c) if strict else (r >= c)


def _resident(shape):
    nd = len(shape)
    return pl.BlockSpec(shape, lambda *_: (0,) * nd, pipeline_mode=pl.Buffered(1))


def _resident_layer(shape, layer):
    nd = len(shape)
    return pl.BlockSpec((None,) + tuple(shape), lambda *_: (layer,) + (0,) * nd, pipeline_mode=pl.Buffered(1))


def _rows(t, width):
    return pl.BlockSpec((None, t, width), lambda b, s: (b, s, 0))


def _mod_spec(d):
    return pl.BlockSpec((None, 6, d), lambda b, s: (b, 0, 0))


def _params(seq_axis_carries):
    sem = ("parallel", "arbitrary") if seq_axis_carries else ("parallel", "parallel")
    return pltpu.CompilerParams(dimension_semantics=sem, vmem_limit_bytes=V7X_VMEM_LIMIT)


def _head_indicator(d, hd):
    ind = (jnp.arange(d)[:, None] // hd == jnp.arange(LANE)[None, :]).astype(BF16)
    return ind, ind.T


def _ada_kernel(c_ref, w_ref, b_ref, o_ref):
    c = c_ref[...]
    o_ref[...] = _mm_hi(_silu(c), w_ref[...]) + b_ref[...]


def _ada(c, ada_w, ada_b):
    depth, d, n6 = ada_w.shape
    b = c.shape[0]
    tn = 1536
    out = pl.pallas_call(
        _ada_kernel,
        grid=(depth, n6 // tn),
        in_specs=[pl.BlockSpec((b, d), lambda l, j: (0, 0)),
                  pl.BlockSpec((None, d, tn), lambda l, j: (l, 0, j)),
                  pl.BlockSpec((None, 1, tn), lambda l, j: (l, 0, j))],
        out_specs=pl.BlockSpec((None, b, tn), lambda l, j: (l, 0, j)),
        out_shape=jax.ShapeDtypeStruct((depth, b, n6), F32),
        compiler_params=_params(False),
        name="ada",
    )(c, ada_w, ada_b.reshape(depth, 1, n6))
    return out.reshape(depth, b, 6, d)


def _ffn_kernel(x_ref, mod_ref, nw_ref, w1_ref, w3_ref, w2_ref, o_ref):
    x = x_ref[...]
    h = _norm_mod(x, nw_ref[...], mod_ref[4:5, :], mod_ref[3:4, :]).astype(BF16)
    a = jnp.dot(h, w1_ref[...], preferred_element_type=F32)
    b = jnp.dot(h, w3_ref[...], preferred_element_type=F32)
    g = (_silu(a) * b).astype(BF16)
    y = jnp.dot(g, w2_ref[...], preferred_element_type=F32)
    o_ref[...] = x + (1.0 + mod_ref[5:6, :]) * y


def _ffn(x, mod, nw, w1, w3, w2, layer):
    b, s, d = x.shape
    dff = w1.shape[2]
    t = FFN_ROW_TILE if s % FFN_ROW_TILE == 0 else ROW_TILE
    return pl.pallas_call(
        _ffn_kernel,
        grid=(b, s // t),
        in_specs=[_rows(t, d), _mod_spec(d), _resident((1, d)), _resident_layer((d, dff), layer),
                  _resident_layer((d, dff), layer), _resident_layer((dff, d), layer)],
        out_specs=_rows(t, d),
        out_shape=jax.ShapeDtypeStruct(x.shape, F32),
        compiler_params=_params(False),
        name="ffn",
    )(x, mod, nw.reshape(1, d), w1, w3, w2)


def _m_in_kernel(x_ref, mod_ref, nw_ref, w_ref, wd_ref, z_ref, xbc_ref, dt_ref):
    h = _norm_mod(x_ref[...], nw_ref[...], mod_ref[1:2, :], mod_ref[0:1, :]).astype(BF16)
    di, cdim = z_ref.shape[1], xbc_ref.shape[1]
    z_ref[...] = jnp.dot(h, w_ref[:, 0:di], preferred_element_type=F32)
    xbc_ref[...] = jnp.dot(h, w_ref[:, di:di + cdim], preferred_element_type=F32)
    dt_ref[...] = jnp.dot(h, wd_ref[...], preferred_element_type=F32)


def _m_ssd_kernel(xbc_ref, z_ref, dt_ref, x_ref, mod_ref, cw_ref, cb_ref, dtb_ref, alog_ref, e_ref,
                  dskip_ref, gnw_ref, ow_ref, o_ref, ubuf, st, ybuf, *, nh):
    t = x_ref.shape[0]
    di = z_ref.shape[1]
    ng, ns, gw = st.shape
    hpg = nh // ng
    hd = gw // hpg

    @pl.when(pl.program_id(1) == 0)
    def _():
        ubuf[0:8, :] = jnp.zeros((8, ubuf.shape[1]), F32)
        st[...] = jnp.zeros(st.shape, F32)

    u = xbc_ref[...]
    ubuf[8:8 + t, :] = u
    acc = cb_ref[...] + cw_ref[3:4, :] * u
    for k in range(3):
        acc = acc + cw_ref[k:k + 1, :] * ubuf[5 + k:5 + k + t, :]
    ubuf[0:8, :] = u[t - 8:t, :]
    xc = _silu(acc)
    xs = xc[:, :di]

    lane = lax.broadcasted_iota(jnp.int32, (1, LANE), 1)
    dt = _softplus(dt_ref[...] + dtb_ref[...])
    a_neg = jnp.where(lane < nh, -jnp.exp(alog_ref[...]), 0.0)
    a = dt * a_neg
    tril = _tril(t)
    cs = _mm_sel_l(jnp.where(tril, 1.0, 0.0).astype(BF16), a)
    cs_t = cs.T
    e = e_ref[...]
    dt_e = _mm_sel_r(dt, e, 2)
    cs_e = _mm_sel_r(cs, e, 2)
    cs_last = cs_e[t - 1:t, :]
    xd = xs * dt_e
    ecs = jnp.exp(cs_e)
    xdd = xd * jnp.exp(cs_last - cs_e)
    cdec = jnp.exp(cs_last)
    yskip = xs * dskip_ref[...]
    gate = _silu(z_ref[...])

    rr = lax.broadcasted_iota(jnp.int32, (hpg * t, gw), 0) // t
    ll = lax.broadcasted_iota(jnp.int32, (hpg * t, gw), 1) // hd
    bd = rr == ll
    groups = range(ng)
    sls = [slice(g * gw, (g + 1) * gw) for g in groups]
    bgs = [xc[:, di + g * ns:di + (g + 1) * ns].astype(BF16) for g in groups]
    cgs = [xc[:, di + (ng + g) * ns:di + (ng + g + 1) * ns].astype(BF16) for g in groups]
    scores = [_mm_nt(cgs[g], bgs[g]) for g in groups]
    sts = [st[g] for g in groups]
    yoff = [_mm(cgs[g], sts[g]) for g in groups]
    supd = [_mm_tn(bgs[g], xdd[:, sls[g]]) for g in groups]
    ydiag = []
    for g in groups:
        parts = []
        for j in range(hpg):
            h = g * hpg + j
            diff = cs[:, h:h + 1] - cs_t[h:h + 1, :]
            parts.append((scores[g] * jnp.exp(jnp.where(tril, diff, NEG))).astype(BF16))
        lhs = jnp.concatenate(parts, axis=1)
        rhs = jnp.where(bd, jnp.concatenate([xd[:, sls[g]]] * hpg, axis=0), 0.0).astype(BF16)
        ydiag.append(jnp.dot(lhs, rhs, preferred_element_type=F32))
    for g in groups:
        sl = sls[g]
        st[g] = sts[g] * cdec[:, sl] + supd[g]
        y = (ydiag[g] + yoff[g] * ecs[:, sl] + yskip[:, sl]) * gate[:, sl]
        y = y * lax.rsqrt(jnp.mean(y * y, axis=-1, keepdims=True) + RMS_EPS) * gnw_ref[:, sl]
        ybuf[:, sl] = y.astype(BF16)

    out = jnp.dot(ybuf[...], ow_ref[...], preferred_element_type=F32)
    o_ref[...] = x_ref[...] + (1.0 + mod_ref[2:3, :]) * out


def _mamba(x, mod, nw, in_w, dt_w, conv_w, conv_b, dt_bias, a_log, d_skip, norm_w, out_w, layer):
    b, s, d = x.shape
    nh = dt_bias.shape[0]
    di = nh * HEAD_DIM
    cdim = conv_w.shape[1]
    ng, ns = SSD_GROUPS, SSD_STATE
    gw = di // ng
    assert cdim == di + 2 * ng * ns and in_w.shape[2] == di + cdim + nh and nh <= LANE
    t = FFN_ROW_TILE if s % FFN_ROW_TILE == 0 else ROW_TILE
    z, xbc, dtr = pl.pallas_call(
        _m_in_kernel,
        grid=(b, s // t),
        in_specs=[_rows(t, d), _mod_spec(d), _resident((1, d)),
                  _resident_layer((d, di + cdim + nh), layer), _resident_layer((d, LANE), layer)],
        out_specs=[_rows(t, di), _rows(t, cdim), _rows(t, LANE)],
        out_shape=[jax.ShapeDtypeStruct((b, s, di), F32), jax.ShapeDtypeStruct((b, s, cdim), F32),
                   jax.ShapeDtypeStruct((b, s, LANE), F32)],
        compiler_params=_params(False),
        name="m_in",
    )(x, mod, nw.reshape(1, d), in_w, dt_w)

    pad1 = lambda v: jnp.pad(v, (0, LANE - nh)).reshape(1, LANE)
    expand = (jnp.arange(LANE)[:, None] == jnp.arange(di)[None, :] // HEAD_DIM).astype(BF16)
    tc = SSD_CHUNK
    return pl.pallas_call(
        functools.partial(_m_ssd_kernel, nh=nh),
        grid=(b, s // tc),
        in_specs=[_rows(tc, cdim), _rows(tc, di), _rows(tc, LANE), _rows(tc, d), _mod_spec(d),
                  _resident((4, cdim)), _resident((1, cdim)), _resident((1, LANE)), _resident((1, LANE)),
                  _resident((LANE, di)), _resident((1, di)), _resident((1, di)),
                  _resident_layer((di, d), layer)],
        out_specs=_rows(tc, d),
        out_shape=jax.ShapeDtypeStruct(x.shape, F32),
        scratch_shapes=[pltpu.VMEM((tc + 8, cdim), F32), pltpu.VMEM((ng, ns, gw), F32),
                        pltpu.VMEM((tc, di), BF16)],
        compiler_params=_params(True),
        name="m_ssd",
    )(xbc, z, dtr, x, mod, conv_w, conv_b.reshape(1, cdim), pad1(dt_bias), pad1(a_log), expand,
      jnp.repeat(d_skip, HEAD_DIM).reshape(1, di), norm_w.reshape(1, di), out_w)


def _r_in_kernel(x_ref, mod_ref, nw_ref, mix_ref, wrkv_ref, w1_ref, w2_ref, a1_ref, a2_ref,
                 g1_ref, g2_ref, vec_ref, ind_ref, indt_ref,
                 r_ref, lw_ref, k_ref, v_ref, kk_ref, bb_ref, bonus_ref, g_ref, hbuf, dxbuf):
    t = x_ref.shape[0]

    @pl.when(pl.program_id(1) == 0)
    def _():
        hbuf[0:8, :] = jnp.zeros((8, hbuf.shape[1]), F32)

    hbuf[8:8 + t, :] = _norm_mod(x_ref[...], nw_ref[...], mod_ref[1:2, :], mod_ref[0:1, :])
    dxbuf[...] = hbuf[7:7 + t, :] - hbuf[8:8 + t, :]
    hbuf[0:8, :] = hbuf[t:t + 8, :]
    w0, a0, k_k, k_a, r_k = (vec_ref[i:i + 1, :] for i in range(5))
    ind, indt = ind_ref[...], indt_ref[...]
    headsum = lambda q: _mm_sel_r(_mm_sel_r(q, ind, 2), indt, 2)
    dotf = lambda a, b_ref: jnp.dot(a, b_ref[...], preferred_element_type=F32)
    mixed = lambda n: (hbuf[8:8 + t, :] + dxbuf[...] * mix_ref[n:n + 1, :]).astype(BF16)

    r = dotf(mixed(0), wrkv_ref.at[0])
    k = dotf(mixed(1), wrkv_ref.at[1])
    v = dotf(mixed(2), wrkv_ref.at[2])
    wl = dotf(jnp.tanh(dotf(mixed(3), w1_ref)).astype(BF16), w2_ref)
    al = dotf(dotf(mixed(4), a1_ref).astype(BF16), a2_ref)
    r_ref[...] = r
    v_ref[...] = v
    g_ref[...] = dotf(_sigmoid(dotf(mixed(5), g1_ref)).astype(BF16), g2_ref)
    lw_ref[...] = -jnp.exp(-_softplus(-(w0 + wl)) - 0.5)
    a = _sigmoid(a0 + al)
    kk = k * k_k
    kk = kk * lax.rsqrt(jnp.maximum(headsum(kk * kk), KK_EPS * KK_EPS))
    k2 = k * (1.0 + (a - 1.0) * k_a)
    k_ref[...] = k2
    kk_ref[...] = kk
    bb_ref[...] = kk * a
    bonus_ref[...] = headsum(r * k2 * r_k) * v


def _r_wkv_kernel(r_ref, lw_ref, k_ref, v_ref, kk_ref, bb_ref, bonus_ref, g_ref, x_ref, mod_ref,
                  lnw_ref, lnb_ref, ind_ref, indt_ref, ow_ref, o_ref, state, ybuf, *, hd):
    nb, t, d = x_ref.shape
    gw = state.shape[1]
    ngr = d // gw
    hpg = gw // hd
    n = hpg * t

    @pl.when(pl.program_id(1) == 0)
    def _():
        state[...] = jnp.zeros(state.shape, F32)

    tri = jnp.where(_tril(t), 1.0, 0.0).astype(BF16)
    at, rt, kt, bt, v, ptot = [], [], [], [], [], []
    for bi in range(nb):
        lw = lw_ref[bi]
        cl = _mm_sel_l(tri, lw)
        ecl = jnp.exp(cl)
        encl = jnp.exp(-cl)
        rt.append(r_ref[bi] * ecl)
        at.append(-kk_ref[bi] * jnp.exp(cl - lw))
        kt.append(k_ref[bi] * encl)
        bt.append(bb_ref[bi] * encl)
        v.append(v_ref[bi])
        ptot.append(ecl[t - 1:t, :])

    ri = lax.broadcasted_iota(jnp.int32, (n, gw), 0)
    ci = lax.broadcasted_iota(jnp.int32, (n, gw), 1)
    headm = (ri // t) == (ci // hd)
    ri2 = lax.broadcasted_iota(jnp.int32, (n, n), 0)
    ci2 = lax.broadcasted_iota(jnp.int32, (n, n), 1)
    same = (ri2 // t) == (ci2 // t)
    tt = lax.broadcasted_iota(jnp.int32, (t, n), 0)
    ss = lax.broadcasted_iota(jnp.int32, (t, n), 1) % t
    strict = tt > ss
    incl = tt >= ss
    eye = jnp.where(tt == ss, 1.0, 0.0)
    vals_same = (lax.broadcasted_iota(jnp.int32, (gw, gw), 0) // hd) == (
        lax.broadcasted_iota(jnp.int32, (gw, gw), 1) // hd)

    def blockdiag(m, mask):
        return jnp.where(mask, jnp.concatenate([m] * hpg, axis=0), 0.0).astype(BF16)

    groups = range(nb * ngr)
    bis = [g // ngr for g in groups]
    sls = [slice((g % ngr) * gw, (g % ngr + 1) * gw) for g in groups]
    dot = lambda a, b: jnp.dot(a.astype(BF16), b, preferred_element_type=F32)
    ar = [jnp.concatenate([at[bi][:, sl], rt[bi][:, sl]], axis=0) for bi, sl in zip(bis, sls)]
    vst = [blockdiag(v[bi][:, sl], headm) for bi, sl in zip(bis, sls)]
    xbk = [jnp.concatenate([blockdiag(bt[bi][:, sl], headm), blockdiag(kt[bi][:, sl], headm)], axis=0)
           for bi, sl in zip(bis, sls)]
    gram = [_mm_nt(ar[g], xbk[g]) for g in groups]
    hs = [state[g] for g in groups]
    hproj = [_mm_nt(ar[g], hs[g]) for g in groups]
    a_ab = [jnp.where(strict, gram[g][0:t, 0:n], 0.0) for g in groups]
    rhs = [hproj[g][0:t] + dot(jnp.where(strict, gram[g][0:t, n:2 * n], 0.0), vst[g]) for g in groups]
    tinv = [eye + a_ab[g] for g in groups]
    p = [dot(a_ab[g], blockdiag(a_ab[g], same)) for g in groups]
    steps = max(t - 1, 1).bit_length() - 1
    for i in range(steps):
        last = i + 1 == steps
        for g in groups:
            tb = blockdiag(tinv[g], same)
            prod = dot(p[g], tb if last else jnp.concatenate([tb, blockdiag(p[g], same)], axis=1))
            tinv[g] = tinv[g] + prod[:, 0:n]
            if not last:
                p[g] = prod[:, n:2 * n]
    u = [dot(tinv[g], blockdiag(rhs[g], headm)) for g in groups]
    for g in groups:
        r_bk = jnp.where(jnp.concatenate([incl, incl], axis=1), gram[g][t:2 * t, :], 0.0)
        ybuf[bis[g] * t:(bis[g] + 1) * t, sls[g]] = hproj[g][t:2 * t] + dot(
            r_bk, jnp.concatenate([blockdiag(u[g], headm), vst[g]], axis=0))
    for g in groups:
        bi, sl = bis[g], sls[g]
        upd = _mm_tn(jnp.concatenate([u[g], v[bi][:, sl]], axis=0),
                     jnp.concatenate([bt[bi][:, sl], kt[bi][:, sl]], axis=0))
        state[g] = (hs[g] + jnp.where(vals_same, upd, 0.0)) * ptot[bi][:, sl]

    y = ybuf[...]
    ind, indt = ind_ref[...], indt_ref[...]
    headmean = lambda q: _mm_sel_r(_mm_sel_r(q, ind, 2), indt, 2) * (1.0 / hd)
    mu = headmean(y)
    yc = y - mu
    var = headmean(yc * yc)
    yn = yc * lax.rsqrt(var + GN_EPS) * lnw_ref[...] + lnb_ref[...]
    yo = (yn + bonus_ref[...].reshape(nb * t, d)) * g_ref[...].reshape(nb * t, d)
    out = _mm(yo, ow_ref[...])
    for bi in range(nb):
        o_ref[bi] = x_ref[bi] + (1.0 + mod_ref[bi, 2:3, :]) * out[bi * t:(bi + 1) * t]


def _rwkv(x, mod, nw, mix, w_rkv, w0, w1, w2, a0, a1, a2, g1, g2, k_k, k_a, r_k, lnx_w, lnx_b, out_w):
    b, s, d = x.shape
    hd = HEAD_DIM
    assert d % WKV_GROUP == 0 and d // hd <= LANE

    def pad_pair(p1, p2):
        r = p1.shape[1]
        rp = -(-r // LANE) * LANE
        return (jnp.pad(p1, ((0, 0), (0, rp - r))).astype(BF16), jnp.pad(p2, ((0, rp - r), (0, 0))).astype(BF16))

    w1p, w2p = pad_pair(w1, w2)
    a1p, a2p = pad_pair(a1, a2)
    g1p, g2p = pad_pair(g1, g2)
    vecs = jnp.stack([w0, a0, k_k, k_a, r_k.reshape(d), jnp.zeros_like(w0), jnp.zeros_like(w0), jnp.zeros_like(w0)])
    ind, indt = _head_indicator(d, hd)
    wb = w_rkv.astype(BF16)
    t = ROW_TILE
    res = lambda arr: _resident(arr.shape)
    outs = pl.pallas_call(
        _r_in_kernel,
        grid=(b, s // t),
        in_specs=[_rows(t, d), _mod_spec(d), _resident((1, d)), _resident((6, d)), _resident((3, d, d)),
                  res(w1p), res(w2p), res(a1p), res(a2p), res(g1p), res(g2p),
                  _resident((8, d)), res(ind), res(indt)],
        out_specs=[_rows(t, d)] * 8,
        out_shape=[jax.ShapeDtypeStruct((b, s, d), F32)] * 8,
        scratch_shapes=[pltpu.VMEM((t + 8, d), F32), pltpu.VMEM((t, d), F32)],
        compiler_params=_params(True),
        name="r_in",
    )(x, mod, nw.reshape(1, d), mix, wb, w1p, w2p, a1p, a2p, g1p, g2p, vecs, ind, indt)

    tc = WKV_CHUNK
    nb = WKV_BATCH_ROWS if b % WKV_BATCH_ROWS == 0 else 1
    rows = pl.BlockSpec((nb, tc, d), lambda bi, si: (bi, si, 0))
    return pl.pallas_call(
        functools.partial(_r_wkv_kernel, hd=hd),
        grid=(b // nb, s // tc),
        in_specs=[rows] * 9 + [pl.BlockSpec((nb, 6, d), lambda bi, si: (bi, 0, 0)), _resident((1, d)),
                               _resident((1, d)), res(ind), res(indt), _resident((d, d))],
        out_specs=rows,
        out_shape=jax.ShapeDtypeStruct(x.shape, F32),
        scratch_shapes=[pltpu.VMEM((nb * d // WKV_GROUP, WKV_GROUP, WKV_GROUP), F32),
                        pltpu.VMEM((nb * tc, d), F32)],
        compiler_params=_params(True),
        name="r_wkv",
    )(*outs, x, mod, lnx_w.reshape(1, d), lnx_b.reshape(1, d), ind, indt, out_w.astype(BF16))


def _f_in_kernel(x_ref, mod_ref, nw_ref, w_ref, wvt_ref, wf_ref, vec_ref, fb_ref, ind_ref, indt_ref,
                 place_ref, ones_ref, q_ref, k_ref, vt_ref, qa_ref, ka_ref, carry, *, hd):
    t = x_ref.shape[0]

    @pl.when(pl.program_id(1) == 0)
    def _():
        carry[...] = jnp.zeros(carry.shape, F32)

    h = _norm_mod(x_ref[...], nw_ref[...], mod_ref[1:2, :], mod_ref[0:1, :]).astype(BF16)
    ind, indt = ind_ref[...], indt_ref[...]

    def head_rms(q, w):
        ms = _mm_sel_r(_mm_sel_r(q * q, ind, 2), indt, 2) * (1.0 / hd)
        return q * lax.rsqrt(ms + RMS_EPS) * w

    d = x_ref.shape[1]
    q = head_rms(jnp.dot(h, w_ref[:, 0:d], preferred_element_type=F32), vec_ref[0:1, :])
    k = head_rms(jnp.dot(h, w_ref[:, d:2 * d], preferred_element_type=F32), vec_ref[1:2, :])
    q_ref[...] = (q * (hd ** -0.5 * LOG2E)).astype(BF16)
    k_ref[...] = k.astype(BF16)
    vt_ref[...] = _mm_nt(wvt_ref[...], h).astype(BF16)
    f = jnp.dot(h, wf_ref[...], preferred_element_type=F32) + fb_ref[...]
    logf = -_softplus(-f)
    cum = _mm_sel_l(jnp.where(_tril(t), 1.0, 0.0).astype(BF16), logf) + carry[0:1, :]
    carry[...] = jnp.broadcast_to(cum[t - 1:t, :], carry.shape)
    pieces = _split(cum * LOG2E, 3)
    place = lambda first: sum(jnp.dot(p, place_ref[first + i], preferred_element_type=F32)
                              for i, p in enumerate(pieces))
    qa_ref[...] = (place(0) + ones_ref[0:1, :]).astype(BF16)
    ka_ref[...] = (ones_ref[1:2, :] - place(3)).astype(BF16)


def _f_att_kernel(q_ref, k_ref, vt_ref, qa_ref, ka_ref, x_ref, mod_ref, ow_ref, o_ref,
                  obuf, qm_scr, m_scr, acc_scr, *, hd, pairs_per_loop):
    tq = x_ref.shape[0]
    d = x_ref.shape[1]
    tk = vt_ref.shape[2]
    qi = pl.program_id(1)
    lane_l = lax.broadcasted_iota(jnp.int32, (1, LANE), 1)
    first_l = lane_l < hd
    first_r = lax.broadcasted_iota(jnp.int32, (LANE, 1), 0) < hd
    on_or_below = (lax.broadcasted_iota(jnp.int32, (tk, tq), 0)
                   <= lax.broadcasted_iota(jnp.int32, (tk, tq), 1))
    zero_b = jnp.zeros((), BF16)
    one_b = jnp.ones((), BF16)

    lanes = lambda p: slice(p * LANE, (p + 1) * LANE)
    for p0 in range(0, d // LANE, pairs_per_loop):
        pairs = range(p0, p0 + pairs_per_loop)
        heads = [2 * p + i for p in pairs for i in range(2)]
        qa = qa_ref[...]
        for p in pairs:
            qp = q_ref[:, lanes(p)]
            for i, qh in enumerate((jnp.where(first_l, qp, zero_b), jnp.where(first_l, zero_b, qp))):
                h = 2 * p + i
                own = (lane_l >= BIAS_COLS * h) & (lane_l < BIAS_COLS * (h + 1))
                qm_scr[h] = jnp.concatenate([qh, jnp.where(own, qa, zero_b)], axis=1)
        for h in heads:
            m_scr[h] = jnp.full((1, tq), NEG, F32)
            acc_scr[h] = jnp.zeros((LANE, tq), F32)

        def step(j, diagonal, pairs=pairs, heads=heads):
            start = pl.multiple_of(j * tk, tk)
            sc = {}
            kaj = ka_ref[pl.ds(start, tk), :]
            for p in pairs:
                kj = jnp.concatenate([k_ref[pl.ds(start, tk), lanes(p)], kaj], axis=1)
                for h in (2 * p, 2 * p + 1):
                    sc[h] = lax.dot_general(kj, qm_scr[h], (((1,), (1,)), ((), ())), preferred_element_type=F32)
            if diagonal:
                sc = {h: jnp.where(on_or_below, sc[h], NEG) for h in heads}
            m_old = {h: m_scr[h] for h in heads}
            m_new = {h: jnp.maximum(m_old[h], jnp.max(sc[h], axis=0, keepdims=True)) for h in heads}
            pr = {h: jnp.exp2(sc[h] - m_new[h]).astype(BF16) for h in heads}
            for p in pairs:
                vtj = vt_ref[j, lanes(p), :]
                vt2 = (jnp.where(first_r, vtj, one_b), jnp.where(first_r, one_b, vtj))
                for i in range(2):
                    h = 2 * p + i
                    acc_scr[h] = (jnp.exp2(m_old[h] - m_new[h]) * acc_scr[h]
                                  + jnp.dot(vt2[i], pr[h], preferred_element_type=F32))
                    m_scr[h] = m_new[h]

        def body(j, c):
            step(j, False)
            return c

        lax.fori_loop(0, qi, body, 0)
        step(qi, True)
        for p in pairs:
            acc0, acc1 = acc_scr[2 * p], acc_scr[2 * p + 1]
            o_t = jnp.where(first_r, acc0 / acc0[hd:hd + 1, :], acc1 / acc1[0:1, :])
            obuf[:, lanes(p)] = o_t.T.astype(BF16)

    out = jnp.dot(obuf[...], ow_ref[...], preferred_element_type=F32)
    o_ref[...] = x_ref[...] + (1.0 + mod_ref[2:3, :]) * out


def _fox(x, mod, nw, qkvf_w, fgate_b, q_norm_w, k_norm_w, out_w):
    b, s, d = x.shape
    hd = HEAD_DIM
    nh = d // hd
    assert qkvf_w.shape[1] == 3 * d + nh and nh <= LANE and 2 * hd == LANE
    w_all = qkvf_w.astype(BF16)
    wf = jnp.pad(qkvf_w[:, 3 * d:], ((0, 0), (0, LANE - nh))).astype(BF16)
    fb = jnp.pad(fgate_b, (0, LANE - nh)).reshape(1, LANE)
    z = jnp.zeros((d,), F32)
    vecs = jnp.stack([jnp.tile(q_norm_w, nh), jnp.tile(k_norm_w, nh), z, z, z, z, z, z])
    ind, indt = _head_indicator(d, hd)
    t = ROW_TILE
    res = lambda arr: _resident(arr.shape)
    nt = s // t
    assert BIAS_COLS * nh <= LANE
    col = jnp.arange(LANE)[None, None, :]
    place = (col == BIAS_COLS * jnp.arange(LANE)[None, :, None] + jnp.arange(BIAS_COLS)[:, None, None])
    place = (place & (jnp.arange(LANE)[None, :, None] < nh)).astype(BF16)
    third = (jnp.arange(LANE) % BIAS_COLS) // 3
    used = jnp.arange(LANE) < BIAS_COLS * nh
    z1 = jnp.zeros((LANE,), F32)
    ones = jnp.stack([(used & (third == 1)).astype(F32), (used & (third == 0)).astype(F32), z1, z1, z1, z1, z1, z1])
    q, k, vt, qa, ka = pl.pallas_call(
        functools.partial(_f_in_kernel, hd=hd),
        grid=(b, nt),
        in_specs=[_rows(t, d), _mod_spec(d), _resident((1, d)), res(w_all),
                  _resident((d, d)), res(wf), _resident((8, d)), _resident((1, LANE)),
                  res(ind), res(indt), res(place), res(ones)],
        out_specs=[_rows(t, d), _rows(t, d), pl.BlockSpec((None, None, d, t), lambda bi, si: (bi, si, 0, 0)),
                   _rows(t, LANE), _rows(t, LANE)],
        out_shape=[jax.ShapeDtypeStruct((b, s, d), BF16), jax.ShapeDtypeStruct((b, s, d), BF16),
                   jax.ShapeDtypeStruct((b, nt, d, t), BF16), jax.ShapeDtypeStruct((b, s, LANE), BF16),
                   jax.ShapeDtypeStruct((b, s, LANE), BF16)],
        scratch_shapes=[pltpu.VMEM((8, LANE), F32)],
        compiler_params=_params(True),
        name="f_in",
    )(x, mod, nw.reshape(1, d), w_all, w_all[:, 2 * d:3 * d].T, wf, vecs, fb, ind, indt, place, ones)

    whole = lambda *shape: pl.BlockSpec((None,) + shape, lambda bi, si: (bi,) + (0,) * len(shape))
    return pl.pallas_call(
        functools.partial(_f_att_kernel, hd=hd, pairs_per_loop=ATT_PAIRS_PER_LOOP),
        grid=(b, nt),
        in_specs=[_rows(t, d), whole(s, d), whole(nt, d, t), _rows(t, LANE), whole(s, LANE),
                  _rows(t, d), _mod_spec(d), _resident((d, d))],
        out_specs=_rows(t, d),
        out_shape=jax.ShapeDtypeStruct(x.shape, F32),
        scratch_shapes=[pltpu.VMEM((t, d), BF16), pltpu.VMEM((nh, t, 2 * LANE), BF16),
                        pltpu.VMEM((nh, 1, t), F32), pltpu.VMEM((nh, LANE, t), F32)],
        compiler_params=_params(False),
        name="f_att",
    )(q, k, vt, qa, ka, x, mod, out_w.astype(BF16))


def kernel(x, c, ada_w, ada_b, norm1_w, norm2_w, ffn_w1, ffn_w3, ffn_w2, m_in_w, m_conv_w, m_conv_b, m_dt_bias, m_A_log, m_D, m_norm_w, m_out_w, r_mix, r_w_rkv, r_w0, r_w1, r_w2, r_a0, r_a1, r_a2, r_g1, r_g2, r_k_k, r_k_a, r_r_k, r_lnx_w, r_lnx_b, r_out_w, f_qkvf_w, f_fgate_b, f_q_norm_w, f_k_norm_w, f_out_w):
    depth = ada_w.shape[0]
    mod = _ada(c, ada_w, ada_b)
    w1b, w3b, w2b = ffn_w1.astype(BF16), ffn_w3.astype(BF16), ffn_w2.astype(BF16)
    m_nh = m_dt_bias.shape[1]
    m_in_b = m_in_w.astype(BF16)
    m_dt_b = jnp.pad(m_in_w[:, :, m_in_w.shape[2] - m_nh:], ((0, 0), (0, 0), (0, LANE - m_nh))).astype(BF16)
    m_out_b = m_out_w.astype(BF16)
    ia = ib = ic = 0
    for i in range(depth):
        kind = i % 3
        if kind == 0:
            x = _mamba(x, mod[i], norm1_w[i], m_in_b, m_dt_b, m_conv_w[ia], m_conv_b[ia], m_dt_bias[ia],
                       m_A_log[ia], m_D[ia], m_norm_w[ia], m_out_b, ia)
            ia += 1
        elif kind == 1:
            x = _rwkv(x, mod[i], norm1_w[i], r_mix[ib], r_w_rkv[ib], r_w0[ib], r_w1[ib], r_w2[ib], r_a0[ib],
                      r_a1[ib], r_a2[ib], r_g1[ib], r_g2[ib], r_k_k[ib], r_k_a[ib], r_r_k[ib],
                      r_lnx_w[ib], r_lnx_b[ib], r_out_w[ib])
            ib += 1
        else:
            x = _fox(x, mod[i], norm1_w[i], f_qkvf_w[ic], f_fgate_b[ic], f_q_norm_w[ic], f_k_norm_w[ic],
                     f_out_w[ic])
            ic += 1
        x = _ffn(x, mod[i], norm2_w[i], w1b, w3b, w2b, i)
    return x
```

```python
import functools

import jax
import jax.numpy as jnp
from jax import lax
from jax.experimental import pallas as pl
from jax.experimental.pallas import tpu as pltpu

F32 = jnp.float32
BF16 = jnp.bfloat16

RMS_EPS = 1e-6
GN_EPS = 64e-5
KK_EPS = 1e-12
NEG = -1e30
LOG2E = 1.4426950408889634

V7X_VMEM_LIMIT = 56 * 1024 * 1024

HEAD_DIM = 64
SSD_GROUPS = 8
SSD_STATE = 128
SSD_CHUNK = 128
WKV_CHUNK = 64
WKV_GROUP = 128
WKV_BATCH_ROWS = 4
ROW_TILE = 256
FFN_ROW_TILE = 512
ATT_PAIRS_PER_LOOP = 8
BIAS_COLS = 6
LANE = 128


def _mm(a, b):
    return jnp.dot(a.astype(BF16), b.astype(BF16), preferred_element_type=F32)


def _mm_nt(a, b):
    return lax.dot_general(a.astype(BF16), b.astype(BF16), (((1,), (1,)), ((), ())),
                           preferred_element_type=F32)


def _mm_tn(a, b):
    return jnp.dot(a.T.astype(BF16), b.astype(BF16), preferred_element_type=F32)


def _split(x, terms):
    parts = []
    for i in range(terms):
        p = x.astype(BF16)
        parts.append(p)
        if i + 1 < terms:
            x = x - p.astype(F32)
    return parts


def _mm_sel_l(sel, x, terms=3):
    return sum(jnp.dot(sel, p, preferred_element_type=F32) for p in _split(x, terms))


def _mm_sel_r(x, sel, terms=3):
    return sum(jnp.dot(p, sel, preferred_element_type=F32) for p in _split(x, terms))


def _mm_hi(a, b):
    ah = a.astype(BF16)
    al = (a - ah.astype(F32)).astype(BF16)
    bh = b.astype(BF16)
    bl = (b - bh.astype(F32)).astype(BF16)
    d = lambda p, q: jnp.dot(p, q, preferred_element_type=F32)
    return d(ah, bh) + d(ah, bl) + d(al, bh)


def _sigmoid(x):
    return jax.nn.sigmoid(x)


def _silu(x):
    hx = 0.5 * x
    return hx + hx * jnp.tanh(hx)


def _softplus(x):
    return jnp.maximum(x, 0.0) + jnp.log(1.0 + jnp.exp(-jnp.abs(x)))


def _norm_mod(x, nw, scale, shift):
    y = x * lax.rsqrt(jnp.mean(x * x, axis=-1, keepdims=True) + RMS_EPS)
    return (y * nw) * (1.0 + scale) + shift


def _tril(n, strict=False):
    r = lax.broadcasted_iota(jnp.int32, (n, n), 0)
    c = lax.broadcasted_iota(jnp.int32, (n, n), 1)
    return (r > c) if strict else (r >= c)


def _resident(shape):
    nd = len(shape)
    return pl.BlockSpec(shape, lambda *_: (0,) * nd, pipeline_mode=pl.Buffered(1))


def _resident_layer(shape, layer):
    nd = len(shape)
    return pl.BlockSpec((None,) + tuple(shape), lambda *_: (layer,) + (0,) * nd, pipeline_mode=pl.Buffered(1))


def _rows(t, width):
    return pl.BlockSpec((None, t, width), lambda b, s: (b, s, 0))


def _mod_spec(d):
    return pl.BlockSpec((None, 6, d), lambda b, s: (b, 0, 0))


def _params(seq_axis_carries):
    sem = ("parallel", "arbitrary") if seq_axis_carries else ("parallel", "parallel")
    return pltpu.CompilerParams(dimension_semantics=sem, vmem_limit_bytes=V7X_VMEM_LIMIT)


def _head_indicator(d, hd):
    ind = (jnp.arange(d)[:, None] // hd == jnp.arange(LANE)[None, :]).astype(BF16)
    return ind, ind.T


def _ada_kernel(c_ref, w_ref, b_ref, o_ref):
    c = c_ref[...]
    o_ref[...] = _mm_hi(_silu(c), w_ref[...]) + b_ref[...]


def _ada(c, ada_w, ada_b):
    depth, d, n6 = ada_w.shape
    b = c.shape[0]
    tn = 1536
    out = pl.pallas_call(
        _ada_kernel,
        grid=(depth, n6 // tn),
        in_specs=[pl.BlockSpec((b, d), lambda l, j: (0, 0)),
                  pl.BlockSpec((None, d, tn), lambda l, j: (l, 0, j)),
                  pl.BlockSpec((None, 1, tn), lambda l, j: (l, 0, j))],
        out_specs=pl.BlockSpec((None, b, tn), lambda l, j: (l, 0, j)),
        out_shape=jax.ShapeDtypeStruct((depth, b, n6), F32),
        compiler_params=_params(False),
        name="ada",
    )(c, ada_w, ada_b.reshape(depth, 1, n6))
    return out.reshape(depth, b, 6, d)


def _ffn_kernel(x_ref, mod_ref, nw_ref, w1_ref, w3_ref, w2_ref, o_ref):
    x = x_ref[...]
    h = _norm_mod(x, nw_ref[...], mod_ref[4:5, :], mod_ref[3:4, :]).astype(BF16)
    a = jnp.dot(h, w1_ref[...], preferred_element_type=F32)
    b = jnp.dot(h, w3_ref[...], preferred_element_type=F32)
    g = (_silu(a) * b).astype(BF16)
    y = jnp.dot(g, w2_ref[...], preferred_element_type=F32)
    o_ref[...] = x + (1.0 + mod_ref[5:6, :]) * y


def _ffn(x, mod, nw, w1, w3, w2, layer):
    b, s, d = x.shape
    dff = w1.shape[2]
    t = FFN_ROW_TILE if s % FFN_ROW_TILE == 0 else ROW_TILE
    return pl.pallas_call(
        _ffn_kernel,
        grid=(b, s // t),
        in_specs=[_rows(t, d), _mod_spec(d), _resident((1, d)), _resident_layer((d, dff), layer),
                  _resident_layer((d, dff), layer), _resident_layer((dff, d), layer)],
        out_specs=_rows(t, d),
        out_shape=jax.ShapeDtypeStruct(x.shape, F32),
        compiler_params=_params(False),
        name="ffn",
    )(x, mod, nw.reshape(1, d), w1, w3, w2)


def _m_in_kernel(x_ref, mod_ref, nw_ref, w_ref, wd_ref, z_ref, xbc_ref, dt_ref):
    h = _norm_mod(x_ref[...], nw_ref[...], mod_ref[1:2, :], mod_ref[0:1, :]).astype(BF16)
    di, cdim = z_ref.shape[1], xbc_ref.shape[1]
    z_ref[...] = _silu(jnp.dot(h, w_ref[:, 0:di], preferred_element_type=F32))
    xbc_ref[...] = jnp.dot(h, w_ref[:, di:di + cdim], preferred_element_type=F32)
    dt_ref[...] = jnp.dot(h, wd_ref[...], preferred_element_type=F32)


def _m_ssd_kernel(xbc_ref, z_ref, dt_ref, x_ref, mod_ref, cw_ref, cb_ref, dtb_ref, alog_ref, e_ref,
                  dskip_ref, gnw_ref, ow_ref, o_ref, ubuf, st, ybuf, *, nh):
    t = x_ref.shape[0]
    di = z_ref.shape[1]
    ng, ns, gw = st.shape
    hpg = nh // ng
    hd = gw // hpg

    @pl.when(pl.program_id(1) == 0)
    def _():
        ubuf[0:8, :] = jnp.zeros((8, ubuf.shape[1]), F32)
        st[...] = jnp.zeros(st.shape, F32)

    u = xbc_ref[...]
    ubuf[8:8 + t, :] = u
    acc = cb_ref[...] + cw_ref[3:4, :] * u
    for k in range(3):
        acc = acc + cw_ref[k:k + 1, :] * ubuf[5 + k:5 + k + t, :]
    ubuf[0:8, :] = u[t - 8:t, :]
    xc = _silu(acc)
    xs = xc[:, :di]

    lane = lax.broadcasted_iota(jnp.int32, (1, LANE), 1)
    dt = _softplus(dt_ref[...] + dtb_ref[...])
    a_neg = jnp.where(lane < nh, -jnp.exp(alog_ref[...]), 0.0)
    a = dt * a_neg
    tril = _tril(t)
    cs = _mm_sel_l(jnp.where(tril, 1.0, 0.0).astype(BF16), a)
    cs_t = cs.T
    e = e_ref[...]
    dt_e = _mm_sel_r(dt, e, 2)
    cs_e = _mm_sel_r(cs, e, 2)
    cs_last = cs_e[t - 1:t, :]
    xd = xs * dt_e
    ecs = jnp.exp(cs_e)
    xdd = xd * jnp.exp(cs_last - cs_e)
    cdec = jnp.exp(cs_last)
    yskip = xs * dskip_ref[...]
    gate = z_ref[...]

    rr = lax.broadcasted_iota(jnp.int32, (hpg * t, gw), 0) // t
    ll = lax.broadcasted_iota(jnp.int32, (hpg * t, gw), 1) // hd
    bd = rr == ll
    groups = range(ng)
    sls = [slice(g * gw, (g + 1) * gw) for g in groups]
    bgs = [xc[:, di + g * ns:di + (g + 1) * ns].astype(BF16) for g in groups]
    cgs = [xc[:, di + (ng + g) * ns:di + (ng + g + 1) * ns].astype(BF16) for g in groups]
    scores = [_mm_nt(cgs[g], bgs[g]) for g in groups]
    sts = [st[g] for g in groups]
    yoff = [_mm(cgs[g], sts[g]) for g in groups]
    supd = [_mm_tn(bgs[g], xdd[:, sls[g]]) for g in groups]
    ydiag = []
    for g in groups:
        parts = []
        for j in range(hpg):
            h = g * hpg + j
            diff = cs[:, h:h + 1] - cs_t[h:h + 1, :]
            parts.append((scores[g] * jnp.exp(jnp.where(tril, diff, NEG))).astype(BF16))
        lhs = jnp.concatenate(parts, axis=1)
        rhs = jnp.where(bd, jnp.concatenate([xd[:, sls[g]]] * hpg, axis=0), 0.0).astype(BF16)
        ydiag.append(jnp.dot(lhs, rhs, preferred_element_type=F32))
    for g in groups:
        sl = sls[g]
        st[g] = sts[g] * cdec[:, sl] + supd[g]
        y = (ydiag[g] + yoff[g] * ecs[:, sl] + yskip[:, sl]) * gate[:, sl]
        y = y * lax.rsqrt(jnp.mean(y * y, axis=-1, keepdims=True) + RMS_EPS) * gnw_ref[:, sl]
        ybuf[:, sl] = y.astype(BF16)

    out = jnp.dot(ybuf[...], ow_ref[...], preferred_element_type=F32)
    o_ref[...] = x_ref[...] + (1.0 + mod_ref[2:3, :]) * out


def _mamba(x, mod, nw, in_w, dt_w, conv_w, conv_b, dt_bias, a_log, d_skip, norm_w, out_w, layer):
    b, s, d = x.shape
    nh = dt_bias.shape[0]
    di = nh * HEAD_DIM
    cdim = conv_w.shape[1]
    ng, ns = SSD_GROUPS, SSD_STATE
    gw = di // ng
    assert cdim == di + 2 * ng * ns and in_w.shape[2] == di + cdim + nh and nh <= LANE
    t = FFN_ROW_TILE if s % FFN_ROW_TILE == 0 else ROW_TILE
    z, xbc, dtr = pl.pallas_call(
        _m_in_kernel,
        grid=(b, s // t),
        in_specs=[_rows(t, d), _mod_spec(d), _resident((1, d)),
                  _resident_layer((d, di + cdim + nh), layer), _resident_layer((d, LANE), layer)],
        out_specs=[_rows(t, di), _rows(t, cdim), _rows(t, LANE)],
        out_shape=[jax.ShapeDtypeStruct((b, s, di), F32), jax.ShapeDtypeStruct((b, s, cdim), F32),
                   jax.ShapeDtypeStruct((b, s, LANE), F32)],
        compiler_params=_params(False),
        name="m_in",
    )(x, mod, nw.reshape(1, d), in_w, dt_w)

    pad1 = lambda v: jnp.pad(v, (0, LANE - nh)).reshape(1, LANE)
    expand = (jnp.arange(LANE)[:, None] == jnp.arange(di)[None, :] // HEAD_DIM).astype(BF16)
    tc = SSD_CHUNK
    return pl.pallas_call(
        functools.partial(_m_ssd_kernel, nh=nh),
        grid=(b, s // tc),
        in_specs=[_rows(tc, cdim), _rows(tc, di), _rows(tc, LANE), _rows(tc, d), _mod_spec(d),
                  _resident((4, cdim)), _resident((1, cdim)), _resident((1, LANE)), _resident((1, LANE)),
                  _resident((LANE, di)), _resident((1, di)), _resident((1, di)),
                  _resident_layer((di, d), layer)],
        out_specs=_rows(tc, d),
        out_shape=jax.ShapeDtypeStruct(x.shape, F32),
        scratch_shapes=[pltpu.VMEM((tc + 8, cdim), F32), pltpu.VMEM((ng, ns, gw), F32),
                        pltpu.VMEM((tc, di), BF16)],
        compiler_params=_params(True),
        name="m_ssd",
    )(xbc, z, dtr, x, mod, conv_w, conv_b.reshape(1, cdim), pad1(dt_bias), pad1(a_log), expand,
      jnp.repeat(d_skip, HEAD_DIM).reshape(1, di), norm_w.reshape(1, di), out_w)


def _r_in_kernel(x_ref, mod_ref, nw_ref, mix_ref, wrkv_ref, w1_ref, w2_ref, a1_ref, a2_ref,
                 g1_ref, g2_ref, vec_ref, ind_ref, indt_ref,
                 r_ref, lw_ref, k_ref, v_ref, kk_ref, bb_ref, bonus_ref, g_ref, hbuf, dxbuf):
    t = x_ref.shape[0]

    @pl.when(pl.program_id(1) == 0)
    def _():
        hbuf[0:8, :] = jnp.zeros((8, hbuf.shape[1]), F32)

    hbuf[8:8 + t, :] = _norm_mod(x_ref[...], nw_ref[...], mod_ref[1:2, :], mod_ref[0:1, :])
    dxbuf[...] = hbuf[7:7 + t, :] - hbuf[8:8 + t, :]
    hbuf[0:8, :] = hbuf[t:t + 8, :]
    w0, a0, k_k, k_a, r_k = (vec_ref[i:i + 1, :] for i in range(5))
    ind, indt = ind_ref[...], indt_ref[...]
    headsum = lambda q: _mm_sel_r(_mm_sel_r(q, ind, 2), indt, 2)
    dotf = lambda a, b_ref: jnp.dot(a, b_ref[...], preferred_element_type=F32)
    mixed = lambda n: (hbuf[8:8 + t, :] + dxbuf[...] * mix_ref[n:n + 1, :]).astype(BF16)

    r = dotf(mixed(0), wrkv_ref.at[0])
    k = dotf(mixed(1), wrkv_ref.at[1])
    v = dotf(mixed(2), wrkv_ref.at[2])
    wl = dotf(jnp.tanh(dotf(mixed(3), w1_ref)).astype(BF16), w2_ref)
    al = dotf(dotf(mixed(4), a1_ref).astype(BF16), a2_ref)
    r_ref[...] = r
    v_ref[...] = v
    g_ref[...] = dotf(_sigmoid(dotf(mixed(5), g1_ref)).astype(BF16), g2_ref)
    lw_ref[...] = -jnp.exp(-_softplus(-(w0 + wl)) - 0.5)
    a = _sigmoid(a0 + al)
    kk = k * k_k
    kk = kk * lax.rsqrt(jnp.maximum(headsum(kk * kk), KK_EPS * KK_EPS))
    k2 = k * (1.0 + (a - 1.0) * k_a)
    k_ref[...] = k2
    kk_ref[...] = kk
    bb_ref[...] = kk * a
    bonus_ref[...] = headsum(r * k2 * r_k) * v


def _r_wkv_kernel(r_ref, lw_ref, k_ref, v_ref, kk_ref, bb_ref, bonus_ref, g_ref, x_ref, mod_ref,
                  lnw_ref, lnb_ref, ind_ref, indt_ref, ow_ref, o_ref, state, ybuf, *, hd):
    nb, t, d = x_ref.shape
    gw = state.shape[1]
    ngr = d // gw
    hpg = gw // hd
    n = hpg * t

    @pl.when(pl.program_id(1) == 0)
    def _():
        state[...] = jnp.zeros(state.shape, F32)

    tri = jnp.where(_tril(t), 1.0, 0.0).astype(BF16)
    at, rt, kt, bt, v, ptot = [], [], [], [], [], []
    for bi in range(nb):
        lw = lw_ref[bi]
        cl = _mm_sel_l(tri, lw)
        ecl = jnp.exp(cl)
        encl = jnp.exp(-cl)
        rt.append(r_ref[bi] * ecl)
        at.append(-kk_ref[bi] * jnp.exp(cl - lw))
        kt.append(k_ref[bi] * encl)
        bt.append(bb_ref[bi] * encl)
        v.append(v_ref[bi])
        ptot.append(ecl[t - 1:t, :])

    ri = lax.broadcasted_iota(jnp.int32, (n, gw), 0)
    ci = lax.broadcasted_iota(jnp.int32, (n, gw), 1)
    headm = (ri // t) == (ci // hd)
    ri2 = lax.broadcasted_iota(jnp.int32, (n, n), 0)
    ci2 = lax.broadcasted_iota(jnp.int32, (n, n), 1)
    same = (ri2 // t) == (ci2 // t)
    tt = lax.broadcasted_iota(jnp.int32, (t, n), 0)
    ss = lax.broadcasted_iota(jnp.int32, (t, n), 1) % t
    strict = tt > ss
    incl = tt >= ss
    eye = jnp.where(tt == ss, 1.0, 0.0)
    vals_same = (lax.broadcasted_iota(jnp.int32, (gw, gw), 0) // hd) == (
        lax.broadcasted_iota(jnp.int32, (gw, gw), 1) // hd)

    def blockdiag(m, mask):
        return jnp.where(mask, jnp.concatenate([m] * hpg, axis=0), 0.0).astype(BF16)

    groups = range(nb * ngr)
    bis = [g // ngr for g in groups]
    sls = [slice((g % ngr) * gw, (g % ngr + 1) * gw) for g in groups]
    dot = lambda a, b: jnp.dot(a.astype(BF16), b, preferred_element_type=F32)
    ar = [jnp.concatenate([at[bi][:, sl], rt[bi][:, sl]], axis=0) for bi, sl in zip(bis, sls)]
    vst = [blockdiag(v[bi][:, sl], headm) for bi, sl in zip(bis, sls)]
    xbk = [jnp.concatenate([blockdiag(bt[bi][:, sl], headm), blockdiag(kt[bi][:, sl], headm)], axis=0)
           for bi, sl in zip(bis, sls)]
    gram = [_mm_nt(ar[g], xbk[g]) for g in groups]
    hs = [state[g] for g in groups]
    hproj = [_mm_nt(ar[g], hs[g]) for g in groups]
    a_ab = [jnp.where(strict, gram[g][0:t, 0:n], 0.0) for g in groups]
    rhs = [hproj[g][0:t] + dot(jnp.where(strict, gram[g][0:t, n:2 * n], 0.0), vst[g]) for g in groups]
    tinv = [eye + a_ab[g] for g in groups]
    p = [dot(a_ab[g], blockdiag(a_ab[g], same)) for g in groups]
    steps = max(t - 1, 1).bit_length() - 1
    for i in range(steps):
        last = i + 1 == steps
        for g in groups:
            tb = blockdiag(tinv[g], same)
            prod = dot(p[g], tb if last else jnp.concatenate([tb, blockdiag(p[g], same)], axis=1))
            tinv[g] = tinv[g] + prod[:, 0:n]
            if not last:
                p[g] = prod[:, n:2 * n]
    u = [dot(tinv[g], blockdiag(rhs[g], headm)) for g in groups]
    for g in groups:
        r_bk = jnp.where(jnp.concatenate([incl, incl], axis=1), gram[g][t:2 * t, :], 0.0)
        ybuf[bis[g] * t:(bis[g] + 1) * t, sls[g]] = hproj[g][t:2 * t] + dot(
            r_bk, jnp.concatenate([blockdiag(u[g], headm), vst[g]], axis=0))
    for g in groups:
        bi, sl = bis[g], sls[g]
        upd = _mm_tn(jnp.concatenate([u[g], v[bi][:, sl]], axis=0),
                     jnp.concatenate([bt[bi][:, sl], kt[bi][:, sl]], axis=0))
        state[g] = (hs[g] + jnp.where(vals_same, upd, 0.0)) * ptot[bi][:, sl]

    y = ybuf[...]
    ind, indt = ind_ref[...], indt_ref[...]
    headmean = lambda q: _mm_sel_r(_mm_sel_r(q, ind, 2), indt, 2) * (1.0 / hd)
    mu = headmean(y)
    yc = y - mu
    var = headmean(yc * yc)
    yn = yc * lax.rsqrt(var + GN_EPS) * lnw_ref[...] + lnb_ref[...]
    yo = (yn + bonus_ref[...].reshape(nb * t, d)) * g_ref[...].reshape(nb * t, d)
    out = _mm(yo, ow_ref[...])
    for bi in range(nb):
        o_ref[bi] = x_ref[bi] + (1.0 + mod_ref[bi, 2:3, :]) * out[bi * t:(bi + 1) * t]


def _rwkv(x, mod, nw, mix, w_rkv, w0, w1, w2, a0, a1, a2, g1, g2, k_k, k_a, r_k, lnx_w, lnx_b, out_w):
    b, s, d = x.shape
    hd = HEAD_DIM
    assert d % WKV_GROUP == 0 and d // hd <= LANE

    def pad_pair(p1, p2):
        r = p1.shape[1]
        rp = -(-r // LANE) * LANE
        return (jnp.pad(p1, ((0, 0), (0, rp - r))).astype(BF16), jnp.pad(p2, ((0, rp - r), (0, 0))).astype(BF16))

    w1p, w2p = pad_pair(w1, w2)
    a1p, a2p = pad_pair(a1, a2)
    g1p, g2p = pad_pair(g1, g2)
    vecs = jnp.stack([w0, a0, k_k, k_a, r_k.reshape(d), jnp.zeros_like(w0), jnp.zeros_like(w0), jnp.zeros_like(w0)])
    ind, indt = _head_indicator(d, hd)
    wb = w_rkv.astype(BF16)
    t = ROW_TILE
    res = lambda arr: _resident(arr.shape)
    outs = pl.pallas_call(
        _r_in_kernel,
        grid=(b, s // t),
        in_specs=[_rows(t, d), _mod_spec(d), _resident((1, d)), _resident((6, d)), _resident((3, d, d)),
                  res(w1p), res(w2p), res(a1p), res(a2p), res(g1p), res(g2p),
                  _resident((8, d)), res(ind), res(indt)],
        out_specs=[_rows(t, d)] * 8,
        out_shape=[jax.ShapeDtypeStruct((b, s, d), F32)] * 8,
        scratch_shapes=[pltpu.VMEM((t + 8, d), F32), pltpu.VMEM((t, d), F32)],
        compiler_params=_params(True),
        name="r_in",
    )(x, mod, nw.reshape(1, d), mix, wb, w1p, w2p, a1p, a2p, g1p, g2p, vecs, ind, indt)

    tc = WKV_CHUNK
    nb = WKV_BATCH_ROWS if b % WKV_BATCH_ROWS == 0 else 1
    rows = pl.BlockSpec((nb, tc, d), lambda bi, si: (bi, si, 0))
    return pl.pallas_call(
        functools.partial(_r_wkv_kernel, hd=hd),
        grid=(b // nb, s // tc),
        in_specs=[rows] * 9 + [pl.BlockSpec((nb, 6, d), lambda bi, si: (bi, 0, 0)), _resident((1, d)),
                               _resident((1, d)), res(ind), res(indt), _resident((d, d))],
        out_specs=rows,
        out_shape=jax.ShapeDtypeStruct(x.shape, F32),
        scratch_shapes=[pltpu.VMEM((nb * d // WKV_GROUP, WKV_GROUP, WKV_GROUP), F32),
                        pltpu.VMEM((nb * tc, d), F32)],
        compiler_params=_params(True),
        name="r_wkv",
    )(*outs, x, mod, lnx_w.reshape(1, d), lnx_b.reshape(1, d), ind, indt, out_w.astype(BF16))


def _f_in_kernel(x_ref, mod_ref, nw_ref, w_ref, wvt_ref, wf_ref, vec_ref, fb_ref, ind_ref, indt_ref,
                 place_ref, ones_ref, q_ref, k_ref, vt_ref, qa_ref, ka_ref, carry, *, hd):
    t = x_ref.shape[0]

    @pl.when(pl.program_id(1) == 0)
    def _():
        carry[...] = jnp.zeros(carry.shape, F32)

    h = _norm_mod(x_ref[...], nw_ref[...], mod_ref[1:2, :], mod_ref[0:1, :]).astype(BF16)
    ind, indt = ind_ref[...], indt_ref[...]

    def head_rms(q, w):
        ms = _mm_sel_r(_mm_sel_r(q * q, ind, 2), indt, 2) * (1.0 / hd)
        return q * lax.rsqrt(ms + RMS_EPS) * w

    d = x_ref.shape[1]
    q = head_rms(jnp.dot(h, w_ref[:, 0:d], preferred_element_type=F32), vec_ref[0:1, :])
    k = head_rms(jnp.dot(h, w_ref[:, d:2 * d], preferred_element_type=F32), vec_ref[1:2, :])
    q_ref[...] = (q * (hd ** -0.5 * LOG2E)).astype(BF16)
    k_ref[...] = k.astype(BF16)
    vt_ref[...] = _mm_nt(wvt_ref[...], h).astype(BF16)
    f = jnp.dot(h, wf_ref[...], preferred_element_type=F32) + fb_ref[...]
    logf = -_softplus(-f)
    cum = _mm_sel_l(jnp.where(_tril(t), 1.0, 0.0).astype(BF16), logf) + carry[0:1, :]
    carry[...] = jnp.broadcast_to(cum[t - 1:t, :], carry.shape)
    pieces = _split(cum * LOG2E, 3)
    place = lambda first: sum(jnp.dot(p, place_ref[first + i], preferred_element_type=F32)
                              for i, p in enumerate(pieces))
    qa_ref[...] = (place(0) + ones_ref[0:1, :]).astype(BF16)
    ka_ref[...] = (ones_ref[1:2, :] - place(3)).astype(BF16)


def _f_att_kernel(q_ref, k_ref, vt_ref, qa_ref, ka_ref, x_ref, mod_ref, ow_ref, o_ref,
                  obuf, qm_scr, m_scr, acc_scr, *, hd, pairs_per_loop):
    tq = x_ref.shape[0]
    d = x_ref.shape[1]
    tk = vt_ref.shape[2]
    qi = pl.program_id(1)
    lane_l = lax.broadcasted_iota(jnp.int32, (1, LANE), 1)
    first_l = lane_l < hd
    first_r = lax.broadcasted_iota(jnp.int32, (LANE, 1), 0) < hd
    on_or_below = (lax.broadcasted_iota(jnp.int32, (tk, tq), 0)
                   <= lax.broadcasted_iota(jnp.int32, (tk, tq), 1))
    zero_b = jnp.zeros((), BF16)
    one_b = jnp.ones((), BF16)

    lanes = lambda p: slice(p * LANE, (p + 1) * LANE)
    for p0 in range(0, d // LANE, pairs_per_loop):
        pairs = range(p0, p0 + pairs_per_loop)
        heads = [2 * p + i for p in pairs for i in range(2)]
        qa = qa_ref[...]
        for p in pairs:
            qp = q_ref[:, lanes(p)]
            for i, qh in enumerate((jnp.where(first_l, qp, zero_b), jnp.where(first_l, zero_b, qp))):
                h = 2 * p + i
                own = (lane_l >= BIAS_COLS * h) & (lane_l < BIAS_COLS * (h + 1))
                qm_scr[h] = jnp.concatenate([qh, jnp.where(own, qa, zero_b)], axis=1)
        for h in heads:
            m_scr[h] = jnp.full((1, tq), NEG, F32)
            acc_scr[h] = jnp.zeros((LANE, tq), F32)

        def step(j, diagonal, pairs=pairs, heads=heads):
            start = pl.multiple_of(j * tk, tk)
            sc = {}
            kaj = ka_ref[pl.ds(start, tk), :]
            for p in pairs:
                kj = jnp.concatenate([k_ref[pl.ds(start, tk), lanes(p)], kaj], axis=1)
                for h in (2 * p, 2 * p + 1):
                    sc[h] = lax.dot_general(kj, qm_scr[h], (((1,), (1,)), ((), ())), preferred_element_type=F32)
            if diagonal:
                sc = {h: jnp.where(on_or_below, sc[h], NEG) for h in heads}
            m_old = {h: m_scr[h] for h in heads}
            m_new = {h: jnp.maximum(m_old[h], jnp.max(sc[h], axis=0, keepdims=True)) for h in heads}
            pr = {h: jnp.exp2(sc[h] - m_new[h]).astype(BF16) for h in heads}
            for p in pairs:
                vtj = vt_ref[j, lanes(p), :]
                vt2 = (jnp.where(first_r, vtj, one_b), jnp.where(first_r, one_b, vtj))
                for i in range(2):
                    h = 2 * p + i
                    acc_scr[h] = (jnp.exp2(m_old[h] - m_new[h]) * acc_scr[h]
                                  + jnp.dot(vt2[i], pr[h], preferred_element_type=F32))
                    m_scr[h] = m_new[h]

        def body(j, c):
            step(j, False)
            return c

        lax.fori_loop(0, qi, body, 0)
        step(qi, True)
        for p in pairs:
            acc0, acc1 = acc_scr[2 * p], acc_scr[2 * p + 1]
            o_t = jnp.where(first_r, acc0 / acc0[hd:hd + 1, :], acc1 / acc1[0:1, :])
            obuf[:, lanes(p)] = o_t.T.astype(BF16)

    out = jnp.dot(obuf[...], ow_ref[...], preferred_element_type=F32)
    o_ref[...] = x_ref[...] + (1.0 + mod_ref[2:3, :]) * out


def _fox(x, mod, nw, qkvf_w, fgate_b, q_norm_w, k_norm_w, out_w):
    b, s, d = x.shape
    hd = HEAD_DIM
    nh = d // hd
    assert qkvf_w.shape[1] == 3 * d + nh and nh <= LANE and 2 * hd == LANE
    w_all = qkvf_w.astype(BF16)
    wf = jnp.pad(qkvf_w[:, 3 * d:], ((0, 0), (0, LANE - nh))).astype(BF16)
    fb = jnp.pad(fgate_b, (0, LANE - nh)).reshape(1, LANE)
    z = jnp.zeros((d,), F32)
    vecs = jnp.stack([jnp.tile(q_norm_w, nh), jnp.tile(k_norm_w, nh), z, z, z, z, z, z])
    ind, indt = _head_indicator(d, hd)
    t = ROW_TILE
    res = lambda arr: _resident(arr.shape)
    nt = s // t
    assert BIAS_COLS * nh <= LANE
    col = jnp.arange(LANE)[None, None, :]
    place = (col == BIAS_COLS * jnp.arange(LANE)[None, :, None] + jnp.arange(BIAS_COLS)[:, None, None])
    place = (place & (jnp.arange(LANE)[None, :, None] < nh)).astype(BF16)
    third = (jnp.arange(LANE) % BIAS_COLS) // 3
    used = jnp.arange(LANE) < BIAS_COLS * nh
    z1 = jnp.zeros((LANE,), F32)
    ones = jnp.stack([(used & (third == 1)).astype(F32), (used & (third == 0)).astype(F32), z1, z1, z1, z1, z1, z1])
    q, k, vt, qa, ka = pl.pallas_call(
        functools.partial(_f_in_kernel, hd=hd),
        grid=(b, nt),
        in_specs=[_rows(t, d), _mod_spec(d), _resident((1, d)), res(w_all),
                  _resident((d, d)), res(wf), _resident((8, d)), _resident((1, LANE)),
                  res(ind), res(indt), res(place), res(ones)],
        out_specs=[_rows(t, d), _rows(t, d), pl.BlockSpec((None, None, d, t), lambda bi, si: (bi, si, 0, 0)),
                   _rows(t, LANE), _rows(t, LANE)],
        out_shape=[jax.ShapeDtypeStruct((b, s, d), BF16), jax.ShapeDtypeStruct((b, s, d), BF16),
                   jax.ShapeDtypeStruct((b, nt, d, t), BF16), jax.ShapeDtypeStruct((b, s, LANE), BF16),
                   jax.ShapeDtypeStruct((b, s, LANE), BF16)],
        scratch_shapes=[pltpu.VMEM((8, LANE), F32)],
        compiler_params=_params(True),
        name="f_in",
    )(x, mod, nw.reshape(1, d), w_all, w_all[:, 2 * d:3 * d].T, wf, vecs, fb, ind, indt, place, ones)

    whole = lambda *shape: pl.BlockSpec((None,) + shape, lambda bi, si: (bi,) + (0,) * len(shape))
    return pl.pallas_call(
        functools.partial(_f_att_kernel, hd=hd, pairs_per_loop=ATT_PAIRS_PER_LOOP),
        grid=(b, nt),
        in_specs=[_rows(t, d), whole(s, d), whole(nt, d, t), _rows(t, LANE), whole(s, LANE),
                  _rows(t, d), _mod_spec(d), _resident((d, d))],
        out_specs=_rows(t, d),
        out_shape=jax.ShapeDtypeStruct(x.shape, F32),
        scratch_shapes=[pltpu.VMEM((t, d), BF16), pltpu.VMEM((nh, t, 2 * LANE), BF16),
                        pltpu.VMEM((nh, 1, t), F32), pltpu.VMEM((nh, LANE, t), F32)],
        compiler_params=_params(False),
        name="f_att",
    )(q, k, vt, qa, ka, x, mod, out_w.astype(BF16))


def kernel(x, c, ada_w, ada_b, norm1_w, norm2_w, ffn_w1, ffn_w3, ffn_w2, m_in_w, m_conv_w, m_conv_b, m_dt_bias, m_A_log, m_D, m_norm_w, m_out_w, r_mix, r_w_rkv, r_w0, r_w1, r_w2, r_a0, r_a1, r_a2, r_g1, r_g2, r_k_k, r_k_a, r_r_k, r_lnx_w, r_lnx_b, r_out_w, f_qkvf_w, f_fgate_b, f_q_norm_w, f_k_norm_w, f_out_w):
    depth = ada_w.shape[0]
    mod = _ada(c, ada_w, ada_b)
    w1b, w3b, w2b = ffn_w1.astype(BF16), ffn_w3.astype(BF16), ffn_w2.astype(BF16)
    m_nh = m_dt_bias.shape[1]
    m_in_b = m_in_w.astype(BF16)
    m_dt_b = jnp.pad(m_in_w[:, :, m_in_w.shape[2] - m_nh:], ((0, 0), (0, 0), (0, LANE - m_nh))).astype(BF16)
    m_out_b = m_out_w.astype(BF16)
    ia = ib = ic = 0
    for i in range(depth):
        kind = i % 3
        if kind == 0:
            x = _mamba(x, mod[i], norm1_w[i], m_in_b, m_dt_b, m_conv_w[ia], m_conv_b[ia], m_dt_bias[ia],
                       m_A_log[ia], m_D[ia], m_norm_w[ia], m_out_b, ia)
            ia += 1
        elif kind == 1:
            x = _rwkv(x, mod[i], norm1_w[i], r_mix[ib], r_w_rkv[ib], r_w0[ib], r_w1[ib], r_w2[ib], r_a0[ib],
                      r_a1[ib], r_a2[ib], r_g1[ib], r_g2[ib], r_k_k[ib], r_k_a[ib], r_r_k[ib],
                      r_lnx_w[ib], r_lnx_b[ib], r_out_w[ib])
            ib += 1
        else:
            x = _fox(x, mod[i], norm1_w[i], f_qkvf_w[ic], f_fgate_b[ic], f_q_norm_w[ic], f_k_norm_w[ic],
                     f_out_w[ic])
            ic += 1
        x = _ffn(x, mod[i], norm2_w[i], w1b, w3b, w2b, i)
    return x
```
